```python
import math
import jax
import jax.numpy as jnp
from jax import lax
import numpy as np

D_MODEL = 2048
BATCH = 1
SEQ = 16384
DEPTH = 4
DEC_BATCH = 8
DEC_SEQ = 64
PAST_LEN = 2048

CHUNK = 64
N_MIXERS = 4
DN_ALPHA = (2 * DEPTH) ** 0.25
DN_BETA = (8 * DEPTH) ** -0.25
LN_EPS = 1e-5
RMS_EPS = 1e-6

SSD_INNER = 2 * D_MODEL
SSD_HEAD_DIM = 64
SSD_HEADS = SSD_INNER // SSD_HEAD_DIM
SSD_GROUPS = 8
SSD_HPG = SSD_HEADS // SSD_GROUPS
SSD_STATE = 128
SSD_CONV = 4
SSD_CONV_CH = SSD_INNER + 2 * SSD_GROUPS * SSD_STATE
SSD_IN = SSD_INNER + SSD_CONV_CH + SSD_HEADS

SWA_HEAD_DIM = 64
SWA_Q_HEADS = D_MODEL // SWA_HEAD_DIM
SWA_KV_HEADS = 4
SWA_GROUP = SWA_Q_HEADS // SWA_KV_HEADS
WINDOW = 128
WINDOW_CHUNKS = WINDOW // CHUNK
SWA_IN = (SWA_Q_HEADS + 2 * SWA_KV_HEADS) * SWA_HEAD_DIM

S5_INNER = D_MODEL
S5_GROUP_CH = 16
S5_GROUPS = S5_INNER // S5_GROUP_CH
S5_STATE = 64

HG_K = 128
HG_HEADS = D_MODEL // HG_K
HG_V = D_MODEL // HG_HEADS

MOE_GROUPS = 4
MOE_PER_GROUP = 8
MOE_EXPERTS = MOE_GROUPS * MOE_PER_GROUP
MOE_TOPK = 2
D_EXPERT = 256

kernel_name = 'hybrid_stream_encoder_step'
F32 = jnp.float32


def layer_norm(x, g, b):
    xf = x.astype(F32)
    xc = xf - jnp.mean(xf, -1, keepdims=True)
    var = jnp.mean(xc * xc, -1, keepdims=True)
    return (xc * lax.rsqrt(var + LN_EPS) * g.astype(F32) + b.astype(F32)).astype(x.dtype)


def rms_norm(x, w):
    xf = x.astype(F32)
    return xf * lax.rsqrt(jnp.mean(xf * xf, -1, keepdims=True) + RMS_EPS) * w.astype(F32)


def causal_mask(n):
    return jnp.tril(jnp.ones((n, n), bool))


def to_chunks(a):
    b, l = a.shape[:2]
    return jnp.swapaxes(a.reshape((b, l // CHUNK, CHUNK) + a.shape[2:]), 0, 1)


def from_chunks(a):
    a = jnp.swapaxes(a, 0, 1)
    return a.reshape((a.shape[0], a.shape[1] * a.shape[2]) + a.shape[3:])


def chunk_scan(step, state0, xs, prompt):
    if prompt:
        state, ys = lax.scan(step, state0, tuple(to_chunks(a) for a in xs))
        return from_chunks(ys), state
    state, y = step(state0, xs)
    return y, state


def ssd_mixer(x, conv_state, h0, p, prompt):
    w_in, conv_w, conv_b, dt_bias, a_log, d_skip, norm_w, w_out = p
    b, l, _ = x.shape
    z, xbc, dt = jnp.split(x @ w_in, [SSD_INNER, SSD_INNER + SSD_CONV_CH], axis=-1)
    ext = jnp.concatenate([conv_state.astype(xbc.dtype), xbc], axis=1)
    conv = conv_b + sum(ext[:, j:j + l] * conv_w[j] for j in range(SSD_CONV))
    xbc = jax.nn.silu(conv.astype(F32))
    xh, bm, cm = jnp.split(xbc, [SSD_INNER, SSD_INNER + SSD_GROUPS * SSD_STATE], axis=-1)
    xh = xh.reshape(b, l, SSD_GROUPS, SSD_HPG, SSD_HEAD_DIM)
    bm = bm.reshape(b, l, SSD_GROUPS, SSD_STATE)
    cm = cm.reshape(b, l, SSD_GROUPS, SSD_STATE)
    dt = jax.nn.softplus(dt.astype(F32) + dt_bias.astype(F32)).reshape(b, l, SSD_GROUPS, SSD_HPG)
    a = -jnp.exp(a_log.astype(F32)).reshape(SSD_GROUPS, SSD_HPG)

    def step(h, inp):
        xc, dtc, bc, cc = inp
        n = xc.shape[1]
        cum = jnp.cumsum(dtc * a, axis=1)
        seg = jnp.where(causal_mask(n)[None, :, :, None, None], cum[:, :, None] - cum[:, None], -jnp.inf)
        xdt = xc * dtc[..., None]
        cb = jnp.einsum('btgn,bsgn->btsg', cc, bc)
        y = jnp.einsum('btsg,btsgh,bsghp->btghp', cb, jnp.exp(seg), xdt)
        y = y + jnp.einsum('btgn,bghpn->btghp', cc, h) * jnp.exp(cum)[..., None]
        last = cum[:, -1]
        h_new = (jnp.exp(last)[..., None, None] * h
                 + jnp.einsum('bsgn,bsghp->bghpn', bc, xdt * jnp.exp(last[:, None] - cum)[..., None]))
        return h_new, y

    h0 = h0.astype(F32).reshape(b, SSD_GROUPS, SSD_HPG, SSD_HEAD_DIM, SSD_STATE)
    y, h = chunk_scan(step, h0, (xh, dt, bm, cm), prompt)
    y = y + d_skip.astype(F32).reshape(SSD_GROUPS, SSD_HPG, 1) * xh
    y = y.reshape(b, l, SSD_GROUPS, SSD_INNER // SSD_GROUPS)
    y = y * jax.nn.silu(z.astype(F32)).reshape(b, l, SSD_GROUPS, SSD_INNER // SSD_GROUPS)
    y = rms_norm(y, norm_w.reshape(SSD_GROUPS, SSD_INNER // SSD_GROUPS)).reshape(b, l, SSD_INNER)
    out = y.astype(x.dtype) @ w_out
    return out, ext[:, l:], h.reshape(b, SSD_HEADS, SSD_HEAD_DIM, SSD_STATE)


def sink_attention(q, k, v, valid, sinks):
    s = jnp.einsum('bntkgd,bnskd->bnkgts', q.astype(F32), k.astype(F32)) * (SWA_HEAD_DIM ** -0.5)
    s = jnp.where(valid[None, :, None, None, None, :], s, -jnp.inf)
    sink = sinks.astype(F32).reshape(1, 1, SWA_KV_HEADS, SWA_GROUP, 1, 1)
    m = jnp.maximum(jnp.max(s, -1, keepdims=True), sink)
    pr = jnp.exp(s - m)
    pr = pr / (jnp.sum(pr, -1, keepdims=True) + jnp.exp(sink - m))
    return jnp.einsum('bnkgts,bnskd->bntkgd', pr, v.astype(F32))


def swa_mixer(x, k_cache, v_cache, p, prompt):
    w_qkv, sinks, w_out = p
    b, l, _ = x.shape
    q, k, v = jnp.split(x @ w_qkv, [SWA_Q_HEADS * SWA_HEAD_DIM,
                                   (SWA_Q_HEADS + SWA_KV_HEADS) * SWA_HEAD_DIM], axis=-1)
    q = q.reshape(b, l, SWA_KV_HEADS, SWA_GROUP, SWA_HEAD_DIM)
    k = k.reshape(b, l, SWA_KV_HEADS, SWA_HEAD_DIM)
    v = v.reshape(b, l, SWA_KV_HEADS, SWA_HEAD_DIM)
    if prompt:
        nc = l // CHUNK
        qb = q.reshape(b, nc, CHUNK, SWA_KV_HEADS, SWA_GROUP, SWA_HEAD_DIM)
        pad = ((0, 0), (WINDOW_CHUNKS, 0), (0, 0), (0, 0), (0, 0))
        kp = jnp.pad(k.reshape(b, nc, CHUNK, SWA_KV_HEADS, SWA_HEAD_DIM), pad)
        vp = jnp.pad(v.reshape(b, nc, CHUNK, SWA_KV_HEADS, SWA_HEAD_DIM), pad)
        kb = jnp.concatenate([kp[:, j:j + nc] for j in range(WINDOW_CHUNKS + 1)], axis=2)
        vb = jnp.concatenate([vp[:, j:j + nc] for j in range(WINDOW_CHUNKS + 1)], axis=2)
        key_chunk = (jnp.arange(nc)[:, None] - WINDOW_CHUNKS
                     + jnp.arange((WINDOW_CHUNKS + 1) * CHUNK)[None] // CHUNK)
        valid = key_chunk >= 0
        new_k, new_v = k[:, l - WINDOW:], v[:, l - WINDOW:]
    else:
        kf = jnp.concatenate([k_cache.astype(k.dtype), k], axis=1)
        vf = jnp.concatenate([v_cache.astype(v.dtype), v], axis=1)
        qb, kb, vb = q[:, None], kf[:, None], vf[:, None]
        valid = jnp.ones((1, kf.shape[1]), bool)
        new_k, new_v = kf[:, -WINDOW:], vf[:, -WINDOW:]
    o = sink_attention(qb, kb, vb, valid, sinks).reshape(b, l, SWA_Q_HEADS * SWA_HEAD_DIM)
    return o.astype(x.dtype) @ w_out, new_k, new_v


def cplx_combine(e1, e2):
    a1r, a1i, b1r, b1i = e1
    a2r, a2i, b2r, b2i = e2
    return (a2r * a1r - a2i * a1i, a2r * a1i + a2i * a1r,
            a2r * b1r - a2i * b1i + b2r, a2r * b1i + a2i * b1r + b2i)


def s5_mixer(x, s_re, s_im, p, prompt):
    w_in, a_re, a_im, log_dt, b_re, b_im, c_re, c_im, d_skip, w_glu = p
    b, l, _ = x.shape
    u = (x @ w_in).astype(F32).reshape(b, l, S5_GROUPS, S5_GROUP_CH)
    lr, li = a_re.astype(F32), a_im.astype(F32)
    dt = jnp.exp(log_dt.astype(F32))[:, None]
    mag = jnp.exp(lr * dt)
    ab_r, ab_i = mag * jnp.cos(li * dt), mag * jnp.sin(li * dt)
    den = lr * lr + li * li
    co_r = ((ab_r - 1.0) * lr + ab_i * li) / den
    co_i = (ab_i * lr - (ab_r - 1.0) * li) / den
    br_, bi_ = b_re.astype(F32), b_im.astype(F32)
    bb_r = co_r[..., None] * br_ - co_i[..., None] * bi_
    bb_i = co_r[..., None] * bi_ + co_i[..., None] * br_
    bu_r = jnp.einsum('blgj,gnj->blgn', u, bb_r)
    bu_i = jnp.einsum('blgj,gnj->blgn', u, bb_i)
    cr, ci = c_re.astype(F32), c_im.astype(F32)

    def step(s, inp):
        sr, si = s
        ur, ui = inp
        ur = ur.at[:, 0].add(ab_r * sr - ab_i * si)
        ui = ui.at[:, 0].add(ab_r * si + ab_i * sr)
        ar = jnp.broadcast_to(ab_r, ur.shape)
        ai = jnp.broadcast_to(ab_i, ur.shape)
        _, _, hr, hi = lax.associative_scan(cplx_combine, (ar, ai, ur, ui), axis=1)
        y = jnp.einsum('gjn,blgn->blgj', cr, hr) - jnp.einsum('gjn,blgn->blgj', ci, hi)
        return (hr[:, -1], hi[:, -1]), y

    y, (s_re, s_im) = chunk_scan(step, (s_re.astype(F32), s_im.astype(F32)), (bu_r, bu_i), prompt)
    y = y + d_skip.astype(F32).reshape(S5_GROUPS, S5_GROUP_CH) * u
    y = jax.nn.gelu(y).reshape(b, l, S5_INNER)
    val, gate = jnp.split(y.astype(x.dtype) @ w_glu, 2, axis=-1)
    return val * jax.nn.sigmoid(gate), s_re, s_im


def hgrn_mixer(x, s0, layer, p, prompt):
    w_in, lb_param, norm_w, w_out = p
    b, l, _ = x.shape
    q, fz, i, g = jnp.split(x @ w_in, 4, axis=-1)
    lbs = jax.nn.softmax(lb_param.astype(F32), axis=0)
    lb = (jnp.cumsum(lbs, axis=0) - lbs[0])[layer].reshape(HG_HEADS, HG_K)
    shp = (b, l, HG_HEADS, HG_K)
    q = jax.nn.silu(q.astype(F32)).reshape(shp)
    logf = jnp.logaddexp(jnp.log(lb), jnp.log1p(-lb) + jax.nn.log_sigmoid(fz.astype(F32).reshape(shp)))
    k = -jnp.expm1(logf)
    v = i.astype(F32).reshape(b, l, HG_HEADS, HG_V)

    def step(S, inp):
        qc, kc, vc, fc = inp
        n = qc.shape[1]
        cum = jnp.cumsum(fc, axis=1)
        dec = jnp.exp(jnp.where(causal_mask(n)[None, :, :, None, None],
                                cum[:, :, None] - cum[:, None], -jnp.inf))
        att = jnp.einsum('bthk,bshk,btshk->bhts', qc, kc, dec)
        o = (jnp.einsum('bhts,bshv->bthv', att, vc)
             + jnp.einsum('bthk,bhkv->bthv', qc * jnp.exp(cum), S))
        last = cum[:, -1]
        S_new = (jnp.exp(last)[..., None] * S
                 + jnp.einsum('bshk,bshv->bhkv', kc * jnp.exp(last[:, None] - cum), vc))
        return S_new, o

    o, S = chunk_scan(step, s0.astype(F32), (q, k, v, logf), prompt)
    o = rms_norm(o, norm_w) * jax.nn.silu(g.astype(F32)).reshape(b, l, HG_HEADS, HG_V)
    return o.reshape(b, l, D_MODEL).astype(x.dtype) @ w_out, S


def hier_moe(x, w_rg, b_rg, w_re, b_re, w_gate, w_up, w_down):
    b, l, d = x.shape
    t = x.reshape(b * l, d)
    pg = jax.nn.softmax((t @ w_rg + b_rg).astype(F32), axis=-1)
    gsel = jnp.argmax(pg, axis=-1)
    gprob = jnp.max(pg, axis=-1)
    gmask = jax.nn.one_hot(gsel, MOE_GROUPS, dtype=F32)
    le = (t @ w_re + b_re).astype(F32).reshape(-1, MOE_GROUPS, MOE_PER_GROUP)
    pe = jax.nn.softmax(jnp.einsum('tge,tg->te', le, gmask), axis=-1)
    top_v, top_i = lax.top_k(pe, MOE_TOPK)
    top_v = top_v / jnp.sum(top_v, -1, keepdims=True)
    w_grp = jnp.einsum('tk,tke->te', top_v, jax.nn.one_hot(top_i, MOE_PER_GROUP, dtype=F32)) * gprob[:, None]
    y = jnp.zeros_like(t)
    for gi in range(MOE_GROUPS):
        sl = slice(gi * MOE_PER_GROUP, (gi + 1) * MOE_PER_GROUP)
        wgt = (w_grp * gmask[:, gi:gi + 1]).astype(t.dtype)
        h = jax.nn.silu(jnp.einsum('td,edf->tef', t, w_gate[sl])) * jnp.einsum('td,edf->tef', t, w_up[sl])
        y = y + jnp.einsum('tef,te,efd->td', h, wgt, w_down[sl])
    return y.reshape(b, l, d)


def run_trunk(x, states, ssd_p, swa_p, s5_p, hg_p, ln_p, moe_p, prompt):
    conv_st, ssd_h, k_c, v_c, s5_r, s5_i, hg_s = states
    out = [conv_st, ssd_h, k_c, v_c, s5_r, s5_i, hg_s]
    ln1_g, ln1_b, ln2_g, ln2_b = ln_p
    for layer in range(DEPTH):
        kind = layer % N_MIXERS
        if kind == 0:
            h, out[0], out[1] = ssd_mixer(x, conv_st, ssd_h, ssd_p, prompt)
        elif kind == 1:
            h, out[2], out[3] = swa_mixer(x, k_c, v_c, swa_p, prompt)
        elif kind == 2:
            h, out[4], out[5] = s5_mixer(x, s5_r, s5_i, s5_p, prompt)
        else:
            h, out[6] = hgrn_mixer(x, hg_s, layer, hg_p, prompt)
        x = layer_norm(DN_ALPHA * x + h, ln1_g[layer], ln1_b[layer])
        f = hier_moe(x, *[w[layer] for w in moe_p])
        x = layer_norm(DN_ALPHA * x + f, ln2_g[layer], ln2_b[layer])
    return x, out


def setup_inputs(seed: int = 0) -> dict:
    key = jax.random.key(seed)
    ks = jax.random.split(key, 64)
    it = iter(range(64))

    def nrm(shape, scale=1.0):
        return scale * jax.random.normal(ks[next(it)], shape, F32)

    def uni(shape, lo, hi):
        return jax.random.uniform(ks[next(it)], shape, F32, lo, hi)

    dt0 = jnp.exp(uni((SSD_HEADS,), math.log(1e-3), math.log(1e-1)))
    s5_w_glu = jnp.concatenate([nrm((S5_INNER, D_MODEL), S5_INNER ** -0.5 * DN_BETA),
                                nrm((S5_INNER, D_MODEL), S5_INNER ** -0.5)], axis=1)
    return {
        'x_prompt': nrm((BATCH, SEQ, D_MODEL)),
        'x_sample': nrm((DEC_BATCH, DEC_SEQ, D_MODEL)),
        'state_ssd_conv': nrm((DEC_BATCH, SSD_CONV - 1, SSD_CONV_CH)),
        'state_ssd': nrm((DEC_BATCH, SSD_HEADS, SSD_HEAD_DIM, SSD_STATE), 0.5),
        'cache_swa_k': nrm((DEC_BATCH, WINDOW, SWA_KV_HEADS, SWA_HEAD_DIM)),
        'cache_swa_v': nrm((DEC_BATCH, WINDOW, SWA_KV_HEADS, SWA_HEAD_DIM)),
        'state_s5_re': nrm((DEC_BATCH, S5_GROUPS, S5_STATE), 0.5),
        'state_s5_im': nrm((DEC_BATCH, S5_GROUPS, S5_STATE), 0.5),
        'state_hgrn': nrm((DEC_BATCH, HG_HEADS, HG_K, HG_V), 0.5),
        'ssd_w_in': nrm((D_MODEL, SSD_IN), D_MODEL ** -0.5),
        'ssd_conv_w': nrm((SSD_CONV, SSD_CONV_CH), SSD_CONV ** -0.5),
        'ssd_conv_b': nrm((SSD_CONV_CH,), 0.01),
        'ssd_dt_bias': dt0 + jnp.log(-jnp.expm1(-dt0)),
        'ssd_a_log': jnp.log(uni((SSD_HEADS,), 1.0, 16.0)),
        'ssd_d': 1.0 + nrm((SSD_HEADS,), 0.1),
        'ssd_norm_w': 1.0 + nrm((SSD_INNER,), 0.02),
        'ssd_w_out': nrm((SSD_INNER, D_MODEL), SSD_INNER ** -0.5 * DN_BETA),
        'swa_w_qkv': nrm((D_MODEL, SWA_IN), D_MODEL ** -0.5),
        'swa_sinks': nrm((SWA_Q_HEADS,)),
        'swa_w_out': nrm((SWA_Q_HEADS * SWA_HEAD_DIM, D_MODEL), D_MODEL ** -0.5 * DN_BETA),
        's5_w_in': nrm((D_MODEL, S5_INNER), D_MODEL ** -0.5),
        's5_a_re': -0.5 + nrm((S5_GROUPS, S5_STATE), 0.01),
        's5_a_im': jnp.pi * jnp.arange(S5_STATE, dtype=F32)[None, :] + nrm((S5_GROUPS, S5_STATE), 0.01),
        's5_log_dt': uni((S5_GROUPS,), math.log(1e-3), math.log(1e-1)),
        's5_b_re': nrm((S5_GROUPS, S5_STATE, S5_GROUP_CH), (2.0 * S5_GROUP_CH) ** -0.5),
        's5_b_im': nrm((S5_GROUPS, S5_STATE, S5_GROUP_CH), (2.0 * S5_GROUP_CH) ** -0.5),
        's5_c_re': nrm((S5_GROUPS, S5_GROUP_CH, S5_STATE), (2.0 * S5_STATE) ** -0.5),
        's5_c_im': nrm((S5_GROUPS, S5_GROUP_CH, S5_STATE), (2.0 * S5_STATE) ** -0.5),
        's5_d': nrm((S5_INNER,)),
        's5_w_glu': s5_w_glu,
        'hg_w_in': nrm((D_MODEL, 4 * D_MODEL), D_MODEL ** -0.5),
        'hg_lb': nrm((DEPTH, HG_HEADS * HG_K)),
        'hg_norm_w': 1.0 + nrm((HG_V,), 0.02),
        'hg_w_out': nrm((D_MODEL, D_MODEL), D_MODEL ** -0.5 * DN_BETA),
        'ln1_g': 1.0 + nrm((DEPTH, D_MODEL), 0.02),
        'ln1_b': nrm((DEPTH, D_MODEL), 0.02),
        'ln2_g': 1.0 + nrm((DEPTH, D_MODEL), 0.02),
        'ln2_b': nrm((DEPTH, D_MODEL), 0.02),
        'moe_w_rg': nrm((DEPTH, D_MODEL, MOE_GROUPS), D_MODEL ** -0.5),
        'moe_b_rg': nrm((DEPTH, MOE_GROUPS), 0.01),
        'moe_w_re': nrm((DEPTH, D_MODEL, MOE_EXPERTS), D_MODEL ** -0.5),
        'moe_b_re': nrm((DEPTH, MOE_EXPERTS), 0.01),
        'moe_w_gate': nrm((DEPTH, MOE_EXPERTS, D_MODEL, D_EXPERT), D_MODEL ** -0.5),
        'moe_w_up': nrm((DEPTH, MOE_EXPERTS, D_MODEL, D_EXPERT), D_MODEL ** -0.5),
        'moe_w_down': nrm((DEPTH, MOE_EXPERTS, D_EXPERT, D_MODEL), D_EXPERT ** -0.5 * DN_BETA),
    }


def reference(x_prompt, x_sample, state_ssd_conv, state_ssd, cache_swa_k, cache_swa_v,
              state_s5_re, state_s5_im, state_hgrn,
              ssd_w_in, ssd_conv_w, ssd_conv_b, ssd_dt_bias, ssd_a_log, ssd_d, ssd_norm_w, ssd_w_out,
              swa_w_qkv, swa_sinks, swa_w_out,
              s5_w_in, s5_a_re, s5_a_im, s5_log_dt, s5_b_re, s5_b_im, s5_c_re, s5_c_im, s5_d, s5_w_glu,
              hg_w_in, hg_lb, hg_norm_w, hg_w_out,
              ln1_g, ln1_b, ln2_g, ln2_b,
              moe_w_rg, moe_b_rg, moe_w_re, moe_b_re, moe_w_gate, moe_w_up, moe_w_down):
    ssd_p = (ssd_w_in, ssd_conv_w, ssd_conv_b, ssd_dt_bias, ssd_a_log, ssd_d, ssd_norm_w, ssd_w_out)
    swa_p = (swa_w_qkv, swa_sinks, swa_w_out)
    s5_p = (s5_w_in, s5_a_re, s5_a_im, s5_log_dt, s5_b_re, s5_b_im, s5_c_re, s5_c_im, s5_d, s5_w_glu)
    hg_p = (hg_w_in, hg_lb, hg_norm_w, hg_w_out)
    ln_p = (ln1_g, ln1_b, ln2_g, ln2_b)
    moe_p = (moe_w_rg, moe_b_rg, moe_w_re, moe_b_re, moe_w_gate, moe_w_up, moe_w_down)

    b = x_prompt.shape[0]
    init = (jnp.zeros((b, SSD_CONV - 1, SSD_CONV_CH), x_prompt.dtype),
            jnp.zeros((b, SSD_HEADS, SSD_HEAD_DIM, SSD_STATE), F32),
            None, None,
            jnp.zeros((b, S5_GROUPS, S5_STATE), F32),
            jnp.zeros((b, S5_GROUPS, S5_STATE), F32),
            jnp.zeros((b, HG_HEADS, HG_K, HG_V), F32))
    y_prompt, ps = run_trunk(x_prompt, init, ssd_p, swa_p, s5_p, hg_p, ln_p, moe_p, True)

    cached = (state_ssd_conv, state_ssd, cache_swa_k, cache_swa_v, state_s5_re, state_s5_im, state_hgrn)
    y_sample, ss = run_trunk(x_sample, cached, ssd_p, swa_p, s5_p, hg_p, ln_p, moe_p, False)

    return (y_prompt, y_sample,
            ps[0], ps[1], ps[2], ps[3], ps[4], ps[5], ps[6],
            ss[0], ss[1], ss[2], ss[3], ss[4], ss[5], ss[6])
```

```python
import functools

import jax
import jax.numpy as jnp
from jax import lax
from jax.experimental import pallas as pl
from jax.experimental.pallas import tpu as pltpu

F32 = jnp.float32
BF16 = jnp.bfloat16
HIGHEST = lax.Precision.HIGHEST

D = 2048
DEPTH = 4
DN_ALPHA = (2 * DEPTH) ** 0.25
LN_EPS = 1e-5
RMS_EPS = 1e-6
NEG = -1e30

VMEM_LIMIT = 56 * 1024 * 1024

SSD_INNER = 4096
SSD_HEADS = 64
SSD_G = 8
SSD_HPG = 8
SSD_N = 128
SSD_GW = SSD_INNER // SSD_G
SSD_CONV = 4
SSD_XBC = SSD_INNER + 2 * SSD_G * SSD_N

SWA_DH = 64
SWA_QH = 32
SWA_KVH = 4
SWA_GRP = SWA_QH // SWA_KVH
WINDOW = 128
CHUNK = 64

S5_G = 128
S5_CH = 16
S5_N = 64
S5_SUB = 16
S5_PAIRS = S5_G // 2
S5_PB = 4

HG_H = 16
HG_K = 128
HG_LEAF = 16

MOE_G = 4
MOE_PG = 8
MOE_E = 32
D_EXPERT = 256
MOE_TM = 256


def _cparams(sem):
    return pltpu.CompilerParams(dimension_semantics=sem, vmem_limit_bytes=VMEM_LIMIT)


def _sigmoid(x):
    return 1.0 / (1.0 + jnp.exp(-x))


def _silu(x):
    return x * _sigmoid(x)


def _softplus(x):
    return jnp.maximum(x, 0.0) + jnp.log(1.0 + jnp.exp(-jnp.abs(x)))


def _dot(a, b, precision=None):
    return jnp.dot(a, b, preferred_element_type=F32, precision=precision)


def _dot_nt(a, b):
    return lax.dot_general(a, b, (((1,), (1,)), ((), ())), preferred_element_type=F32)


def _tri(n, upper=False):
    r = lax.broadcasted_iota(jnp.int32, (n, n), 0)
    c = lax.broadcasted_iota(jnp.int32, (n, n), 1)
    return (r <= c) if upper else (c <= r)


def _transpose_rows(x):
    c = x.shape[0]
    if c == 128:
        return x.T
    pad = jnp.zeros((128 - c, 128), x.dtype)
    return jnp.concatenate([x, pad], axis=0).T[:, :c]


def _mm_body(x_ref, w_ref, o_ref, wb_ref, *, tiled_out):
    @pl.when(pl.program_id(1) == 0)
    def _():
        wb_ref[...] = w_ref[...].astype(BF16)

    r = _dot(x_ref[...], wb_ref[...]).astype(o_ref.dtype)
    if tiled_out:
        o_ref[0] = r
    else:
        o_ref[...] = r


def matmul(x, w, *, col0=0, ncols=None, tn=512, tm=512, out_dtype=F32, tiled_out=False):
    m, k = x.shape
    ncols = w.shape[1] - col0 if ncols is None else ncols
    tm = min(tm, m)
    assert col0 % tn == 0 and ncols % tn == 0 and m % tm == 0
    nj = ncols // tn
    j0 = col0 // tn
    if tiled_out:
        out_shape = jax.ShapeDtypeStruct((nj, m, tn), out_dtype)
        out_spec = pl.BlockSpec((1, tm, tn), lambda j, i: (j, i, 0))
    else:
        out_shape = jax.ShapeDtypeStruct((m, ncols), out_dtype)
        out_spec = pl.BlockSpec((tm, tn), lambda j, i: (i, j))
    return pl.pallas_call(
        functools.partial(_mm_body, tiled_out=tiled_out),
        grid=(nj, m // tm),
        in_specs=[pl.BlockSpec((tm, k), lambda j, i: (i, 0)),
                  pl.BlockSpec((k, tn), lambda j, i: (0, j + j0))],
        out_specs=out_spec,
        out_shape=out_shape,
        scratch_shapes=[pltpu.VMEM((k, tn), BF16)],
        compiler_params=_cparams(("arbitrary", "arbitrary")),
    )(x, w)


def _glu_body(x_ref, wv_ref, wg_ref, o_ref, wvb_ref, wgb_ref):
    @pl.when(pl.program_id(1) == 0)
    def _():
        wvb_ref[...] = wv_ref[...].astype(BF16)
        wgb_ref[...] = wg_ref[...].astype(BF16)

    x = x_ref[...]
    o_ref[...] = _dot(x, wvb_ref[...]) * _sigmoid(_dot(x, wgb_ref[...]))


def glu_matmul(x, w, *, tn=512, tm=512):
    m, k = x.shape
    n = w.shape[1] // 2
    tm = min(tm, m)
    nj = n // tn
    return pl.pallas_call(
        _glu_body,
        grid=(nj, m // tm),
        in_specs=[pl.BlockSpec((tm, k), lambda j, i: (i, 0)),
                  pl.BlockSpec((k, tn), lambda j, i: (0, j)),
                  pl.BlockSpec((k, tn), lambda j, i: (0, j + nj))],
        out_specs=pl.BlockSpec((tm, tn), lambda j, i: (i, j)),
        out_shape=jax.ShapeDtypeStruct((m, n), F32),
        scratch_shapes=[pltpu.VMEM((k, tn), BF16), pltpu.VMEM((k, tn), BF16)],
        compiler_params=_cparams(("arbitrary", "arbitrary")),
    )(x, w, w)


def _layer_norm(v, g, b):
    mu = jnp.mean(v, axis=-1, keepdims=True)
    vc = v - mu
    var = jnp.mean(vc * vc, axis=-1, keepdims=True)
    return vc * lax.rsqrt(var + LN_EPS) * g + b


def _route(x, wr, br):
    lg = _dot(x, wr, precision=HIGHEST) + br
    lane = lax.broadcasted_iota(jnp.int32, lg.shape, 1)
    lanef = lane.astype(F32)
    big = jnp.float32(1e9)
    is_g = lane < MOE_G
    gl = jnp.where(is_g, lg, NEG)
    gmax = jnp.max(gl, axis=-1, keepdims=True)
    gsel = jnp.min(jnp.where(is_g & (gl == gmax), lanef, big), axis=-1, keepdims=True)
    gprob = 1.0 / jnp.sum(jnp.where(is_g, jnp.exp(gl - gmax), 0.0), axis=-1, keepdims=True)
    lo = MOE_G + MOE_PG * gsel
    is_e = (lanef >= lo) & (lanef < lo + MOE_PG)
    el = jnp.where(is_e, lg, NEG)
    m1 = jnp.max(el, axis=-1, keepdims=True)
    l1 = jnp.min(jnp.where(is_e & (el == m1), lanef, big), axis=-1, keepdims=True)
    is_e2 = is_e & (lanef != l1)
    el2 = jnp.where(is_e2, lg, NEG)
    m2 = jnp.max(el2, axis=-1, keepdims=True)
    l2 = jnp.min(jnp.where(is_e2 & (el2 == m2), lanef, big), axis=-1, keepdims=True)
    r = jnp.exp(m2 - m1)
    w1 = gprob / (1.0 + r)
    w2 = gprob * r / (1.0 + r)
    out = jnp.where(lane == 0, l1 - MOE_G,
                    jnp.where(lane == 1, l2 - MOE_G,
                              jnp.where(lane == 2, w1, jnp.where(lane == 3, w2, 0.0))))
    return out


def _add_ln_body(x_ref, h_ref, g_ref, b_ref, *rest, route):
    y = _layer_norm(DN_ALPHA * x_ref[...] + h_ref[...], g_ref[...], b_ref[...])
    if route:
        wr_ref, br_ref, o_ref, ob_ref, r_ref = rest
        r_ref[...] = _route(y, wr_ref[...], br_ref[...])
    else:
        o_ref, ob_ref = rest
    o_ref[...] = y
    ob_ref[...] = y.astype(BF16)


def add_ln(x, h, g, b, router=None, tm=256):
    m = x.shape[0]
    row = pl.BlockSpec((tm, D), lambda i: (i, 0))
    vec = pl.BlockSpec((1, D), lambda i: (0, 0))
    in_specs = [row, row, vec, vec]
    args = [x, h, g.reshape(1, D), b.reshape(1, D)]
    out_specs = [row, row]
    out_shape = [jax.ShapeDtypeStruct((m, D), F32), jax.ShapeDtypeStruct((m, D), BF16)]
    if router is not None:
        in_specs += [pl.BlockSpec((D, 128), lambda i: (0, 0)), pl.BlockSpec((1, 128), lambda i: (0, 0))]
        args += list(router)
        out_specs.append(pl.BlockSpec((tm, 128), lambda i: (i, 0)))
        out_shape.append(jax.ShapeDtypeStruct((m, 128), F32))
    return pl.pallas_call(
        functools.partial(_add_ln_body, route=router is not None),
        grid=(m // tm,),
        in_specs=in_specs, out_specs=out_specs, out_shape=out_shape,
        compiler_params=_cparams(("arbitrary",)),
    )(*args)


def _ssd_body(z_ref, x_ref, b_ref, c_ref, dtg_ref, dtt_ref, csx_ref, csb_ref, csc_ref, h0_ref,
              cwx_ref, cwb_ref, cwc_ref, cbx_ref, cbb_ref, cbc_ref, dtbr_ref, dtbc_ref,
              alr_ref, alc_ref, dsk_ref, nw_ref, *rest, C, aliased):
    if aliased:
        rest = rest[1:]
    (y_ref, cox_ref, cob_ref, coc_ref, ho_ref,
     tx_ref, tb_ref, tc_ref, ex_ref, eb_ref, ec_ref, ht_ref, yb_ref) = rest
    c = pl.program_id(1)
    last_chunk = c == pl.num_programs(1) - 1

    @pl.when(c == 0)
    def _init():
        tx_ref[...] = jnp.zeros_like(tx_ref)
        tb_ref[...] = jnp.zeros_like(tb_ref)
        tc_ref[...] = jnp.zeros_like(tc_ref)
        for g in range(SSD_G):
            tx_ref[g, 5:8, :] = csx_ref[0, g]
            tb_ref[g, 5:8, :] = csb_ref[0, g]
            tc_ref[g, 5:8, :] = csc_ref[0, g]
        for p in range(SSD_HEADS // 2):
            ht_ref[p] = h0_ref[0, p].T

    tril = _tri(C)
    tril_f = tril.astype(F32)
    triu_f = _tri(C, upper=True).astype(F32)
    lane = lax.broadcasted_iota(jnp.int32, (C, 128), 1)
    left = lane < 64

    def conv(e_ref, t_ref, raw, w_ref, bias_ref, g):
        e_ref[0:8, :] = t_ref[g]
        e_ref[8:8 + C, :] = raw
        w = w_ref[g]
        acc = bias_ref[g] + w[3:4, :] * raw
        for k in range(SSD_CONV - 1):
            acc = acc + w[k:k + 1, :] * e_ref[5 + k:5 + k + C, :]
        t_ref[g] = e_ref[C:C + 8, :]
        return _silu(acc)

    def group(g, carry):
        xs = conv(ex_ref, tx_ref, x_ref[g], cwx_ref, cbx_ref, g)
        bs = conv(eb_ref, tb_ref, b_ref[g], cwb_ref, cbb_ref, g)
        cs = conv(ec_ref, tc_ref, c_ref[g], cwc_ref, cbc_ref, g)

        @pl.when(last_chunk)
        def _():
            cox_ref[0, g] = ex_ref[C + 5:C + 8, :]
            cob_ref[0, g] = eb_ref[C + 5:C + 8, :]
            coc_ref[0, g] = ec_ref[C + 5:C + 8, :]

        dtv = _softplus(dtg_ref[0, g] + dtbr_ref[g])
        dtvt = _softplus(dtt_ref[0, g] + dtbc_ref[g])
        cum = _dot(tril_f, dtv * (-jnp.exp(alr_ref[g])), precision=HIGHEST)
        cumt = _dot(dtvt * (-jnp.exp(alc_ref[g])), triu_f, precision=HIGHEST)
        bsb = bs.astype(BF16)
        csb = cs.astype(BF16)
        cb = _dot_nt(csb, bsb)
        bst = _transpose_rows(bs).astype(BF16)
        dsk = dsk_ref[g]
        ys = []
        for j in range(SSD_HPG // 2):
            h0, h1 = 2 * j, 2 * j + 1
            c0, c1 = cum[:, h0:h0 + 1], cum[:, h1:h1 + 1]
            l0 = jnp.where(tril, jnp.exp(c0 - cumt[h0:h0 + 1, :]), 0.0) * cb
            l1 = jnp.where(tril, jnp.exp(c1 - cumt[h1:h1 + 1, :]), 0.0) * cb
            lhs = jnp.concatenate([l0, l1], axis=1).astype(BF16)
            xp = xs[:, 128 * j:128 * (j + 1)]
            xdt = xp * jnp.where(left, dtv[:, h0:h0 + 1], dtv[:, h1:h1 + 1])
            rhs = jnp.concatenate([jnp.where(left, xdt, 0.0), jnp.where(left, 0.0, xdt)],
                                  axis=0).astype(BF16)
            htp = ht_ref[g * 4 + j]
            y = _dot(lhs, rhs)
            y = y + _dot(csb, htp.astype(BF16)) * jnp.where(left, jnp.exp(c0), jnp.exp(c1))
            e0, e1 = cum[C - 1:C, h0:h0 + 1], cum[C - 1:C, h1:h1 + 1]
            wgt = (xdt * jnp.where(left, jnp.exp(e0 - c0), jnp.exp(e1 - c1))).astype(BF16)
            ht_ref[g * 4 + j] = jnp.where(left[0:1, :], jnp.exp(e0), jnp.exp(e1)) * htp + _dot(bst, wgt)
            ys.append(y + dsk[:, 128 * j:128 * (j + 1)] * xp)
        y = jnp.concatenate(ys, axis=1) * _silu(z_ref[g])
        ms = jnp.mean(y * y, axis=-1, keepdims=True)
        yb_ref[g] = (y * lax.rsqrt(ms + RMS_EPS) * nw_ref[g]).astype(BF16)
        return carry

    lax.fori_loop(0, SSD_G, group, 0)
    for g in range(SSD_G):
        y_ref[:, SSD_GW * g:SSD_GW * (g + 1)] = yb_ref[g]

    @pl.when(last_chunk)
    def _fin():
        for p in range(SSD_HEADS // 2):
            ho_ref[0, p] = ht_ref[p].T


def ssd_core(zx, bc, dt, conv_state, h0, params, *, row0, streams, length, chunk, y_prev=None):
    conv_w, conv_b, dt_bias, a_log, d_skip, norm_w = params
    S, L, C = streams, length, chunk
    nch = L // C
    rb0 = row0 // C
    t_all = zx.shape[1]
    dseg = dt[row0:row0 + S * L].reshape(S, L, SSD_G, SSD_HPG)
    dtg = dseg.transpose(0, 2, 1, 3)
    dtt = dseg.transpose(0, 2, 3, 1)

    def split(a, lead):
        ax = a[..., :SSD_INNER].reshape(lead + (SSD_G, SSD_GW))
        ab = a[..., SSD_INNER:SSD_INNER + SSD_G * SSD_N].reshape(lead + (SSD_G, SSD_N))
        ac = a[..., SSD_INNER + SSD_G * SSD_N:].reshape(lead + (SSD_G, SSD_N))
        return ax, ab, ac

    csx, csb, csc = (jnp.moveaxis(a, 2, 1) for a in split(conv_state, (S, SSD_CONV - 1)))
    cwx, cwb, cwc = (jnp.moveaxis(a, 1, 0) for a in split(conv_w, (SSD_CONV,)))
    cbx, cbb, cbc = (jnp.moveaxis(a, 1, 0) for a in split(conv_b.reshape(1, -1), (1,)))
    dtbr = dt_bias.reshape(SSD_G, 1, SSD_HPG)
    dtbc = dt_bias.reshape(SSD_G, SSD_HPG, 1)
    alr = a_log.reshape(SSD_G, 1, SSD_HPG)
    alc = a_log.reshape(SSD_G, SSD_HPG, 1)
    dsk = jnp.repeat(d_skip, SSD_INNER // SSD_HEADS).reshape(SSD_G, 1, SSD_GW)
    nw = norm_w.reshape(SSD_G, 1, SSD_GW)
    h0p = h0.reshape(S, SSD_HEADS // 2, 128, SSD_N)

    def rb(s, c):
        return rb0 + s * nch + c

    def full(a):
        nd = a.ndim
        return pl.BlockSpec(a.shape, lambda s, c: (0,) * nd)

    def per_stream(a):
        nd = a.ndim
        return pl.BlockSpec((1,) + a.shape[1:], lambda s, c: (s,) + (0,) * (nd - 1))

    in_specs = [
        pl.BlockSpec((SSD_G, C, SSD_GW), lambda s, c: (0, rb(s, c), 0)),
        pl.BlockSpec((SSD_G, C, SSD_GW), lambda s, c: (1, rb(s, c), 0)),
        pl.BlockSpec((SSD_G, C, SSD_N), lambda s, c: (0, rb(s, c), 0)),
        pl.BlockSpec((SSD_G, C, SSD_N), lambda s, c: (1, rb(s, c), 0)),
        pl.BlockSpec((1, SSD_G, C, SSD_HPG), lambda s, c: (s, 0, c, 0)),
        pl.BlockSpec((1, SSD_G, SSD_HPG, C), lambda s, c: (s, 0, 0, c)),
        per_stream(csx), per_stream(csb), per_stream(csc), per_stream(h0p),
        full(cwx), full(cwb), full(cwc), full(cbx), full(cbb), full(cbc),
        full(dtbr), full(dtbc), full(alr), full(alc), full(dsk), full(nw),
    ]
    args = [zx, zx, bc, bc, dtg, dtt, csx, csb, csc, h0p, cwx, cwb, cwc, cbx, cbb, cbc,
            dtbr, dtbc, alr, alc, dsk, nw]
    aliases = {}
    if y_prev is not None:
        in_specs.append(pl.BlockSpec(memory_space=pl.ANY))
        args.append(y_prev)
        aliases = {len(args) - 1: 0}
    out_shape = [
        jax.ShapeDtypeStruct((t_all, SSD_INNER), BF16),
        jax.ShapeDtypeStruct(csx.shape, F32), jax.ShapeDtypeStruct(csb.shape, F32),
        jax.ShapeDtypeStruct(csc.shape, F32), jax.ShapeDtypeStruct(h0p.shape, F32),
    ]
    out_specs = [
        pl.BlockSpec((C, SSD_INNER), lambda s, c: (rb(s, c), 0)),
        per_stream(csx), per_stream(csb), per_stream(csc), per_stream(h0p),
    ]
    scratch = [
        pltpu.VMEM((SSD_G, 8, SSD_GW), F32), pltpu.VMEM((SSD_G, 8, SSD_N), F32),
        pltpu.VMEM((SSD_G, 8, SSD_N), F32),
        pltpu.VMEM((C + 8, SSD_GW), F32), pltpu.VMEM((C + 8, SSD_N), F32), pltpu.VMEM((C + 8, SSD_N), F32),
        pltpu.VMEM((SSD_HEADS // 2, SSD_N, 128), F32),
        pltpu.VMEM((SSD_G, C, SSD_GW), BF16),
    ]
    y, cox, cob, coc, ho = pl.pallas_call(
        functools.partial(_ssd_body, C=C, aliased=y_prev is not None),
        grid=(S, nch), in_specs=in_specs, out_specs=out_specs, out_shape=out_shape,
        scratch_shapes=scratch, input_output_aliases=aliases,
        compiler_params=_cparams(("arbitrary", "arbitrary")),
    )(*args)
    conv_out = jnp.concatenate([jnp.moveaxis(a, 1, 2).reshape(S, SSD_CONV - 1, -1) for a in (cox, cob, coc)],
                               axis=-1)
    return y, conv_out, ho.reshape(S, SSD_HEADS, SSD_INNER // SSD_HEADS, SSD_N)


def _swa_body(sink_ref, q_ref, pk_ref, pv_ref, kv_ref, *rest, QT, prev_valid, aliased):
    o_ref = rest[-1]
    i = pl.program_id(0)
    nk = WINDOW + QT
    qc = lax.broadcasted_iota(jnp.int32, (QT, nk), 0) // CHUNK
    kc = lax.broadcasted_iota(jnp.int32, (QT, nk), 1) // CHUNK - WINDOW // CHUNK
    valid = (kc <= qc) & (kc >= qc - WINDOW // CHUNK)
    if not prev_valid:
        valid = valid & ((kc >= 0) | (i > 0))
    kcat = jnp.concatenate([pk_ref[...], kv_ref[:, :SWA_KVH * SWA_DH]], axis=0).astype(BF16)
    vcat = jnp.concatenate([pv_ref[...], kv_ref[:, SWA_KVH * SWA_DH:]], axis=0).astype(BF16)
    scale = SWA_DH ** -0.5
    outs = []
    for h in range(SWA_QH):
        kh = h // SWA_GRP
        q = q_ref[:, SWA_DH * h:SWA_DH * (h + 1)]
        k = kcat[:, SWA_DH * kh:SWA_DH * (kh + 1)]
        v = vcat[:, SWA_DH * kh:SWA_DH * (kh + 1)]
        s = jnp.where(valid, _dot_nt(q, k) * scale, NEG)
        sink = sink_ref[h]
        m = jnp.maximum(jnp.max(s, axis=-1, keepdims=True), sink)
        p = jnp.exp(s - m)
        den = jnp.sum(p, axis=-1, keepdims=True) + jnp.exp(sink - m)
        outs.append(_dot(p.astype(BF16), v) / den)
    o_ref[...] = jnp.concatenate(outs, axis=1).astype(BF16)


def swa_core(q, kv, prev_k, prev_v, sinks, *, row0, tiles, qt, prompt, o_prev=None):
    t_all = q.shape[0]
    rb0 = row0 // qt
    kvw = SWA_KVH * SWA_DH
    if prompt:
        wpt = qt // WINDOW
        prev_map_k = lambda i, s: (jnp.maximum(wpt * (rb0 + i) - 1, 0), 0)
        prev_map_v = lambda i, s: (jnp.maximum(wpt * (rb0 + i) - 1, 0), 1)
        pk_spec = pl.BlockSpec((WINDOW, kvw), prev_map_k)
        pv_spec = pl.BlockSpec((WINDOW, kvw), prev_map_v)
        prev_k = prev_v = kv
    else:
        pk_spec = pl.BlockSpec((None, WINDOW, kvw), lambda i, s: (i, 0, 0))
        pv_spec = pl.BlockSpec((None, WINDOW, kvw), lambda i, s: (i, 0, 0))
    in_specs = [pl.BlockSpec((qt, D), lambda i, s: (rb0 + i, 0)), pk_spec, pv_spec,
                pl.BlockSpec((qt, 2 * kvw), lambda i, s: (rb0 + i, 0))]
    args = [sinks, q, prev_k, prev_v, kv]
    aliases = {}
    if o_prev is not None:
        in_specs.append(pl.BlockSpec(memory_space=pl.ANY))
        args.append(o_prev)
        aliases = {len(args) - 1: 0}
    return pl.pallas_call(
        functools.partial(_swa_body, QT=qt, prev_valid=not prompt, aliased=o_prev is not None),
        grid_spec=pltpu.PrefetchScalarGridSpec(
            num_scalar_prefetch=1, grid=(tiles,), in_specs=in_specs,
            out_specs=pl.BlockSpec((qt, D), lambda i, s: (rb0 + i, 0))),
        out_shape=jax.ShapeDtypeStruct((t_all, D), BF16),
        input_output_aliases=aliases,
        compiler_params=_cparams(("arbitrary",)),
    )(*args)


def _gelu_tanh(y):
    return 0.5 * y * (1.0 + jnp.tanh(0.7978845608028654 * (y + 0.044715 * y * y * y)))


def _s5_body(u_ref, wx_ref, wy_ref, kt_ref, a_ref, dsk_ref, hr0_ref, hi0_ref,
             y_ref, hro_ref, hio_ref, xr_ref, xi_ref, pr_ref, pi_ref, *, S, R):
    for p in range(S5_PB):
        ub = u_ref[p].astype(BF16)
        x = _dot(ub, wx_ref[p])
        xr_ref[p] = x[:, :128]
        xi_ref[p] = x[:, 128:]
    ar = [a_ref[p, 0:1, :] for p in range(S5_PB)]
    ai = [a_ref[p, 1:2, :] for p in range(S5_PB)]
    for s in range(S):
        def step(k, carry):
            row = s * R + k
            new = []
            for p in range(S5_PB):
                hr, hi = carry[2 * p], carry[2 * p + 1]
                pr_ref[p, pl.ds(row, 1), :] = hr
                pi_ref[p, pl.ds(row, 1), :] = hi
                nr = ar[p] * hr - ai[p] * hi + xr_ref[p, pl.ds(row, 1), :]
                ni = ar[p] * hi + ai[p] * hr + xi_ref[p, pl.ds(row, 1), :]
                new += [nr, ni]
            return tuple(new)

        init = []
        for p in range(S5_PB):
            init += [hr0_ref[s, p], hi0_ref[s, p]]
        fin = lax.fori_loop(0, R, step, tuple(init))
        for p in range(S5_PB):
            hro_ref[s, p] = fin[2 * p]
            hio_ref[s, p] = fin[2 * p + 1]
    for p in range(S5_PB):
        u = u_ref[p]
        hprev = jnp.concatenate([pr_ref[p], pi_ref[p]], axis=1).astype(BF16)
        y = _dot(u.astype(BF16), kt_ref[p]) + _dot(hprev, wy_ref[p]) + dsk_ref[p] * u
        y_ref[p] = _gelu_tanh(y).astype(BF16)


def s5_tables(p):
    a_re, a_im, log_dt, b_re, b_im, c_re, c_im, d_skip = p
    lr, li = a_re.astype(F32), a_im.astype(F32)
    dt = jnp.exp(log_dt.astype(F32))[:, None]
    mag = jnp.exp(lr * dt)
    ab_r, ab_i = mag * jnp.cos(li * dt), mag * jnp.sin(li * dt)
    den = lr * lr + li * li
    co_r = ((ab_r - 1.0) * lr + ab_i * li) / den
    co_i = (ab_i * lr - (ab_r - 1.0) * li) / den
    bb_r = co_r[..., None] * b_re - co_i[..., None] * b_im
    bb_i = co_r[..., None] * b_im + co_i[..., None] * b_re
    pw_r, pw_i = [jnp.ones_like(ab_r)], [jnp.zeros_like(ab_i)]
    for _ in range(S5_SUB):
        r, i = pw_r[-1], pw_i[-1]
        pw_r.append(ab_r * r - ab_i * i)
        pw_i.append(ab_r * i + ab_i * r)
    pr, pi = jnp.stack(pw_r, 0), jnp.stack(pw_i, 0)
    er, ei = pr[S5_SUB - 1::-1][:S5_SUB], pi[S5_SUB - 1::-1][:S5_SUB]
    wx_r = er[..., None] * bb_r[None] - ei[..., None] * bb_i[None]
    wx_i = er[..., None] * bb_i[None] + ei[..., None] * bb_r[None]
    wx_r = wx_r.transpose(1, 0, 3, 2).reshape(S5_G, S5_SUB * S5_CH, S5_N)
    wx_i = wx_i.transpose(1, 0, 3, 2).reshape(S5_G, S5_SUB * S5_CH, S5_N)
    qr, qi = pr[1:], pi[1:]
    cr, ci = c_re.astype(F32), c_im.astype(F32)
    wy_r = cr[None] * qr[:, :, None, :] - ci[None] * qi[:, :, None, :]
    wy_i = -(cr[None] * qi[:, :, None, :] + ci[None] * qr[:, :, None, :])
    wy_r = wy_r.transpose(1, 3, 0, 2).reshape(S5_G, S5_N, S5_SUB * S5_CH)
    wy_i = wy_i.transpose(1, 3, 0, 2).reshape(S5_G, S5_N, S5_SUB * S5_CH)
    tr = pr[:S5_SUB, :, None, :] * cr[None] - pi[:S5_SUB, :, None, :] * ci[None]
    ti = pr[:S5_SUB, :, None, :] * ci[None] + pi[:S5_SUB, :, None, :] * cr[None]
    taps = jnp.einsum('agjn,gnk->agjk', tr, bb_r) - jnp.einsum('agjn,gnk->agjk', ti, bb_i)
    t_idx = jnp.arange(S5_SUB)
    lag = t_idx[None, :] - t_idx[:, None]
    kt = jnp.where((lag >= 0)[:, :, None, None, None], taps[jnp.clip(lag, 0, S5_SUB - 1)], 0.0)
    kt = kt.transpose(2, 0, 4, 1, 3).reshape(S5_G, S5_SUB * S5_CH, S5_SUB * S5_CH)

    def pair_rows(a):
        return a.reshape(S5_PAIRS, 2, a.shape[1], a.shape[2])

    z = jnp.zeros
    wxr, wxi = pair_rows(wx_r), pair_rows(wx_i)
    zz = z((S5_PAIRS, S5_SUB * S5_CH, S5_N), F32)
    wx = jnp.concatenate([
        jnp.concatenate([wxr[:, 0], zz, wxi[:, 0], zz], axis=2),
        jnp.concatenate([zz, wxr[:, 1], zz, wxi[:, 1]], axis=2)], axis=1)
    wyr, wyi = pair_rows(wy_r), pair_rows(wy_i)
    zy = z((S5_PAIRS, S5_N, S5_SUB * S5_CH), F32)
    wy = jnp.concatenate([
        jnp.concatenate([wyr[:, 0], zy], axis=2), jnp.concatenate([zy, wyr[:, 1]], axis=2),
        jnp.concatenate([wyi[:, 0], zy], axis=2), jnp.concatenate([zy, wyi[:, 1]], axis=2)], axis=1)
    ktp = pair_rows(kt)
    zk = z((S5_PAIRS, S5_SUB * S5_CH, S5_SUB * S5_CH), F32)
    ktt = jnp.concatenate([jnp.concatenate([ktp[:, 0], zk], axis=2),
                           jnp.concatenate([zk, ktp[:, 1]], axis=2)], axis=1)
    a16 = jnp.stack([pr[S5_SUB].reshape(S5_PAIRS, 2 * S5_N), pi[S5_SUB].reshape(S5_PAIRS, 2 * S5_N)], axis=1)
    dsk = jnp.tile(d_skip.astype(F32).reshape(S5_G, 1, S5_CH), (1, S5_SUB, 1)).reshape(S5_PAIRS, 1, 2 * S5_SUB * S5_CH)
    return wx.astype(BF16), wy.astype(BF16), ktt.astype(BF16), a16, dsk


def s5_core(u, s_re, s_im, tables, *, streams, length):
    wx, wy, kt, a16, dsk = tables
    S, L = streams, length
    R = L // S5_SUB
    w = 2 * S5_SUB * S5_CH
    ug = u.reshape(S * R, S5_SUB, S5_PAIRS, 2, S5_CH).transpose(2, 0, 3, 1, 4).reshape(S5_PAIRS, S * R, w)
    hr0 = s_re.reshape(S, S5_PAIRS, 1, 2 * S5_N)
    hi0 = s_im.reshape(S, S5_PAIRS, 1, 2 * S5_N)
    pb = lambda a: pl.BlockSpec((S5_PB,) + a.shape[1:], lambda i: (i,) + (0,) * (a.ndim - 1))
    st = pl.BlockSpec((S, S5_PB, 1, 2 * S5_N), lambda i: (0, i, 0, 0))
    y, hro, hio = pl.pallas_call(
        functools.partial(_s5_body, S=S, R=R),
        grid=(S5_PAIRS // S5_PB,),
        in_specs=[pb(ug), pb(wx), pb(wy), pb(kt), pb(a16), pb(dsk), st, st],
        out_specs=[pb(ug), st, st],
        out_shape=[jax.ShapeDtypeStruct(ug.shape, BF16), jax.ShapeDtypeStruct(hr0.shape, F32),
                   jax.ShapeDtypeStruct(hi0.shape, F32)],
        scratch_shapes=[pltpu.VMEM((S5_PB, S * R, 2 * S5_N), F32) for _ in range(4)],
        compiler_params=_cparams(("arbitrary",)),
    )(ug, wx, wy, kt, a16, dsk, hr0, hi0)
    yt = y.reshape(S5_PAIRS, S * R, 2, S5_SUB, S5_CH).transpose(1, 3, 0, 2, 4).reshape(S * L, D)
    return yt, hro.reshape(S, S5_G, S5_N), hio.reshape(S, S5_G, S5_N)


def _hgrn_body(q_ref, f_ref, i_ref, g_ref, lbp_ref, nw_ref, s0_ref, *rest, C, layer, aliased):
    if aliased:
        rest = rest[1:]
    o_ref, so_ref, st_ref = rest
    c = pl.program_id(1)

    @pl.when(c == 0)
    def _init():
        for h in range(HG_H):
            st_ref[h] = s0_ref[0, h].T

    lbp = lbp_ref[...]
    e = jnp.exp(lbp - jnp.max(lbp, axis=0, keepdims=True))
    lbs = e / jnp.sum(e, axis=0, keepdims=True)
    lb = jnp.zeros((1, D), F32)
    for r in range(1, layer + 1):
        lb = lb + lbs[r:r + 1, :]
    fz = f_ref[...]
    log_sig = jnp.minimum(fz, 0.0) - jnp.log(1.0 + jnp.exp(-jnp.abs(fz)))
    la = jnp.log(lb)
    lbb = jnp.log(1.0 - lb) + log_sig
    mx = jnp.maximum(la, lbb)
    logf = mx + jnp.log(1.0 + jnp.exp(-jnp.abs(la - lbb)))
    kk = 1.0 - jnp.exp(logf)
    qs = _silu(q_ref[...])
    tril = _tri(C)
    cum = _dot(tril.astype(F32), logf, precision=HIGHEST)
    row = lax.broadcasted_iota(jnp.int32, (C, C), 0)
    col = lax.broadcasted_iota(jnp.int32, (C, C), 1)

    levels = []
    b = C
    while b >= HG_LEAF:
        nb = C // b
        ref = jnp.broadcast_to(cum.reshape(nb, b, D)[:, b // 2 - 1:b // 2, :], (nb, b, D)).reshape(C, D)
        ex = cum - ref
        same = (row // b) == (col // b)
        if b == HG_LEAF:
            mask = same & (col <= row)
            qe, ke = jnp.exp(jnp.minimum(ex, 80.0)), jnp.exp(jnp.minimum(-ex, 80.0))
        else:
            mask = same & ((row % b) >= b // 2) & ((col % b) < b // 2)
            qe, ke = jnp.exp(jnp.minimum(ex, 0.0)), jnp.exp(jnp.minimum(-ex, 0.0))
        levels.append((mask, (qs * qe).astype(BF16), (kk * ke).astype(BF16)))
        b //= 2

    last = cum[C - 1:C, :]
    qin = (qs * jnp.exp(cum)).astype(BF16)
    kin = (kk * jnp.exp(last - cum)).astype(BF16)
    dec = jnp.exp(last)
    vv = i_ref[...]
    vb = vv.astype(BF16)
    gate = _silu(g_ref[...])
    nw = nw_ref[...]
    outs = []
    for h in range(HG_H):
        sl = slice(HG_K * h, HG_K * (h + 1))
        att = jnp.zeros((C, C), F32)
        for mask, ql, kl in levels:
            att = att + jnp.where(mask, _dot_nt(ql[:, sl], kl[:, sl]), 0.0)
        st = st_ref[h]
        o = _dot(att.astype(BF16), vb[:, sl]) + _dot_nt(qin[:, sl], st.astype(BF16))
        st_ref[h] = st * dec[:, sl] + _dot(_transpose_rows(vv[:, sl]).astype(BF16), kin[:, sl])
        ms = jnp.mean(o * o, axis=-1, keepdims=True)
        outs.append(o * lax.rsqrt(ms + RMS_EPS) * nw)
    o_ref[...] = (jnp.concatenate(outs, axis=1) * gate).astype(BF16)

    @pl.when(c == pl.num_programs(1) - 1)
    def _fin():
        for h in range(HG_H):
            so_ref[0, h] = st_ref[h].T


def hgrn_core(qfig, lb_param, norm_w, s0, *, layer, row0, streams, length, chunk, o_prev=None):
    S, L, C = streams, length, chunk
    nch = L // C
    rb0 = row0 // C
    t_all = qfig.shape[0]
    rowspec = lambda j: pl.BlockSpec((C, D), lambda s, c: (rb0 + s * nch + c, j))
    in_specs = [rowspec(0), rowspec(1), rowspec(2), rowspec(3),
                pl.BlockSpec((DEPTH, D), lambda s, c: (0, 0)),
                pl.BlockSpec((1, HG_K), lambda s, c: (0, 0)),
                pl.BlockSpec((1, HG_H, HG_K, HG_K), lambda s, c: (s, 0, 0, 0))]
    args = [qfig, qfig, qfig, qfig, lb_param, norm_w.reshape(1, HG_K), s0]
    aliases = {}
    if o_prev is not None:
        in_specs.append(pl.BlockSpec(memory_space=pl.ANY))
        args.append(o_prev)
        aliases = {len(args) - 1: 0}
    return pl.pallas_call(
        functools.partial(_hgrn_body, C=C, layer=layer, aliased=o_prev is not None),
        grid=(S, nch), in_specs=in_specs,
        out_specs=[rowspec(0), pl.BlockSpec((1, HG_H, HG_K, HG_K), lambda s, c: (s, 0, 0, 0))],
        out_shape=[jax.ShapeDtypeStruct((t_all, D), BF16), jax.ShapeDtypeStruct(s0.shape, F32)],
        scratch_shapes=[pltpu.VMEM((HG_H, HG_K, HG_K), F32)],
        input_output_aliases=aliases,
        compiler_params=_cparams(("arbitrary", "arbitrary")),
    )(*args)


def _expert_body(te_ref, nt_ref, x_ref, w_ref, wg_ref, wu_ref, wd_ref, o_ref, wgb_ref, wub_ref, wdb_ref):
    i = pl.program_id(0)
    prev = te_ref[jnp.maximum(i - 1, 0)]

    @pl.when((i == 0) | (te_ref[i] != prev))
    def _():
        wgb_ref[...] = wg_ref[0, 0].astype(BF16)
        wub_ref[...] = wu_ref[0, 0].astype(BF16)
        wdb_ref[...] = wd_ref[0, 0].astype(BF16)

    @pl.when(i < nt_ref[0])
    def _():
        x = x_ref[...]
        h = _silu(_dot(x, wgb_ref[...])) * _dot(x, wub_ref[...]) * w_ref[...]
        o_ref[...] = _dot(h.astype(BF16), wdb_ref[...])

    @pl.when(i >= nt_ref[0])
    def _():
        o_ref[...] = jnp.zeros_like(o_ref)


def expert_mlp(xs, ws, tile_expert, n_tiles, w_gate, w_up, w_down, layer):
    p = xs.shape[0]
    nt = p // MOE_TM
    wspec = lambda shp: pl.BlockSpec((1, 1) + shp, lambda i, te, n: (layer, te[i], 0, 0))
    return pl.pallas_call(
        _expert_body,
        grid_spec=pltpu.PrefetchScalarGridSpec(
            num_scalar_prefetch=2, grid=(nt,),
            in_specs=[pl.BlockSpec((MOE_TM, D), lambda i, te, n: (i, 0)),
                      pl.BlockSpec((MOE_TM, 1), lambda i, te, n: (i, 0)),
                      wspec((D, D_EXPERT)), wspec((D, D_EXPERT)), wspec((D_EXPERT, D))],
            out_specs=pl.BlockSpec((MOE_TM, D), lambda i, te, n: (i, 0)),
            scratch_shapes=[pltpu.VMEM((D, D_EXPERT), BF16), pltpu.VMEM((D, D_EXPERT), BF16),
                            pltpu.VMEM((D_EXPERT, D), BF16)]),
        out_shape=jax.ShapeDtypeStruct((p, D), F32),
        compiler_params=_cparams(("arbitrary",)),
    )(tile_expert, n_tiles, xs, ws, w_gate, w_up, w_down)


def moe_layer(x1, x1b, route, w_gate, w_up, w_down, layer):
    t = x1.shape[0]
    n = 2 * t
    eid = jnp.concatenate([route[:, 0], route[:, 1]]).astype(jnp.int32)
    wgt = jnp.concatenate([route[:, 2], route[:, 3]])
    tok = jnp.concatenate([jnp.arange(t, dtype=jnp.int32)] * 2)
    oh = (eid[:, None] == jnp.arange(MOE_E, dtype=jnp.int32)[None, :]).astype(jnp.int32)
    rank = jnp.sum((jnp.cumsum(oh, axis=0) - oh) * oh, axis=1)
    counts = jnp.sum(oh, axis=0)
    padded = (counts + MOE_TM - 1) // MOE_TM * MOE_TM
    ends = jnp.cumsum(padded)
    dest = (ends - padded)[eid] + rank
    p_rows = (n + MOE_E * (MOE_TM - 1)) // MOE_TM * MOE_TM + MOE_TM
    nt = p_rows // MOE_TM
    src = jnp.zeros((p_rows,), jnp.int32).at[dest].set(tok)
    ws = jnp.zeros((p_rows,), F32).at[dest].set(wgt).reshape(p_rows, 1)
    tile_start = jnp.arange(nt, dtype=jnp.int32) * MOE_TM
    tile_expert = jnp.minimum(jnp.sum((tile_start[:, None] >= ends[None, :]).astype(jnp.int32), axis=1),
                              MOE_E - 1).astype(jnp.int32)
    n_tiles = (ends[-1] // MOE_TM).astype(jnp.int32).reshape(1)
    xs = jnp.take(x1b, src, axis=0)
    ys = expert_mlp(xs, ws, tile_expert, n_tiles, w_gate, w_up, w_down, layer)
    return jnp.take(ys, dest[:t], axis=0) + jnp.take(ys, dest[t:], axis=0)


def _forward(x_prompt, x_sample, states, ssd_p, swa_p, s5_p, hg_p, ln_p, moe_p):
    (state_ssd_conv, state_ssd, cache_k, cache_v, s5_re, s5_im, state_hgrn) = states
    lp = x_prompt.shape[1]
    sb, ls = x_sample.shape[0], x_sample.shape[1]
    t_all = lp + sb * ls
    ln1_g, ln1_b, ln2_g, ln2_b = ln_p
    w_rg, b_rg, w_re, b_re, w_gate, w_up, w_down = moe_p

    x = jnp.concatenate([x_prompt.reshape(lp, D), x_sample.reshape(sb * ls, D)], axis=0)
    xb = x.astype(BF16)
    outs = {}
    for layer in range(DEPTH):
        kind = layer % 4
        if kind == 0:
            w_in, conv_w, conv_b, dt_bias, a_log, d_skip, norm_w, w_out = ssd_p
            zx = matmul(xb, w_in, col0=0, ncols=2 * SSD_INNER, tn=SSD_GW, tiled_out=True)
            bc = matmul(xb, w_in, col0=2 * SSD_INNER, ncols=2 * SSD_G * SSD_N, tn=SSD_N, tiled_out=True)
            dt = matmul(xb, w_in[:, SSD_INNER + SSD_XBC:], tn=SSD_HEADS)
            prm = (conv_w, conv_b, dt_bias, a_log, d_skip, norm_w)
            zc = jnp.zeros((1, SSD_CONV - 1, SSD_XBC), F32)
            zh = jnp.zeros((1, SSD_HEADS, SSD_INNER // SSD_HEADS, SSD_N), F32)
            y, pc, ph = ssd_core(zx, bc, dt, zc, zh, prm, row0=0, streams=1, length=lp, chunk=128)
            y, sc, sh = ssd_core(zx, bc, dt, state_ssd_conv, state_ssd, prm, row0=lp, streams=sb,
                                 length=ls, chunk=ls, y_prev=y)
            outs['conv'], outs['ssd'] = (pc, sc), (ph, sh)
            h = matmul(y, w_out)
        elif kind == 1:
            w_qkv, sinks, w_out = swa_p
            kvw = SWA_KVH * SWA_DH
            q = matmul(xb, w_qkv, col0=0, ncols=D, out_dtype=BF16)
            kv = matmul(xb, w_qkv, col0=D, ncols=2 * kvw)
            o = swa_core(q, kv, None, None, sinks, row0=0, tiles=lp // 256, qt=256, prompt=True)
            ck = cache_k.reshape(sb, WINDOW, kvw)
            cv = cache_v.reshape(sb, WINDOW, kvw)
            o = swa_core(q, kv, ck, cv, sinks, row0=lp, tiles=sb, qt=ls, prompt=False, o_prev=o)
            kshape = (SWA_KVH, SWA_DH)
            pk = kv[lp - WINDOW:lp, :kvw].reshape((1, WINDOW) + kshape)
            pv = kv[lp - WINDOW:lp, kvw:].reshape((1, WINDOW) + kshape)
            kvs = kv[lp:].reshape(sb, ls, 2 * kvw)
            sk = jnp.concatenate([ck, kvs[:, :, :kvw]], axis=1)[:, -WINDOW:].reshape((sb, WINDOW) + kshape)
            sv = jnp.concatenate([cv, kvs[:, :, kvw:]], axis=1)[:, -WINDOW:].reshape((sb, WINDOW) + kshape)
            outs['k'], outs['v'] = (pk, sk), (pv, sv)
            h = matmul(o, w_out)
        elif kind == 2:
            w_in, w_glu = s5_p[0], s5_p[-1]
            tables = s5_tables(s5_p[1:-1])
            u = matmul(xb, w_in)
            zs = jnp.zeros((1, S5_G, S5_N), F32)
            yp, pr, pi = s5_core(u[:lp], zs, zs, tables, streams=1, length=lp)
            ys, sr, si = s5_core(u[lp:], s5_re, s5_im, tables, streams=sb, length=ls)
            outs['s5r'], outs['s5i'] = (pr, sr), (pi, si)
            h = glu_matmul(jnp.concatenate([yp, ys], axis=0), w_glu)
        else:
            w_in, lb_param, norm_w, w_out = hg_p
            qfig = matmul(xb, w_in)
            zs = jnp.zeros((1, HG_H, HG_K, HG_K), F32)
            o, ps = hgrn_core(qfig, lb_param, norm_w, zs, layer=layer, row0=0, streams=1, length=lp, chunk=128)
            o, ss = hgrn_core(qfig, lb_param, norm_w, state_hgrn, layer=layer, row0=lp, streams=sb,
                              length=ls, chunk=ls, o_prev=o)
            outs['hg'] = (ps, ss)
            h = matmul(o, w_out)
        wr = jnp.concatenate([w_rg[layer], w_re[layer], jnp.zeros((D, 128 - MOE_G - MOE_E), F32)], axis=1)
        br = jnp.concatenate([b_rg[layer], b_re[layer], jnp.zeros((128 - MOE_G - MOE_E,), F32)]).reshape(1, 128)
        x1, x1b, route = add_ln(x, h, ln1_g[layer], ln1_b[layer], router=(wr, br))
        f = moe_layer(x1, x1b, route, w_gate, w_up, w_down, layer)
        x, xb = add_ln(x1, f, ln2_g[layer], ln2_b[layer])
    y_prompt = x[:lp].reshape(1, lp, D)
    y_sample = x[lp:].reshape(sb, ls, D)
    order = ('conv', 'ssd', 'k', 'v', 's5r', 's5i', 'hg')
    return (y_prompt, y_sample) + tuple(outs[k][0] for k in order) + tuple(outs[k][1] for k in order)


def kernel(x_prompt, x_sample, state_ssd_conv, state_ssd, cache_swa_k, cache_swa_v, state_s5_re, state_s5_im, state_hgrn, ssd_w_in, ssd_conv_w, ssd_conv_b, ssd_dt_bias, ssd_a_log, ssd_d, ssd_norm_w, ssd_w_out, swa_w_qkv, swa_sinks, swa_w_out, s5_w_in, s5_a_re, s5_a_im, s5_log_dt, s5_b_re, s5_b_im, s5_c_re, s5_c_im, s5_d, s5_w_glu, hg_w_in, hg_lb, hg_norm_w, hg_w_out, ln1_g, ln1_b, ln2_g, ln2_b, moe_w_rg, moe_b_rg, moe_w_re, moe_b_re, moe_w_gate, moe_w_up, moe_w_down):
    states = (state_ssd_conv, state_ssd, cache_swa_k, cache_swa_v, state_s5_re, state_s5_im, state_hgrn)
    ssd_p = (ssd_w_in, ssd_conv_w, ssd_conv_b, ssd_dt_bias, ssd_a_log, ssd_d, ssd_norm_w, ssd_w_out)
    swa_p = (swa_w_qkv, swa_sinks, swa_w_out)
    s5_p = (s5_w_in, s5_a_re, s5_a_im, s5_log_dt, s5_b_re, s5_b_im, s5_c_re, s5_c_im, s5_d, s5_w_glu)
    hg_p = (hg_w_in, hg_lb, hg_norm_w, hg_w_out)
    ln_p = (ln1_g, ln1_b, ln2_g, ln2_b)
    moe_p = (moe_w_rg, moe_b_rg, moe_w_re, moe_b_re, moe_w_gate, moe_w_up, moe_w_down)
    return _forward(x_prompt, x_sample, states, ssd_p, swa_p, s5_p, hg_p, ln_p, moe_p)
```

```python
import functools

import jax
import jax.numpy as jnp
from jax import lax
from jax.experimental import pallas as pl
from jax.experimental.pallas import tpu as pltpu

F32 = jnp.float32
BF16 = jnp.bfloat16
HIGHEST = lax.Precision.HIGHEST

D = 2048
DEPTH = 4
DN_ALPHA = (2 * DEPTH) ** 0.25
LN_EPS = 1e-5
RMS_EPS = 1e-6
NEG = -1e30

VMEM_LIMIT = 56 * 1024 * 1024

SSD_INNER = 4096
SSD_HEADS = 64
SSD_G = 8
SSD_HPG = 8
SSD_N = 128
SSD_GW = SSD_INNER // SSD_G
SSD_CONV = 4
SSD_XBC = SSD_INNER + 2 * SSD_G * SSD_N

SWA_DH = 64
SWA_QH = 32
SWA_KVH = 4
SWA_GRP = SWA_QH // SWA_KVH
WINDOW = 128
CHUNK = 64

S5_G = 128
S5_CH = 16
S5_N = 64
S5_SUB = 16
S5_GB = 8
S5_CW = S5_GB * S5_CH
S5_KR = 256

HG_H = 16
HG_K = 128
HG_LEAF = 16

MOE_G = 4
MOE_PG = 8
MOE_E = 32
D_EXPERT = 256
MOE_TM = 256


def _cparams(sem):
    return pltpu.CompilerParams(dimension_semantics=sem, vmem_limit_bytes=VMEM_LIMIT)


def _sigmoid(x):
    return 1.0 / (1.0 + jnp.exp(-x))


def _silu(x):
    return x * _sigmoid(x)


def _softplus(x):
    return jnp.maximum(x, 0.0) + jnp.log(1.0 + jnp.exp(-jnp.abs(x)))


def _dot(a, b, precision=None):
    return jnp.dot(a, b, preferred_element_type=F32, precision=precision)


def _dot_nt(a, b):
    return lax.dot_general(a, b, (((1,), (1,)), ((), ())), preferred_element_type=F32)


def _tri(n, upper=False):
    r = lax.broadcasted_iota(jnp.int32, (n, n), 0)
    c = lax.broadcasted_iota(jnp.int32, (n, n), 1)
    return (r <= c) if upper else (c <= r)


def _transpose_rows(x):
    c = x.shape[0]
    if c == 128:
        return x.T
    pad = jnp.zeros((128 - c, 128), x.dtype)
    return jnp.concatenate([x, pad], axis=0).T[:, :c]


def _mm_body(x_ref, w_ref, o_ref, wb_ref, *, tiled_out):
    @pl.when(pl.program_id(1) == 0)
    def _():
        wb_ref[...] = w_ref[...].astype(BF16)

    r = _dot(x_ref[...], wb_ref[...]).astype(o_ref.dtype)
    if tiled_out:
        sub = o_ref.shape[2]
        for q in range(o_ref.shape[0]):
            o_ref[q] = r[:, sub * q:sub * (q + 1)]
    else:
        o_ref[...] = r


def matmul(x, w, *, col0=0, ncols=None, tn=512, tm=512, out_dtype=F32, tiled_out=False, sub=None,
           name="matmul"):
    m, k = x.shape
    ncols = w.shape[1] - col0 if ncols is None else ncols
    tm = min(tm, m)
    assert col0 % tn == 0 and ncols % tn == 0 and m % tm == 0
    nj = ncols // tn
    j0 = col0 // tn
    if tiled_out:
        sub = tn if sub is None else sub
        out_shape = jax.ShapeDtypeStruct((ncols // sub, m, sub), out_dtype)
        out_spec = pl.BlockSpec((tn // sub, tm, sub), lambda j, i: (j, i, 0))
    else:
        out_shape = jax.ShapeDtypeStruct((m, ncols), out_dtype)
        out_spec = pl.BlockSpec((tm, tn), lambda j, i: (i, j))
    return pl.pallas_call(
        functools.partial(_mm_body, tiled_out=tiled_out),
        grid=(nj, m // tm),
        in_specs=[pl.BlockSpec((tm, k), lambda j, i: (i, 0)),
                  pl.BlockSpec((k, tn), lambda j, i: (0, j + j0))],
        out_specs=out_spec,
        out_shape=out_shape,
        scratch_shapes=[pltpu.VMEM((k, tn), BF16)],
        compiler_params=_cparams(("arbitrary", "arbitrary")),
        name=name,
    )(x, w)


def _glu_body(x_ref, wv_ref, wg_ref, o_ref, wvb_ref, wgb_ref):
    @pl.when(pl.program_id(1) == 0)
    def _():
        wvb_ref[...] = wv_ref[...].astype(BF16)
        wgb_ref[...] = wg_ref[...].astype(BF16)

    x = x_ref[...].astype(BF16)
    o_ref[...] = _dot(x, wvb_ref[...]) * _sigmoid(_dot(x, wgb_ref[...]))


def glu_matmul(x, w, *, tn=512, tm=512):
    m, k = x.shape
    n = w.shape[1] // 2
    tm = min(tm, m)
    nj = n // tn
    return pl.pallas_call(
        _glu_body,
        grid=(nj, m // tm),
        in_specs=[pl.BlockSpec((tm, k), lambda j, i: (i, 0)),
                  pl.BlockSpec((k, tn), lambda j, i: (0, j)),
                  pl.BlockSpec((k, tn), lambda j, i: (0, j + nj))],
        out_specs=pl.BlockSpec((tm, tn), lambda j, i: (i, j)),
        out_shape=jax.ShapeDtypeStruct((m, n), F32),
        scratch_shapes=[pltpu.VMEM((k, tn), BF16), pltpu.VMEM((k, tn), BF16)],
        compiler_params=_cparams(("arbitrary", "arbitrary")),
        name="glu_matmul",
    )(x, w, w)


def _layer_norm(v, g, b):
    mu = jnp.mean(v, axis=-1, keepdims=True)
    vc = v - mu
    var = jnp.mean(vc * vc, axis=-1, keepdims=True)
    return vc * lax.rsqrt(var + LN_EPS) * g + b


def _route(x, wr, br, carry):
    lg = _dot(x, wr, precision=HIGHEST) + br
    tm = lg.shape[0]
    lane = lax.broadcasted_iota(jnp.int32, lg.shape, 1)
    lanef = lane.astype(F32)
    big = jnp.float32(1e9)
    is_g = lane < MOE_G
    gl = jnp.where(is_g, lg, NEG)
    gmax = jnp.max(gl, axis=-1, keepdims=True)
    gsel = jnp.min(jnp.where(is_g & (gl == gmax), lanef, big), axis=-1, keepdims=True)
    gprob = 1.0 / jnp.sum(jnp.where(is_g, jnp.exp(gl - gmax), 0.0), axis=-1, keepdims=True)
    lo = MOE_G + MOE_PG * gsel
    is_e = (lanef >= lo) & (lanef < lo + MOE_PG)
    el = jnp.where(is_e, lg, NEG)
    m1 = jnp.max(el, axis=-1, keepdims=True)
    l1 = jnp.min(jnp.where(is_e & (el == m1), lanef, big), axis=-1, keepdims=True)
    is_e2 = is_e & (lanef != l1)
    el2 = jnp.where(is_e2, lg, NEG)
    m2 = jnp.max(el2, axis=-1, keepdims=True)
    l2 = jnp.min(jnp.where(is_e2 & (el2 == m2), lanef, big), axis=-1, keepdims=True)
    r = jnp.exp(m2 - m1)
    w1 = gprob / (1.0 + r)
    w2 = gprob * r / (1.0 + r)
    hit1 = lanef == l1
    hit2 = lanef == l2
    oh = jnp.where(hit1 | hit2, 1.0, 0.0)
    rr = lax.broadcasted_iota(jnp.int32, (tm, tm), 0)
    cc = lax.broadcasted_iota(jnp.int32, (tm, tm), 1)
    before = _dot(jnp.where(cc < rr, 1.0, 0.0).astype(BF16), oh.astype(BF16)) + carry
    k1 = jnp.sum(jnp.where(hit1, before, 0.0), axis=-1, keepdims=True)
    k2 = jnp.sum(jnp.where(hit2, before, 0.0), axis=-1, keepdims=True)
    table = jnp.where(lane == 0, l1 - MOE_G,
                      jnp.where(lane == 1, l2 - MOE_G,
                                jnp.where(lane == 2, w1,
                                          jnp.where(lane == 3, w2,
                                                    jnp.where(lane == 4, k1, jnp.where(lane == 5, k2, 0.0))))))
    return table, carry + jnp.sum(oh, axis=0, keepdims=True)


def _ln_route_body(x_ref, h_ref, g_ref, b_ref, wr_ref, br_ref, o_ref, r_ref, cnt_ref):
    @pl.when(pl.program_id(0) == 0)
    def _():
        cnt_ref[...] = jnp.zeros_like(cnt_ref)

    y = _layer_norm(DN_ALPHA * x_ref[...] + h_ref[...], g_ref[...], b_ref[...])
    o_ref[...] = y
    r_ref[...], cnt_ref[...] = _route(y, wr_ref[...], br_ref[...], cnt_ref[...])


def ln_route(x, h, g, b, wr, br, tm=256):
    m = x.shape[0]
    row = pl.BlockSpec((tm, D), lambda i: (i, 0))
    vec = pl.BlockSpec((1, D), lambda i: (0, 0))
    one = pl.BlockSpec((1, 128), lambda i: (0, 0))
    return pl.pallas_call(
        _ln_route_body,
        grid=(m // tm,),
        in_specs=[row, row, vec, vec, pl.BlockSpec((D, 128), lambda i: (0, 0)), one],
        out_specs=[row, pl.BlockSpec((tm, 128), lambda i: (i, 0)), one],
        out_shape=[jax.ShapeDtypeStruct((m, D), F32), jax.ShapeDtypeStruct((m, 128), F32),
                   jax.ShapeDtypeStruct((1, 128), F32)],
        compiler_params=_cparams(("arbitrary",)),
        name="ln_route",
    )(x, h, g.reshape(1, D), b.reshape(1, D), wr, br)


def _ln_combine_body(x_ref, y0_ref, y1_ref, r_ref, g_ref, b_ref, o_ref, ob_ref):
    r = r_ref[...]
    f = r[:, 2:3] * y0_ref[...] + r[:, 3:4] * y1_ref[...]
    y = _layer_norm(DN_ALPHA * x_ref[...] + f, g_ref[...], b_ref[...])
    o_ref[...] = y
    ob_ref[...] = y.astype(BF16)


def ln_combine(x, y0, y1, route, g, b, tm=256):
    m = x.shape[0]
    row = pl.BlockSpec((tm, D), lambda i: (i, 0))
    vec = pl.BlockSpec((1, D), lambda i: (0, 0))
    return pl.pallas_call(
        _ln_combine_body,
        grid=(m // tm,),
        in_specs=[row, row, row, pl.BlockSpec((tm, 128), lambda i: (i, 0)), vec, vec],
        out_specs=[row, row],
        out_shape=[jax.ShapeDtypeStruct((m, D), F32), jax.ShapeDtypeStruct((m, D), BF16)],
        compiler_params=_cparams(("arbitrary",)),
        name="ln_combine",
    )(x, y0, y1, route, g.reshape(1, D), b.reshape(1, D))


def _ssd_body(z_ref, x_ref, b_ref, c_ref, dtg_ref, dtt_ref, csx_ref, csb_ref, csc_ref, h0_ref,
              cwx_ref, cwb_ref, cwc_ref, cbx_ref, cbb_ref, cbc_ref, dtbr_ref, dtbc_ref,
              alr_ref, alc_ref, dsk_ref, nw_ref, *rest, C, aliased):
    if aliased:
        rest = rest[1:]
    (y_ref, cox_ref, cob_ref, coc_ref, ho_ref,
     tx_ref, tb_ref, tc_ref, ex_ref, eb_ref, ec_ref, ht_ref, yb_ref) = rest
    c = pl.program_id(1)
    last_chunk = c == pl.num_programs(1) - 1

    @pl.when(c == 0)
    def _init():
        tx_ref[...] = jnp.zeros_like(tx_ref)
        tb_ref[...] = jnp.zeros_like(tb_ref)
        tc_ref[...] = jnp.zeros_like(tc_ref)
        for g in range(SSD_G):
            tx_ref[g, 5:8, :] = csx_ref[0, g]
            tb_ref[g, 5:8, :] = csb_ref[0, g]
            tc_ref[g, 5:8, :] = csc_ref[0, g]
        for p in range(SSD_HEADS // 2):
            ht_ref[p] = h0_ref[0, p].T

    tril = _tri(C)
    tril_f = tril.astype(F32)
    triu_f = _tri(C, upper=True).astype(F32)
    lane = lax.broadcasted_iota(jnp.int32, (C, 128), 1)
    left = lane < 64

    def conv(e_ref, t_ref, raw, w_ref, bias_ref, g):
        e_ref[0:8, :] = t_ref[g]
        e_ref[8:8 + C, :] = raw
        w = w_ref[g]
        acc = bias_ref[g] + w[3:4, :] * raw
        for k in range(SSD_CONV - 1):
            acc = acc + w[k:k + 1, :] * e_ref[5 + k:5 + k + C, :]
        t_ref[g] = e_ref[C:C + 8, :]
        return _silu(acc)

    def group(g, carry):
        xs = conv(ex_ref, tx_ref, x_ref[g], cwx_ref, cbx_ref, g)
        bs = conv(eb_ref, tb_ref, b_ref[g], cwb_ref, cbb_ref, g)
        cs = conv(ec_ref, tc_ref, c_ref[g], cwc_ref, cbc_ref, g)

        @pl.when(last_chunk)
        def _():
            cox_ref[0, g] = ex_ref[C + 5:C + 8, :]
            cob_ref[0, g] = eb_ref[C + 5:C + 8, :]
            coc_ref[0, g] = ec_ref[C + 5:C + 8, :]

        dtv = _softplus(dtg_ref[0, g] + dtbr_ref[g])
        dtvt = _softplus(dtt_ref[0, g] + dtbc_ref[g])
        cum = _dot(tril_f, dtv * (-jnp.exp(alr_ref[g])), precision=HIGHEST)
        cumt = _dot(dtvt * (-jnp.exp(alc_ref[g])), triu_f, precision=HIGHEST)
        bsb = bs.astype(BF16)
        csb = cs.astype(BF16)
        cb = _dot_nt(csb, bsb)
        bst = _transpose_rows(bs).astype(BF16)
        dsk = dsk_ref[g]
        ys = []
        for j in range(SSD_HPG // 2):
            h0, h1 = 2 * j, 2 * j + 1
            c0, c1 = cum[:, h0:h0 + 1], cum[:, h1:h1 + 1]
            l0 = jnp.where(tril, jnp.exp(c0 - cumt[h0:h0 + 1, :]), 0.0) * cb
            l1 = jnp.where(tril, jnp.exp(c1 - cumt[h1:h1 + 1, :]), 0.0) * cb
            lhs = jnp.concatenate([l0, l1], axis=1).astype(BF16)
            xp = xs[:, 128 * j:128 * (j + 1)]
            xdt = xp * jnp.where(left, dtv[:, h0:h0 + 1], dtv[:, h1:h1 + 1])
            rhs = jnp.concatenate([jnp.where(left, xdt, 0.0), jnp.where(left, 0.0, xdt)],
                                  axis=0).astype(BF16)
            htp = ht_ref[g * 4 + j]
            y = _dot(lhs, rhs)
            y = y + _dot(csb, htp.astype(BF16)) * jnp.where(left, jnp.exp(c0), jnp.exp(c1))
            e0, e1 = cum[C - 1:C, h0:h0 + 1], cum[C - 1:C, h1:h1 + 1]
            wgt = (xdt * jnp.where(left, jnp.exp(e0 - c0), jnp.exp(e1 - c1))).astype(BF16)
            ht_ref[g * 4 + j] = jnp.where(left[0:1, :], jnp.exp(e0), jnp.exp(e1)) * htp + _dot(bst, wgt)
            ys.append(y + dsk[:, 128 * j:128 * (j + 1)] * xp)
        y = jnp.concatenate(ys, axis=1) * _silu(z_ref[g])
        ms = jnp.mean(y * y, axis=-1, keepdims=True)
        yb_ref[g] = (y * lax.rsqrt(ms + RMS_EPS) * nw_ref[g]).astype(BF16)
        return carry

    lax.fori_loop(0, SSD_G, group, 0)
    for g in range(SSD_G):
        y_ref[:, SSD_GW * g:SSD_GW * (g + 1)] = yb_ref[g]

    @pl.when(last_chunk)
    def _fin():
        for p in range(SSD_HEADS // 2):
            ho_ref[0, p] = ht_ref[p].T


def ssd_core(zx, bc, dt, conv_state, h0, params, *, row0, streams, length, chunk, y_prev=None):
    conv_w, conv_b, dt_bias, a_log, d_skip, norm_w = params
    S, L, C = streams, length, chunk
    nch = L // C
    rb0 = row0 // C
    t_all = zx.shape[1]
    dseg = dt[row0:row0 + S * L].reshape(S, L, SSD_G, SSD_HPG)
    dtg = dseg.transpose(0, 2, 1, 3)
    dtt = dseg.transpose(0, 2, 3, 1)

    def split(a, lead):
        ax = a[..., :SSD_INNER].reshape(lead + (SSD_G, SSD_GW))
        ab = a[..., SSD_INNER:SSD_INNER + SSD_G * SSD_N].reshape(lead + (SSD_G, SSD_N))
        ac = a[..., SSD_INNER + SSD_G * SSD_N:].reshape(lead + (SSD_G, SSD_N))
        return ax, ab, ac

    csx, csb, csc = (jnp.moveaxis(a, 2, 1) for a in split(conv_state, (S, SSD_CONV - 1)))
    cwx, cwb, cwc = (jnp.moveaxis(a, 1, 0) for a in split(conv_w, (SSD_CONV,)))
    cbx, cbb, cbc = (jnp.moveaxis(a, 1, 0) for a in split(conv_b.reshape(1, -1), (1,)))
    dtbr = dt_bias.reshape(SSD_G, 1, SSD_HPG)
    dtbc = dt_bias.reshape(SSD_G, SSD_HPG, 1)
    alr = a_log.reshape(SSD_G, 1, SSD_HPG)
    alc = a_log.reshape(SSD_G, SSD_HPG, 1)
    dsk = jnp.repeat(d_skip, SSD_INNER // SSD_HEADS).reshape(SSD_G, 1, SSD_GW)
    nw = norm_w.reshape(SSD_G, 1, SSD_GW)
    h0p = h0.reshape(S, SSD_HEADS // 2, 128, SSD_N)

    def rb(s, c):
        return rb0 + s * nch + c

    def full(a):
        nd = a.ndim
        return pl.BlockSpec(a.shape, lambda s, c: (0,) * nd)

    def per_stream(a):
        nd = a.ndim
        return pl.BlockSpec((1,) + a.shape[1:], lambda s, c: (s,) + (0,) * (nd - 1))

    in_specs = [
        pl.BlockSpec((SSD_G, C, SSD_GW), lambda s, c: (0, rb(s, c), 0)),
        pl.BlockSpec((SSD_G, C, SSD_GW), lambda s, c: (1, rb(s, c), 0)),
        pl.BlockSpec((SSD_G, C, SSD_N), lambda s, c: (0, rb(s, c), 0)),
        pl.BlockSpec((SSD_G, C, SSD_N), lambda s, c: (1, rb(s, c), 0)),
        pl.BlockSpec((1, SSD_G, C, SSD_HPG), lambda s, c: (s, 0, c, 0)),
        pl.BlockSpec((1, SSD_G, SSD_HPG, C), lambda s, c: (s, 0, 0, c)),
        per_stream(csx), per_stream(csb), per_stream(csc), per_stream(h0p),
        full(cwx), full(cwb), full(cwc), full(cbx), full(cbb), full(cbc),
        full(dtbr), full(dtbc), full(alr), full(alc), full(dsk), full(nw),
    ]
    args = [zx, zx, bc, bc, dtg, dtt, csx, csb, csc, h0p, cwx, cwb, cwc, cbx, cbb, cbc,
            dtbr, dtbc, alr, alc, dsk, nw]
    aliases = {}
    if y_prev is not None:
        in_specs.append(pl.BlockSpec(memory_space=pl.ANY))
        args.append(y_prev)
        aliases = {len(args) - 1: 0}
    out_shape = [
        jax.ShapeDtypeStruct((t_all, SSD_INNER), BF16),
        jax.ShapeDtypeStruct(csx.shape, F32), jax.ShapeDtypeStruct(csb.shape, F32),
        jax.ShapeDtypeStruct(csc.shape, F32), jax.ShapeDtypeStruct(h0p.shape, F32),
    ]
    out_specs = [
        pl.BlockSpec((C, SSD_INNER), lambda s, c: (rb(s, c), 0)),
        per_stream(csx), per_stream(csb), per_stream(csc), per_stream(h0p),
    ]
    scratch = [
        pltpu.VMEM((SSD_G, 8, SSD_GW), F32), pltpu.VMEM((SSD_G, 8, SSD_N), F32),
        pltpu.VMEM((SSD_G, 8, SSD_N), F32),
        pltpu.VMEM((C + 8, SSD_GW), F32), pltpu.VMEM((C + 8, SSD_N), F32), pltpu.VMEM((C + 8, SSD_N), F32),
        pltpu.VMEM((SSD_HEADS // 2, SSD_N, 128), F32),
        pltpu.VMEM((SSD_G, C, SSD_GW), BF16),
    ]
    y, cox, cob, coc, ho = pl.pallas_call(
        functools.partial(_ssd_body, C=C, aliased=y_prev is not None),
        grid=(S, nch), in_specs=in_specs, out_specs=out_specs, out_shape=out_shape,
        scratch_shapes=scratch, input_output_aliases=aliases,
        compiler_params=_cparams(("arbitrary", "arbitrary")),
        name="ssd_core",
    )(*args)
    conv_out = jnp.concatenate([jnp.moveaxis(a, 1, 2).reshape(S, SSD_CONV - 1, -1) for a in (cox, cob, coc)],
                               axis=-1)
    return y, conv_out, ho.reshape(S, SSD_HEADS, SSD_INNER // SSD_HEADS, SSD_N)


def _swa_body(sink_ref, q_ref, pk_ref, pv_ref, kv_ref, *rest, QT, prev_valid, aliased):
    o_ref = rest[-1]
    i = pl.program_id(0)
    nk = WINDOW + QT
    qc = lax.broadcasted_iota(jnp.int32, (QT, nk), 0) // CHUNK
    kc = lax.broadcasted_iota(jnp.int32, (QT, nk), 1) // CHUNK - WINDOW // CHUNK
    valid = (kc <= qc) & (kc >= qc - WINDOW // CHUNK)
    if not prev_valid:
        valid = valid & ((kc >= 0) | (i > 0))
    kcat = jnp.concatenate([pk_ref[...], kv_ref[:, :SWA_KVH * SWA_DH]], axis=0).astype(BF16)
    vcat = jnp.concatenate([pv_ref[...], kv_ref[:, SWA_KVH * SWA_DH:]], axis=0).astype(BF16)
    scale = SWA_DH ** -0.5
    outs = []
    for h in range(SWA_QH):
        kh = h // SWA_GRP
        q = q_ref[:, SWA_DH * h:SWA_DH * (h + 1)]
        k = kcat[:, SWA_DH * kh:SWA_DH * (kh + 1)]
        v = vcat[:, SWA_DH * kh:SWA_DH * (kh + 1)]
        s = jnp.where(valid, _dot_nt(q, k) * scale, NEG)
        sink = sink_ref[h]
        m = jnp.maximum(jnp.max(s, axis=-1, keepdims=True), sink)
        p = jnp.exp(s - m)
        den = jnp.sum(p, axis=-1, keepdims=True) + jnp.exp(sink - m)
        outs.append(_dot(p.astype(BF16), v) / den)
    o_ref[...] = jnp.concatenate(outs, axis=1).astype(BF16)


def swa_core(q, kv, prev_k, prev_v, sinks, *, row0, tiles, qt, prompt, o_prev=None):
    t_all = q.shape[0]
    rb0 = row0 // qt
    kvw = SWA_KVH * SWA_DH
    if prompt:
        wpt = qt // WINDOW
        prev_map_k = lambda i, s: (jnp.maximum(wpt * (rb0 + i) - 1, 0), 0)
        prev_map_v = lambda i, s: (jnp.maximum(wpt * (rb0 + i) - 1, 0), 1)
        pk_spec = pl.BlockSpec((WINDOW, kvw), prev_map_k)
        pv_spec = pl.BlockSpec((WINDOW, kvw), prev_map_v)
        prev_k = prev_v = kv
    else:
        pk_spec = pl.BlockSpec((None, WINDOW, kvw), lambda i, s: (i, 0, 0))
        pv_spec = pl.BlockSpec((None, WINDOW, kvw), lambda i, s: (i, 0, 0))
    in_specs = [pl.BlockSpec((qt, D), lambda i, s: (rb0 + i, 0)), pk_spec, pv_spec,
                pl.BlockSpec((qt, 2 * kvw), lambda i, s: (rb0 + i, 0))]
    args = [sinks, q, prev_k, prev_v, kv]
    aliases = {}
    if o_prev is not None:
        in_specs.append(pl.BlockSpec(memory_space=pl.ANY))
        args.append(o_prev)
        aliases = {len(args) - 1: 0}
    return pl.pallas_call(
        functools.partial(_swa_body, QT=qt, prev_valid=not prompt, aliased=o_prev is not None),
        grid_spec=pltpu.PrefetchScalarGridSpec(
            num_scalar_prefetch=1, grid=(tiles,), in_specs=in_specs,
            out_specs=pl.BlockSpec((qt, D), lambda i, s: (rb0 + i, 0))),
        out_shape=jax.ShapeDtypeStruct((t_all, D), BF16),
        input_output_aliases=aliases,
        compiler_params=_cparams(("arbitrary",)),
        name="swa_core",
    )(*args)


def _gelu_tanh(y):
    return 0.5 * y * (1.0 + jnp.tanh(0.7978845608028654 * (y + 0.044715 * y * y * y)))


def _s5_body(u_ref, wxr_ref, wxi_ref, wyr_ref, wyi_ref, kt_ref, a_ref, dsk_ref, hr0_ref, hi0_ref, *rest,
             S, R, aliased):
    if aliased:
        rest = rest[1:]
    (y_ref, hro_ref, hio_ref, wx_s, wy_s, ktm_s, xr_s, xi_s, pr_s, pi_s, hr_s, hi_s) = rest
    kb = pl.program_id(1)
    half = S5_CW * S5_N // S5_CH

    @pl.when((pl.program_id(0) == 0) & (kb == 0))
    def _zero():
        ktm_s[...] = jnp.zeros_like(ktm_s)

    @pl.when(kb == 0)
    def _build():
        own = (lax.broadcasted_iota(jnp.int32, (S5_CW, half), 0) // S5_CH
               == lax.broadcasted_iota(jnp.int32, (S5_CW, half), 1) // S5_N)
        for s in range(S5_SUB):
            rows = slice(S5_CW * s, S5_CW * (s + 1))
            wx_s[rows, 0:half] = jnp.where(own, jnp.concatenate([wxr_ref[s]] * 4, axis=1), 0.0).astype(BF16)
            wx_s[rows, half:2 * half] = jnp.where(own, jnp.concatenate([wxi_ref[s]] * 4, axis=1), 0.0).astype(BF16)
        own_t = (lax.broadcasted_iota(jnp.int32, (half, S5_CW), 0) // S5_N
                 == lax.broadcasted_iota(jnp.int32, (half, S5_CW), 1) // S5_CH)
        for t in range(S5_SUB):
            cols = slice(S5_CW * t, S5_CW * (t + 1))
            wy_s[0:half, cols] = jnp.where(own_t, jnp.concatenate([wyr_ref[t]] * 8, axis=0), 0.0).astype(BF16)
            wy_s[half:2 * half, cols] = jnp.where(own_t, jnp.concatenate([wyi_ref[t]] * 8, axis=0), 0.0).astype(BF16)
        for s in range(S5_SUB):
            for t in range(s, S5_SUB):
                ktm_s[S5_CW * s:S5_CW * (s + 1), S5_CW * t:S5_CW * (t + 1)] = kt_ref[t - s].astype(BF16)
        for st in range(S):
            hr_s[st] = hr0_ref[st]
            hi_s[st] = hi0_ref[st]

    kr = S * R
    ucat = jnp.concatenate([u_ref[pl.ds(s, kr, stride=S5_SUB), :] for s in range(S5_SUB)],
                           axis=1)
    ub = ucat.astype(BF16)
    x = _dot(ub, wx_s[...])
    xr_s[...] = x[:, :half]
    xi_s[...] = x[:, half:]
    ar = a_ref[0:1, :]
    ai = a_ref[1:2, :]
    for st in range(S):
        def step(k, carry):
            hr, hi = carry
            row = st * R + k
            pr_s[pl.ds(row, 1), :] = hr
            pi_s[pl.ds(row, 1), :] = hi
            nr = ar * hr - ai * hi + xr_s[pl.ds(row, 1), :]
            ni = ar * hi + ai * hr + xi_s[pl.ds(row, 1), :]
            return nr, ni

        hr, hi = lax.fori_loop(0, R, step, (hr_s[st], hi_s[st]))
        hr_s[st] = hr
        hi_s[st] = hi
    hprev = jnp.concatenate([pr_s[...], pi_s[...]], axis=1).astype(BF16)
    dsk = jnp.concatenate([dsk_ref[...]] * S5_SUB, axis=1)
    y = _gelu_tanh(_dot(ub, ktm_s[...]) + _dot(hprev, wy_s[...]) + dsk * ucat)
    for t in range(S5_SUB):
        y_ref[pl.ds(t, kr, stride=S5_SUB), :] = y[:, S5_CW * t:S5_CW * (t + 1)]

    @pl.when(kb == pl.num_programs(1) - 1)
    def _fin():
        for st in range(S):
            hro_ref[st] = hr_s[st]
            hio_ref[st] = hi_s[st]


def s5_tables(p):
    a_re, a_im, log_dt, b_re, b_im, c_re, c_im, d_skip = p
    lr, li = a_re.astype(F32), a_im.astype(F32)
    dt = jnp.exp(log_dt.astype(F32))[:, None]
    mag = jnp.exp(lr * dt)
    ab_r, ab_i = mag * jnp.cos(li * dt), mag * jnp.sin(li * dt)
    den = lr * lr + li * li
    co_r = ((ab_r - 1.0) * lr + ab_i * li) / den
    co_i = (ab_i * lr - (ab_r - 1.0) * li) / den
    bb_r = co_r[..., None] * b_re - co_i[..., None] * b_im
    bb_i = co_r[..., None] * b_im + co_i[..., None] * b_re
    pw_r, pw_i = [jnp.ones_like(ab_r)], [jnp.zeros_like(ab_i)]
    for _ in range(S5_SUB):
        r, i = pw_r[-1], pw_i[-1]
        pw_r.append(ab_r * r - ab_i * i)
        pw_i.append(ab_r * i + ab_i * r)
    pr, pi = jnp.stack(pw_r, 0), jnp.stack(pw_i, 0)
    er, ei = pr[S5_SUB - 1::-1][:S5_SUB], pi[S5_SUB - 1::-1][:S5_SUB]
    wx_r = er[..., None] * bb_r[None] - ei[..., None] * bb_i[None]
    wx_i = er[..., None] * bb_i[None] + ei[..., None] * bb_r[None]
    gb, gw = S5_G // S5_GB, S5_GB
    wx_r = wx_r.reshape(S5_SUB, gb, gw, S5_N, S5_CH).transpose(1, 0, 2, 4, 3).reshape(gb, S5_SUB, S5_CW, S5_N)
    wx_i = wx_i.reshape(S5_SUB, gb, gw, S5_N, S5_CH).transpose(1, 0, 2, 4, 3).reshape(gb, S5_SUB, S5_CW, S5_N)
    wxr = jnp.concatenate([wx_r, wx_r], axis=-1)
    wxi = jnp.concatenate([wx_i, wx_i], axis=-1)
    qr, qi = pr[1:], pi[1:]
    cr, ci = c_re.astype(F32), c_im.astype(F32)
    wy_r = cr[None] * qr[:, :, None, :] - ci[None] * qi[:, :, None, :]
    wy_i = -(cr[None] * qi[:, :, None, :] + ci[None] * qr[:, :, None, :])
    wyr = wy_r.reshape(S5_SUB, gb, gw, S5_CH, S5_N).transpose(1, 0, 4, 2, 3).reshape(gb, S5_SUB, S5_N, S5_CW)
    wyi = wy_i.reshape(S5_SUB, gb, gw, S5_CH, S5_N).transpose(1, 0, 4, 2, 3).reshape(gb, S5_SUB, S5_N, S5_CW)
    tr = pr[:S5_SUB, :, None, :] * cr[None] - pi[:S5_SUB, :, None, :] * ci[None]
    ti = pr[:S5_SUB, :, None, :] * ci[None] + pi[:S5_SUB, :, None, :] * cr[None]
    taps = jnp.einsum('agjn,gnk->agjk', tr, bb_r) - jnp.einsum('agjn,gnk->agjk', ti, bb_i)
    taps = taps.reshape(S5_SUB, gb, gw, S5_CH, S5_CH).transpose(1, 0, 2, 4, 3)
    kt = jnp.einsum('ptgab,gh->ptgahb', taps, jnp.eye(gw, dtype=F32)).reshape(gb, S5_SUB, S5_CW, S5_CW)
    a16 = jnp.stack([pr[S5_SUB].reshape(gb, gw * S5_N), pi[S5_SUB].reshape(gb, gw * S5_N)], axis=1)
    dsk = d_skip.astype(F32).reshape(gb, 1, S5_CW)
    return wxr, wxi, wyr, wyi, kt, a16, dsk


def s5_core(u, s_re, s_im, tables, *, row0, streams, length, y_prev=None):
    wxr, wxi, wyr, wyi, kt, a16, dsk = tables
    S, L = streams, length
    gb = S5_G // S5_GB
    rows = S * L // S5_SUB
    if S == 1:
        kr = min(S5_KR, rows)
        spb, rps = 1, kr
    else:
        kr = rows
        spb, rps = S, L // S5_SUB
    nkb = rows // kr
    kb0 = row0 // S5_SUB // kr
    half = S5_GB * S5_N
    hr0 = s_re.reshape(S, gb, 1, half)
    hi0 = s_im.reshape(S, gb, 1, half)
    tab = lambda a: pl.BlockSpec((None,) + a.shape[1:], lambda p, k: (p,) + (0,) * (a.ndim - 1))
    st = pl.BlockSpec((S, None, 1, half), lambda p, k: (0, p, 0, 0))
    uspec = pl.BlockSpec((kr * S5_SUB, S5_CW), lambda p, k: (kb0 + k, p))
    in_specs = [uspec, tab(wxr), tab(wxi), tab(wyr), tab(wyi), tab(kt), tab(a16), tab(dsk), st, st]
    args = [u, wxr, wxi, wyr, wyi, kt, a16, dsk, hr0, hi0]
    aliases = {}
    if y_prev is not None:
        in_specs.append(pl.BlockSpec(memory_space=pl.ANY))
        args.append(y_prev)
        aliases = {len(args) - 1: 0}
    wide = S5_SUB * S5_CW
    y, hro, hio = pl.pallas_call(
        functools.partial(_s5_body, S=spb, R=rps, aliased=y_prev is not None),
        grid=(gb, nkb),
        in_specs=in_specs,
        out_specs=[uspec, st, st],
        out_shape=[jax.ShapeDtypeStruct(u.shape, F32), jax.ShapeDtypeStruct(hr0.shape, F32),
                   jax.ShapeDtypeStruct(hi0.shape, F32)],
        scratch_shapes=[pltpu.VMEM((wide, 2 * half), BF16), pltpu.VMEM((2 * half, wide), BF16),
                        pltpu.VMEM((wide, wide), BF16)]
        + [pltpu.VMEM((kr, half), F32) for _ in range(4)]
        + [pltpu.VMEM((spb, 1, half), F32) for _ in range(2)],
        input_output_aliases=aliases,
        compiler_params=_cparams(("arbitrary", "arbitrary")),
        name="s5_core",
    )(*args)
    return y, hro.reshape(S, S5_G, S5_N), hio.reshape(S, S5_G, S5_N)


def _hgrn_body(q_ref, f_ref, i_ref, g_ref, lbp_ref, nw_ref, s0_ref, *rest, C, layer, aliased):
    if aliased:
        rest = rest[1:]
    o_ref, so_ref, st_ref = rest
    c = pl.program_id(1)

    @pl.when(c == 0)
    def _init():
        for h in range(HG_H):
            st_ref[h] = s0_ref[0, h].T

    lbp = lbp_ref[...]
    e = jnp.exp(lbp - jnp.max(lbp, axis=0, keepdims=True))
    lbs = e / jnp.sum(e, axis=0, keepdims=True)
    lb = jnp.zeros((1, D), F32)
    for r in range(1, layer + 1):
        lb = lb + lbs[r:r + 1, :]
    fz = f_ref[...]
    log_sig = jnp.minimum(fz, 0.0) - jnp.log(1.0 + jnp.exp(-jnp.abs(fz)))
    la = jnp.log(lb)
    lbb = jnp.log(1.0 - lb) + log_sig
    mx = jnp.maximum(la, lbb)
    logf = mx + jnp.log(1.0 + jnp.exp(-jnp.abs(la - lbb)))
    kk = 1.0 - jnp.exp(logf)
    qs = _silu(q_ref[...])
    tril = _tri(C)
    cum = _dot(tril.astype(F32), logf, precision=HIGHEST)
    row = lax.broadcasted_iota(jnp.int32, (C, C), 0)
    col = lax.broadcasted_iota(jnp.int32, (C, C), 1)

    levels = []
    b = C
    while b >= HG_LEAF:
        nb = C // b
        ref = jnp.broadcast_to(cum.reshape(nb, b, D)[:, b // 2 - 1:b // 2, :], (nb, b, D)).reshape(C, D)
        ex = cum - ref
        same = (row // b) == (col // b)
        if b == HG_LEAF:
            mask = same & (col <= row)
            qe, ke = jnp.exp(jnp.minimum(ex, 80.0)), jnp.exp(jnp.minimum(-ex, 80.0))
        else:
            mask = same & ((row % b) >= b // 2) & ((col % b) < b // 2)
            qe, ke = jnp.exp(jnp.minimum(ex, 0.0)), jnp.exp(jnp.minimum(-ex, 0.0))
        levels.append((mask, (qs * qe).astype(BF16), (kk * ke).astype(BF16)))
        b //= 2

    last = cum[C - 1:C, :]
    qin = (qs * jnp.exp(cum)).astype(BF16)
    kin = (kk * jnp.exp(last - cum)).astype(BF16)
    dec = jnp.exp(last)
    vv = i_ref[...]
    vb = vv.astype(BF16)
    gate = _silu(g_ref[...])
    nw = nw_ref[...]
    outs = []
    for h in range(HG_H):
        sl = slice(HG_K * h, HG_K * (h + 1))
        att = jnp.zeros((C, C), F32)
        for mask, ql, kl in levels:
            att = att + jnp.where(mask, _dot_nt(ql[:, sl], kl[:, sl]), 0.0)
        st = st_ref[h]
        o = _dot(att.astype(BF16), vb[:, sl]) + _dot_nt(qin[:, sl], st.astype(BF16))
        st_ref[h] = st * dec[:, sl] + _dot(_transpose_rows(vv[:, sl]).astype(BF16), kin[:, sl])
        ms = jnp.mean(o * o, axis=-1, keepdims=True)
        outs.append(o * lax.rsqrt(ms + RMS_EPS) * nw)
    o_ref[...] = (jnp.concatenate(outs, axis=1) * gate).astype(BF16)

    @pl.when(c == pl.num_programs(1) - 1)
    def _fin():
        for h in range(HG_H):
            so_ref[0, h] = st_ref[h].T


def hgrn_core(qfig, lb_param, norm_w, s0, *, layer, row0, streams, length, chunk, o_prev=None):
    S, L, C = streams, length, chunk
    nch = L // C
    rb0 = row0 // C
    t_all = qfig.shape[0]
    rowspec = lambda j: pl.BlockSpec((C, D), lambda s, c: (rb0 + s * nch + c, j))
    in_specs = [rowspec(0), rowspec(1), rowspec(2), rowspec(3),
                pl.BlockSpec((DEPTH, D), lambda s, c: (0, 0)),
                pl.BlockSpec((1, HG_K), lambda s, c: (0, 0)),
                pl.BlockSpec((1, HG_H, HG_K, HG_K), lambda s, c: (s, 0, 0, 0))]
    args = [qfig, qfig, qfig, qfig, lb_param, norm_w.reshape(1, HG_K), s0]
    aliases = {}
    if o_prev is not None:
        in_specs.append(pl.BlockSpec(memory_space=pl.ANY))
        args.append(o_prev)
        aliases = {len(args) - 1: 0}
    return pl.pallas_call(
        functools.partial(_hgrn_body, C=C, layer=layer, aliased=o_prev is not None),
        grid=(S, nch), in_specs=in_specs,
        out_specs=[rowspec(0), pl.BlockSpec((1, HG_H, HG_K, HG_K), lambda s, c: (s, 0, 0, 0))],
        out_shape=[jax.ShapeDtypeStruct((t_all, D), BF16), jax.ShapeDtypeStruct(s0.shape, F32)],
        scratch_shapes=[pltpu.VMEM((HG_H, HG_K, HG_K), F32)],
        input_output_aliases=aliases,
        compiler_params=_cparams(("arbitrary", "arbitrary")),
        name="hgrn_core",
    )(*args)


def _expert_body(te_ref, nt_ref, x_ref, wg_ref, wu_ref, wd_ref, o_ref, wgb_ref, wub_ref, wdb_ref):
    i = pl.program_id(0)
    prev = te_ref[jnp.maximum(i - 1, 0)]

    @pl.when((i == 0) | (te_ref[i] != prev))
    def _():
        wgb_ref[...] = wg_ref[0, 0].astype(BF16)
        wub_ref[...] = wu_ref[0, 0].astype(BF16)
        wdb_ref[...] = wd_ref[0, 0].astype(BF16)

    @pl.when(i < nt_ref[0])
    def _():
        x = x_ref[...].astype(BF16)
        h = _silu(_dot(x, wgb_ref[...])) * _dot(x, wub_ref[...])
        o_ref[...] = _dot(h.astype(BF16), wdb_ref[...])

    @pl.when(i >= nt_ref[0])
    def _():
        o_ref[...] = jnp.zeros_like(o_ref)


def expert_mlp(xs, tile_expert, n_tiles, w_gate, w_up, w_down, layer):
    p = xs.shape[0]
    nt = p // MOE_TM
    wspec = lambda shp: pl.BlockSpec((1, 1) + shp, lambda i, te, n: (layer, te[i], 0, 0))
    return pl.pallas_call(
        _expert_body,
        grid_spec=pltpu.PrefetchScalarGridSpec(
            num_scalar_prefetch=2, grid=(nt,),
            in_specs=[pl.BlockSpec((MOE_TM, D), lambda i, te, n: (i, 0)),
                      wspec((D, D_EXPERT)), wspec((D, D_EXPERT)), wspec((D_EXPERT, D))],
            out_specs=pl.BlockSpec((MOE_TM, D), lambda i, te, n: (i, 0)),
            scratch_shapes=[pltpu.VMEM((D, D_EXPERT), BF16), pltpu.VMEM((D, D_EXPERT), BF16),
                            pltpu.VMEM((D_EXPERT, D), BF16)]),
        out_shape=jax.ShapeDtypeStruct((p, D), F32),
        compiler_params=_cparams(("arbitrary",)),
        name="expert_mlp",
    )(tile_expert, n_tiles, xs, w_gate, w_up, w_down)


def moe_layer(x1, route, counts, w_gate, w_up, w_down, layer):
    t = x1.shape[0]
    e0 = route[:, 0].astype(jnp.int32)
    e1 = route[:, 1].astype(jnp.int32)
    cnt = counts[0, MOE_G:MOE_G + MOE_E].astype(jnp.int32)
    padded = (cnt + MOE_TM - 1) // MOE_TM * MOE_TM
    ends = jnp.cumsum(padded)
    starts = ends - padded
    d0 = starts[e0] + route[:, 4].astype(jnp.int32)
    d1 = starts[e1] + route[:, 5].astype(jnp.int32)
    p_rows = (2 * t + MOE_E * (MOE_TM - 1)) // 512 * 512 + 512
    nt = p_rows // MOE_TM
    tok = jnp.arange(t, dtype=jnp.int32)
    src = jnp.zeros((p_rows,), jnp.int32).at[jnp.concatenate([d0, d1])].set(jnp.concatenate([tok, tok]))
    tile_start = jnp.arange(nt, dtype=jnp.int32) * MOE_TM
    tile_expert = jnp.minimum(jnp.sum((tile_start[:, None] >= ends[None, :]).astype(jnp.int32), axis=1),
                              MOE_E - 1).astype(jnp.int32)
    n_tiles = (ends[-1] // MOE_TM).astype(jnp.int32).reshape(1)
    xs = jnp.take(x1, src, axis=0)
    ys = expert_mlp(xs, tile_expert, n_tiles, w_gate, w_up, w_down, layer)
    return jnp.take(ys, d0, axis=0), jnp.take(ys, d1, axis=0)


def _forward(x_prompt, x_sample, states, ssd_p, swa_p, s5_p, hg_p, ln_p, moe_p):
    (state_ssd_conv, state_ssd, cache_k, cache_v, s5_re, s5_im, state_hgrn) = states
    lp = x_prompt.shape[1]
    sb, ls = x_sample.shape[0], x_sample.shape[1]
    t_all = lp + sb * ls
    ln1_g, ln1_b, ln2_g, ln2_b = ln_p
    w_rg, b_rg, w_re, b_re, w_gate, w_up, w_down = moe_p

    x = jnp.concatenate([x_prompt.reshape(lp, D), x_sample.reshape(sb * ls, D)], axis=0)
    xb = x.astype(BF16)
    outs = {}
    for layer in range(DEPTH):
        kind = layer % 4
        if kind == 0:
            w_in, conv_w, conv_b, dt_bias, a_log, d_skip, norm_w, w_out = ssd_p
            zx = matmul(xb, w_in, col0=0, ncols=2 * SSD_INNER, tn=SSD_GW, tiled_out=True, name="mm_ssd_zx")
            bc = matmul(xb, w_in, col0=2 * SSD_INNER, ncols=2 * SSD_G * SSD_N, tn=512, tiled_out=True,
                        sub=SSD_N, name="mm_ssd_bc")
            dt = matmul(xb, w_in[:, SSD_INNER + SSD_XBC:], tn=SSD_HEADS, name="mm_ssd_dt")
            prm = (conv_w, conv_b, dt_bias, a_log, d_skip, norm_w)
            zc = jnp.zeros((1, SSD_CONV - 1, SSD_XBC), F32)
            zh = jnp.zeros((1, SSD_HEADS, SSD_INNER // SSD_HEADS, SSD_N), F32)
            y, pc, ph = ssd_core(zx, bc, dt, zc, zh, prm, row0=0, streams=1, length=lp, chunk=128)
            y, sc, sh = ssd_core(zx, bc, dt, state_ssd_conv, state_ssd, prm, row0=lp, streams=sb,
                                 length=ls, chunk=ls, y_prev=y)
            outs['conv'], outs['ssd'] = (pc, sc), (ph, sh)
            h = matmul(y, w_out, name="mm_ssd_out")
        elif kind == 1:
            w_qkv, sinks, w_out = swa_p
            kvw = SWA_KVH * SWA_DH
            q = matmul(xb, w_qkv, col0=0, ncols=D, out_dtype=BF16, name="mm_swa_q")
            kv = matmul(xb, w_qkv, col0=D, ncols=2 * kvw, name="mm_swa_kv")
            o = swa_core(q, kv, None, None, sinks, row0=0, tiles=lp // 256, qt=256, prompt=True)
            ck = cache_k.reshape(sb, WINDOW, kvw)
            cv = cache_v.reshape(sb, WINDOW, kvw)
            o = swa_core(q, kv, ck, cv, sinks, row0=lp, tiles=sb, qt=ls, prompt=False, o_prev=o)
            kshape = (SWA_KVH, SWA_DH)
            pk = kv[lp - WINDOW:lp, :kvw].reshape((1, WINDOW) + kshape)
            pv = kv[lp - WINDOW:lp, kvw:].reshape((1, WINDOW) + kshape)
            kvs = kv[lp:].reshape(sb, ls, 2 * kvw)
            sk = jnp.concatenate([ck, kvs[:, :, :kvw]], axis=1)[:, -WINDOW:].reshape((sb, WINDOW) + kshape)
            sv = jnp.concatenate([cv, kvs[:, :, kvw:]], axis=1)[:, -WINDOW:].reshape((sb, WINDOW) + kshape)
            outs['k'], outs['v'] = (pk, sk), (pv, sv)
            h = matmul(o, w_out, name="mm_swa_out")
        elif kind == 2:
            w_in, w_glu = s5_p[0], s5_p[-1]
            tables = s5_tables(s5_p[1:-1])
            u = matmul(xb, w_in, name="mm_s5_in")
            zs = jnp.zeros((1, S5_G, S5_N), F32)
            y, pr, pi = s5_core(u, zs, zs, tables, row0=0, streams=1, length=lp)
            y, sr, si = s5_core(u, s5_re, s5_im, tables, row0=lp, streams=sb, length=ls, y_prev=y)
            outs['s5r'], outs['s5i'] = (pr, sr), (pi, si)
            h = glu_matmul(y, w_glu)
        else:
            w_in, lb_param, norm_w, w_out = hg_p
            qfig = matmul(xb, w_in, name="mm_hg_in")
            zs = jnp.zeros((1, HG_H, HG_K, HG_K), F32)
            o, ps = hgrn_core(qfig, lb_param, norm_w, zs, layer=layer, row0=0, streams=1, length=lp, chunk=128)
            o, ss = hgrn_core(qfig, lb_param, norm_w, state_hgrn, layer=layer, row0=lp, streams=sb,
                              length=ls, chunk=ls, o_prev=o)
            outs['hg'] = (ps, ss)
            h = matmul(o, w_out, name="mm_hg_out")
        wr = jnp.concatenate([w_rg[layer], w_re[layer], jnp.zeros((D, 128 - MOE_G - MOE_E), F32)], axis=1)
        br = jnp.concatenate([b_rg[layer], b_re[layer], jnp.zeros((128 - MOE_G - MOE_E,), F32)]).reshape(1, 128)
        x1, route, counts = ln_route(x, h, ln1_g[layer], ln1_b[layer], wr, br)
        y0, y1 = moe_layer(x1, route, counts, w_gate, w_up, w_down, layer)
        x, xb = ln_combine(x1, y0, y1, route, ln2_g[layer], ln2_b[layer])
    y_prompt = x[:lp].reshape(1, lp, D)
    y_sample = x[lp:].reshape(sb, ls, D)
    order = ('conv', 'ssd', 'k', 'v', 's5r', 's5i', 'hg')
    return (y_prompt, y_sample) + tuple(outs[k][0] for k in order) + tuple(outs[k][1] for k in order)


def kernel(x_prompt, x_sample, state_ssd_conv, state_ssd, cache_swa_k, cache_swa_v, state_s5_re, state_s5_im, state_hgrn, ssd_w_in, ssd_conv_w, ssd_conv_b, ssd_dt_bias, ssd_a_log, ssd_d, ssd_norm_w, ssd_w_out, swa_w_qkv, swa_sinks, swa_w_out, s5_w_in, s5_a_re, s5_a_im, s5_log_dt, s5_b_re, s5_b_im, s5_c_re, s5_c_im, s5_d, s5_w_glu, hg_w_in, hg_lb, hg_norm_w, hg_w_out, ln1_g, ln1_b, ln2_g, ln2_b, moe_w_rg, moe_b_rg, moe_w_re, moe_b_re, moe_w_gate, moe_w_up, moe_w_down):
    states = (state_ssd_conv, state_ssd, cache_swa_k, cache_swa_v, state_s5_re, state_s5_im, state_hgrn)
    ssd_p = (ssd_w_in, ssd_conv_w, ssd_conv_b, ssd_dt_bias, ssd_a_log, ssd_d, ssd_norm_w, ssd_w_out)
    swa_p = (swa_w_qkv, swa_sinks, swa_w_out)
    s5_p = (s5_w_in, s5_a_re, s5_a_im, s5_log_dt, s5_b_re, s5_b_im, s5_c_re, s5_c_im, s5_d, s5_w_glu)
    hg_p = (hg_w_in, hg_lb, hg_norm_w, hg_w_out)
    ln_p = (ln1_g, ln1_b, ln2_g, ln2_b)
    moe_p = (moe_w_rg, moe_b_rg, moe_w_re, moe_b_re, moe_w_gate, moe_w_up, moe_w_down)
    return _forward(x_prompt, x_sample, states, ssd_p, swa_p, s5_p, hg_p, ln_p, moe_p)
```

```python
import functools

import jax
import jax.numpy as jnp
from jax import lax
from jax.experimental import pallas as pl
from jax.experimental.pallas import tpu as pltpu

F32 = jnp.float32
BF16 = jnp.bfloat16
HIGHEST = lax.Precision.HIGHEST

D = 2048
DEPTH = 4
DN_ALPHA = (2 * DEPTH) ** 0.25
LN_EPS = 1e-5
RMS_EPS = 1e-6
NEG = -1e30

VMEM_LIMIT = 56 * 1024 * 1024
MM_TN = 1024

SSD_INNER = 4096
SSD_HEADS = 64
SSD_G = 8
SSD_HPG = 8
SSD_N = 128
SSD_GW = SSD_INNER // SSD_G
SSD_CONV = 4
SSD_XBC = SSD_INNER + 2 * SSD_G * SSD_N

SWA_DH = 64
SWA_QH = 32
SWA_KVH = 4
SWA_GRP = SWA_QH // SWA_KVH
WINDOW = 128
CHUNK = 64

S5_G = 128
S5_CH = 16
S5_N = 64
S5_SUB = 16
S5_GB = 8
S5_CW = S5_GB * S5_CH
S5_KR = 256

HG_H = 16
HG_K = 128
HG_LEAF = 16

MOE_G = 4
MOE_PG = 8
MOE_E = 32
D_EXPERT = 256
MOE_TM = 256


def _cparams(sem):
    return pltpu.CompilerParams(dimension_semantics=sem, vmem_limit_bytes=VMEM_LIMIT)


def _sigmoid(x):
    return 1.0 / (1.0 + jnp.exp(-x))


def _silu(x):
    return x * _sigmoid(x)


def _softplus(x):
    return jnp.maximum(x, 0.0) + jnp.log(1.0 + jnp.exp(-jnp.abs(x)))


def _dot(a, b, precision=None):
    return jnp.dot(a, b, preferred_element_type=F32, precision=precision)


def _dot_nt(a, b):
    return lax.dot_general(a, b, (((1,), (1,)), ((), ())), preferred_element_type=F32)


def _tri(n, upper=False):
    r = lax.broadcasted_iota(jnp.int32, (n, n), 0)
    c = lax.broadcasted_iota(jnp.int32, (n, n), 1)
    return (r <= c) if upper else (c <= r)


def _transpose_rows(x):
    c = x.shape[0]
    if c == 128:
        return x.T
    pad = jnp.zeros((128 - c, 128), x.dtype)
    return jnp.concatenate([x, pad], axis=0).T[:, :c]


def _mm_body(x_ref, w_ref, o_ref, wb_ref, *, tiled_out):
    @pl.when(pl.program_id(1) == 0)
    def _():
        wb_ref[...] = w_ref[...].astype(BF16)

    r = _dot(x_ref[...], wb_ref[...]).astype(o_ref.dtype)
    if tiled_out:
        sub = o_ref.shape[2]
        for q in range(o_ref.shape[0]):
            o_ref[q] = r[:, sub * q:sub * (q + 1)]
    else:
        o_ref[...] = r


def matmul(x, w, *, col0=0, ncols=None, tn=None, tm=512, out_dtype=F32, tiled_out=False, sub=None,
           name="matmul"):
    m, k = x.shape
    ncols = w.shape[1] - col0 if ncols is None else ncols
    tm = min(tm, m)
    if tn is None:
        tn = MM_TN if k * MM_TN * 4 <= 8 * 1024 * 1024 else MM_TN // 2
        tn = min(tn, ncols)
    assert col0 % tn == 0 and ncols % tn == 0 and m % tm == 0
    nj = ncols // tn
    j0 = col0 // tn
    if tiled_out:
        sub = tn if sub is None else sub
        out_shape = jax.ShapeDtypeStruct((ncols // sub, m, sub), out_dtype)
        out_spec = pl.BlockSpec((tn // sub, tm, sub), lambda j, i: (j, i, 0))
    else:
        out_shape = jax.ShapeDtypeStruct((m, ncols), out_dtype)
        out_spec = pl.BlockSpec((tm, tn), lambda j, i: (i, j))
    return pl.pallas_call(
        functools.partial(_mm_body, tiled_out=tiled_out),
        grid=(nj, m // tm),
        in_specs=[pl.BlockSpec((tm, k), lambda j, i: (i, 0)),
                  pl.BlockSpec((k, tn), lambda j, i: (0, j + j0))],
        out_specs=out_spec,
        out_shape=out_shape,
        scratch_shapes=[pltpu.VMEM((k, tn), BF16)],
        compiler_params=_cparams(("arbitrary", "arbitrary")),
        name=name,
    )(x, w)


def _glu_body(x_ref, wv_ref, wg_ref, o_ref, wvb_ref, wgb_ref):
    @pl.when(pl.program_id(1) == 0)
    def _():
        wvb_ref[...] = wv_ref[...].astype(BF16)
        wgb_ref[...] = wg_ref[...].astype(BF16)

    x = x_ref[...].astype(BF16)
    o_ref[...] = _dot(x, wvb_ref[...]) * _sigmoid(_dot(x, wgb_ref[...]))


def glu_matmul(x, w, *, tn=512, tm=512):
    m, k = x.shape
    n = w.shape[1] // 2
    tm = min(tm, m)
    nj = n // tn
    return pl.pallas_call(
        _glu_body,
        grid=(nj, m // tm),
        in_specs=[pl.BlockSpec((tm, k), lambda j, i: (i, 0)),
                  pl.BlockSpec((k, tn), lambda j, i: (0, j)),
                  pl.BlockSpec((k, tn), lambda j, i: (0, j + nj))],
        out_specs=pl.BlockSpec((tm, tn), lambda j, i: (i, j)),
        out_shape=jax.ShapeDtypeStruct((m, n), F32),
        scratch_shapes=[pltpu.VMEM((k, tn), BF16), pltpu.VMEM((k, tn), BF16)],
        compiler_params=_cparams(("arbitrary", "arbitrary")),
        name="glu_matmul",
    )(x, w, w)


def _layer_norm(v, g, b):
    mu = jnp.mean(v, axis=-1, keepdims=True)
    vc = v - mu
    var = jnp.mean(vc * vc, axis=-1, keepdims=True)
    return vc * lax.rsqrt(var + LN_EPS) * g + b


def _route(x, wr, br, carry):
    lg = _dot(x, wr, precision=HIGHEST) + br
    tm = lg.shape[0]
    lane = lax.broadcasted_iota(jnp.int32, lg.shape, 1)
    lanef = lane.astype(F32)
    big = jnp.float32(1e9)
    is_g = lane < MOE_G
    gl = jnp.where(is_g, lg, NEG)
    gmax = jnp.max(gl, axis=-1, keepdims=True)
    gsel = jnp.min(jnp.where(is_g & (gl == gmax), lanef, big), axis=-1, keepdims=True)
    gprob = 1.0 / jnp.sum(jnp.where(is_g, jnp.exp(gl - gmax), 0.0), axis=-1, keepdims=True)
    lo = MOE_G + MOE_PG * gsel
    is_e = (lanef >= lo) & (lanef < lo + MOE_PG)
    el = jnp.where(is_e, lg, NEG)
    m1 = jnp.max(el, axis=-1, keepdims=True)
    l1 = jnp.min(jnp.where(is_e & (el == m1), lanef, big), axis=-1, keepdims=True)
    is_e2 = is_e & (lanef != l1)
    el2 = jnp.where(is_e2, lg, NEG)
    m2 = jnp.max(el2, axis=-1, keepdims=True)
    l2 = jnp.min(jnp.where(is_e2 & (el2 == m2), lanef, big), axis=-1, keepdims=True)
    r = jnp.exp(m2 - m1)
    w1 = gprob / (1.0 + r)
    w2 = gprob * r / (1.0 + r)
    hit1 = lanef == l1
    hit2 = lanef == l2
    oh = jnp.where(hit1 | hit2, 1.0, 0.0)
    rr = lax.broadcasted_iota(jnp.int32, (tm, tm), 0)
    cc = lax.broadcasted_iota(jnp.int32, (tm, tm), 1)
    before = _dot(jnp.where(cc < rr, 1.0, 0.0).astype(BF16), oh.astype(BF16)) + carry
    k1 = jnp.sum(jnp.where(hit1, before, 0.0), axis=-1, keepdims=True)
    k2 = jnp.sum(jnp.where(hit2, before, 0.0), axis=-1, keepdims=True)
    table = jnp.where(lane == 0, l1 - MOE_G,
                      jnp.where(lane == 1, l2 - MOE_G,
                                jnp.where(lane == 2, w1,
                                          jnp.where(lane == 3, w2,
                                                    jnp.where(lane == 4, k1, jnp.where(lane == 5, k2, 0.0))))))
    return table, carry + jnp.sum(oh, axis=0, keepdims=True)


def _ln_route_body(x_ref, h_ref, g_ref, b_ref, wr_ref, br_ref, o_ref, r_ref, rt_ref, cnt_ref):
    @pl.when(pl.program_id(0) == 0)
    def _():
        cnt_ref[...] = jnp.zeros_like(cnt_ref)

    y = _layer_norm(DN_ALPHA * x_ref[...] + h_ref[...], g_ref[...], b_ref[...])
    o_ref[...] = y
    table, cnt_ref[...] = _route(y, wr_ref[...], br_ref[...], cnt_ref[...])
    r_ref[...] = table
    rt_ref[...] = jnp.concatenate([table[128 * q:128 * (q + 1)].T[:8] for q in range(table.shape[0] // 128)],
                                  axis=1)


def ln_route(x, h, g, b, wr, br, tm=256):
    m = x.shape[0]
    row = pl.BlockSpec((tm, D), lambda i: (i, 0))
    vec = pl.BlockSpec((1, D), lambda i: (0, 0))
    one = pl.BlockSpec((1, 128), lambda i: (0, 0))
    return pl.pallas_call(
        _ln_route_body,
        grid=(m // tm,),
        in_specs=[row, row, vec, vec, pl.BlockSpec((D, 128), lambda i: (0, 0)), one],
        out_specs=[row, pl.BlockSpec((tm, 128), lambda i: (i, 0)), pl.BlockSpec((8, tm), lambda i: (0, i)), one],
        out_shape=[jax.ShapeDtypeStruct((m, D), F32), jax.ShapeDtypeStruct((m, 128), F32),
                   jax.ShapeDtypeStruct((8, m), F32), jax.ShapeDtypeStruct((1, 128), F32)],
        compiler_params=_cparams(("arbitrary",)),
        name="ln_route",
    )(x, h, g.reshape(1, D), b.reshape(1, D), wr, br)


def _ln_combine_body(x_ref, y0_ref, y1_ref, r_ref, g_ref, b_ref, o_ref, ob_ref):
    r = r_ref[...]
    f = r[:, 2:3] * y0_ref[...] + r[:, 3:4] * y1_ref[...]
    y = _layer_norm(DN_ALPHA * x_ref[...] + f, g_ref[...], b_ref[...])
    o_ref[...] = y
    ob_ref[...] = y.astype(BF16)


def ln_combine(x, y0, y1, route, g, b, tm=256):
    m = x.shape[0]
    row = pl.BlockSpec((tm, D), lambda i: (i, 0))
    vec = pl.BlockSpec((1, D), lambda i: (0, 0))
    return pl.pallas_call(
        _ln_combine_body,
        grid=(m // tm,),
        in_specs=[row, row, row, pl.BlockSpec((tm, 128), lambda i: (i, 0)), vec, vec],
        out_specs=[row, row],
        out_shape=[jax.ShapeDtypeStruct((m, D), F32), jax.ShapeDtypeStruct((m, D), BF16)],
        compiler_params=_cparams(("arbitrary",)),
        name="ln_combine",
    )(x, y0, y1, route, g.reshape(1, D), b.reshape(1, D))


def _ssd_body(z_ref, x_ref, b_ref, c_ref, dtg_ref, dtt_ref, csx_ref, csb_ref, csc_ref, h0_ref,
              cwx_ref, cwb_ref, cwc_ref, cbx_ref, cbb_ref, cbc_ref, dtbr_ref, dtbc_ref,
              alr_ref, alc_ref, dsk_ref, nw_ref, *rest, C, aliased):
    if aliased:
        rest = rest[1:]
    (y_ref, cox_ref, cob_ref, coc_ref, ho_ref,
     tx_ref, tb_ref, tc_ref, ex_ref, eb_ref, ec_ref, ht_ref, yb_ref) = rest
    c = pl.program_id(1)
    last_chunk = c == pl.num_programs(1) - 1

    @pl.when(c == 0)
    def _init():
        tx_ref[...] = jnp.zeros_like(tx_ref)
        tb_ref[...] = jnp.zeros_like(tb_ref)
        tc_ref[...] = jnp.zeros_like(tc_ref)
        for g in range(SSD_G):
            tx_ref[g, 5:8, :] = csx_ref[0, g]
            tb_ref[g, 5:8, :] = csb_ref[0, g]
            tc_ref[g, 5:8, :] = csc_ref[0, g]
        for p in range(SSD_HEADS // 2):
            ht_ref[p] = h0_ref[0, p].T

    tril = _tri(C)
    tril_f = tril.astype(F32)
    triu_f = _tri(C, upper=True).astype(F32)
    lane = lax.broadcasted_iota(jnp.int32, (C, 128), 1)
    left = lane < 64

    def conv(e_ref, t_ref, raw, w_ref, bias_ref, g):
        e_ref[0:8, :] = t_ref[g]
        e_ref[8:8 + C, :] = raw
        w = w_ref[g]
        acc = bias_ref[g] + w[3:4, :] * raw
        for k in range(SSD_CONV - 1):
            acc = acc + w[k:k + 1, :] * e_ref[5 + k:5 + k + C, :]
        t_ref[g] = e_ref[C:C + 8, :]
        return _silu(acc)

    def group(g, carry):
        xs = conv(ex_ref, tx_ref, x_ref[g], cwx_ref, cbx_ref, g)
        bs = conv(eb_ref, tb_ref, b_ref[g], cwb_ref, cbb_ref, g)
        cs = conv(ec_ref, tc_ref, c_ref[g], cwc_ref, cbc_ref, g)

        @pl.when(last_chunk)
        def _():
            cox_ref[0, g] = ex_ref[C + 5:C + 8, :]
            cob_ref[0, g] = eb_ref[C + 5:C + 8, :]
            coc_ref[0, g] = ec_ref[C + 5:C + 8, :]

        dtv = _softplus(dtg_ref[0, g] + dtbr_ref[g])
        dtvt = _softplus(dtt_ref[0, g] + dtbc_ref[g])
        cum = _dot(tril_f, dtv * (-jnp.exp(alr_ref[g])), precision=HIGHEST)
        cumt = _dot(dtvt * (-jnp.exp(alc_ref[g])), triu_f, precision=HIGHEST)
        bsb = bs.astype(BF16)
        csb = cs.astype(BF16)
        cb = _dot_nt(csb, bsb)
        bst = _transpose_rows(bs).astype(BF16)
        dsk = dsk_ref[g]
        ys = []
        for j in range(SSD_HPG // 2):
            h0, h1 = 2 * j, 2 * j + 1
            c0, c1 = cum[:, h0:h0 + 1], cum[:, h1:h1 + 1]
            l0 = jnp.where(tril, jnp.exp(c0 - cumt[h0:h0 + 1, :]), 0.0) * cb
            l1 = jnp.where(tril, jnp.exp(c1 - cumt[h1:h1 + 1, :]), 0.0) * cb
            lhs = jnp.concatenate([l0, l1], axis=1).astype(BF16)
            xp = xs[:, 128 * j:128 * (j + 1)]
            xdt = xp * jnp.where(left, dtv[:, h0:h0 + 1], dtv[:, h1:h1 + 1])
            rhs = jnp.concatenate([jnp.where(left, xdt, 0.0), jnp.where(left, 0.0, xdt)],
                                  axis=0).astype(BF16)
            htp = ht_ref[g * 4 + j]
            y = _dot(lhs, rhs)
            y = y + _dot(csb, htp.astype(BF16)) * jnp.where(left, jnp.exp(c0), jnp.exp(c1))
            e0, e1 = cum[C - 1:C, h0:h0 + 1], cum[C - 1:C, h1:h1 + 1]
            wgt = (xdt * jnp.where(left, jnp.exp(e0 - c0), jnp.exp(e1 - c1))).astype(BF16)
            ht_ref[g * 4 + j] = jnp.where(left[0:1, :], jnp.exp(e0), jnp.exp(e1)) * htp + _dot(bst, wgt)
            ys.append(y + dsk[:, 128 * j:128 * (j + 1)] * xp)
        y = jnp.concatenate(ys, axis=1) * _silu(z_ref[g])
        ms = jnp.mean(y * y, axis=-1, keepdims=True)
        yb_ref[g] = (y * lax.rsqrt(ms + RMS_EPS) * nw_ref[g]).astype(BF16)
        return carry

    lax.fori_loop(0, SSD_G, group, 0)
    for g in range(SSD_G):
        y_ref[:, SSD_GW * g:SSD_GW * (g + 1)] = yb_ref[g]

    @pl.when(last_chunk)
    def _fin():
        for p in range(SSD_HEADS // 2):
            ho_ref[0, p] = ht_ref[p].T


def ssd_core(zx, bc, dt, conv_state, h0, params, *, row0, streams, length, chunk, y_prev=None):
    conv_w, conv_b, dt_bias, a_log, d_skip, norm_w = params
    S, L, C = streams, length, chunk
    nch = L // C
    rb0 = row0 // C
    t_all = zx.shape[1]
    dseg = dt[row0:row0 + S * L].reshape(S, L, SSD_G, SSD_HPG)
    dtg = dseg.transpose(0, 2, 1, 3)
    dtt = dseg.transpose(0, 2, 3, 1)

    def split(a, lead):
        ax = a[..., :SSD_INNER].reshape(lead + (SSD_G, SSD_GW))
        ab = a[..., SSD_INNER:SSD_INNER + SSD_G * SSD_N].reshape(lead + (SSD_G, SSD_N))
        ac = a[..., SSD_INNER + SSD_G * SSD_N:].reshape(lead + (SSD_G, SSD_N))
        return ax, ab, ac

    csx, csb, csc = (jnp.moveaxis(a, 2, 1) for a in split(conv_state, (S, SSD_CONV - 1)))
    cwx, cwb, cwc = (jnp.moveaxis(a, 1, 0) for a in split(conv_w, (SSD_CONV,)))
    cbx, cbb, cbc = (jnp.moveaxis(a, 1, 0) for a in split(conv_b.reshape(1, -1), (1,)))
    dtbr = dt_bias.reshape(SSD_G, 1, SSD_HPG)
    dtbc = dt_bias.reshape(SSD_G, SSD_HPG, 1)
    alr = a_log.reshape(SSD_G, 1, SSD_HPG)
    alc = a_log.reshape(SSD_G, SSD_HPG, 1)
    dsk = jnp.repeat(d_skip, SSD_INNER // SSD_HEADS).reshape(SSD_G, 1, SSD_GW)
    nw = norm_w.reshape(SSD_G, 1, SSD_GW)
    h0p = h0.reshape(S, SSD_HEADS // 2, 128, SSD_N)

    def rb(s, c):
        return rb0 + s * nch + c

    def full(a):
        nd = a.ndim
        return pl.BlockSpec(a.shape, lambda s, c: (0,) * nd)

    def per_stream(a):
        nd = a.ndim
        return pl.BlockSpec((1,) + a.shape[1:], lambda s, c: (s,) + (0,) * (nd - 1))

    in_specs = [
        pl.BlockSpec((SSD_G, C, SSD_GW), lambda s, c: (0, rb(s, c), 0)),
        pl.BlockSpec((SSD_G, C, SSD_GW), lambda s, c: (1, rb(s, c), 0)),
        pl.BlockSpec((SSD_G, C, SSD_N), lambda s, c: (0, rb(s, c), 0)),
        pl.BlockSpec((SSD_G, C, SSD_N), lambda s, c: (1, rb(s, c), 0)),
        pl.BlockSpec((1, SSD_G, C, SSD_HPG), lambda s, c: (s, 0, c, 0)),
        pl.BlockSpec((1, SSD_G, SSD_HPG, C), lambda s, c: (s, 0, 0, c)),
        per_stream(csx), per_stream(csb), per_stream(csc), per_stream(h0p),
        full(cwx), full(cwb), full(cwc), full(cbx), full(cbb), full(cbc),
        full(dtbr), full(dtbc), full(alr), full(alc), full(dsk), full(nw),
    ]
    args = [zx, zx, bc, bc, dtg, dtt, csx, csb, csc, h0p, cwx, cwb, cwc, cbx, cbb, cbc,
            dtbr, dtbc, alr, alc, dsk, nw]
    aliases = {}
    if y_prev is not None:
        in_specs.append(pl.BlockSpec(memory_space=pl.ANY))
        args.append(y_prev)
        aliases = {len(args) - 1: 0}
    out_shape = [
        jax.ShapeDtypeStruct((t_all, SSD_INNER), BF16),
        jax.ShapeDtypeStruct(csx.shape, F32), jax.ShapeDtypeStruct(csb.shape, F32),
        jax.ShapeDtypeStruct(csc.shape, F32), jax.ShapeDtypeStruct(h0p.shape, F32),
    ]
    out_specs = [
        pl.BlockSpec((C, SSD_INNER), lambda s, c: (rb(s, c), 0)),
        per_stream(csx), per_stream(csb), per_stream(csc), per_stream(h0p),
    ]
    scratch = [
        pltpu.VMEM((SSD_G, 8, SSD_GW), F32), pltpu.VMEM((SSD_G, 8, SSD_N), F32),
        pltpu.VMEM((SSD_G, 8, SSD_N), F32),
        pltpu.VMEM((C + 8, SSD_GW), F32), pltpu.VMEM((C + 8, SSD_N), F32), pltpu.VMEM((C + 8, SSD_N), F32),
        pltpu.VMEM((SSD_HEADS // 2, SSD_N, 128), F32),
        pltpu.VMEM((SSD_G, C, SSD_GW), BF16),
    ]
    y, cox, cob, coc, ho = pl.pallas_call(
        functools.partial(_ssd_body, C=C, aliased=y_prev is not None),
        grid=(S, nch), in_specs=in_specs, out_specs=out_specs, out_shape=out_shape,
        scratch_shapes=scratch, input_output_aliases=aliases,
        compiler_params=_cparams(("arbitrary", "arbitrary")),
        name="ssd_core",
    )(*args)
    conv_out = jnp.concatenate([jnp.moveaxis(a, 1, 2).reshape(S, SSD_CONV - 1, -1) for a in (cox, cob, coc)],
                               axis=-1)
    return y, conv_out, ho.reshape(S, SSD_HEADS, SSD_INNER // SSD_HEADS, SSD_N)


def _swa_body(sink_ref, q_ref, pk_ref, pv_ref, kv_ref, *rest, QT, prev_valid, aliased):
    o_ref = rest[-1]
    i = pl.program_id(0)
    nk = WINDOW + QT
    qc = lax.broadcasted_iota(jnp.int32, (QT, nk), 0) // CHUNK
    kc = lax.broadcasted_iota(jnp.int32, (QT, nk), 1) // CHUNK - WINDOW // CHUNK
    valid = (kc <= qc) & (kc >= qc - WINDOW // CHUNK)
    if not prev_valid:
        valid = valid & ((kc >= 0) | (i > 0))
    kcat = jnp.concatenate([pk_ref[...], kv_ref[:, :SWA_KVH * SWA_DH]], axis=0).astype(BF16)
    vcat = jnp.concatenate([pv_ref[...], kv_ref[:, SWA_KVH * SWA_DH:]], axis=0).astype(BF16)
    scale = SWA_DH ** -0.5
    outs = []
    for h in range(SWA_QH):
        kh = h // SWA_GRP
        q = q_ref[:, SWA_DH * h:SWA_DH * (h + 1)]
        k = kcat[:, SWA_DH * kh:SWA_DH * (kh + 1)]
        v = vcat[:, SWA_DH * kh:SWA_DH * (kh + 1)]
        s = jnp.where(valid, _dot_nt(q, k) * scale, NEG)
        sink = sink_ref[h]
        m = jnp.maximum(jnp.max(s, axis=-1, keepdims=True), sink)
        p = jnp.exp(s - m)
        den = jnp.sum(p, axis=-1, keepdims=True) + jnp.exp(sink - m)
        outs.append(_dot(p.astype(BF16), v) / den)
    o_ref[...] = jnp.concatenate(outs, axis=1).astype(BF16)


def swa_core(q, kv, prev_k, prev_v, sinks, *, row0, tiles, qt, prompt, o_prev=None):
    t_all = q.shape[0]
    rb0 = row0 // qt
    kvw = SWA_KVH * SWA_DH
    if prompt:
        wpt = qt // WINDOW
        prev_map_k = lambda i, s: (jnp.maximum(wpt * (rb0 + i) - 1, 0), 0)
        prev_map_v = lambda i, s: (jnp.maximum(wpt * (rb0 + i) - 1, 0), 1)
        pk_spec = pl.BlockSpec((WINDOW, kvw), prev_map_k)
        pv_spec = pl.BlockSpec((WINDOW, kvw), prev_map_v)
        prev_k = prev_v = kv
    else:
        pk_spec = pl.BlockSpec((None, WINDOW, kvw), lambda i, s: (i, 0, 0))
        pv_spec = pl.BlockSpec((None, WINDOW, kvw), lambda i, s: (i, 0, 0))
    in_specs = [pl.BlockSpec((qt, D), lambda i, s: (rb0 + i, 0)), pk_spec, pv_spec,
                pl.BlockSpec((qt, 2 * kvw), lambda i, s: (rb0 + i, 0))]
    args = [sinks, q, prev_k, prev_v, kv]
    aliases = {}
    if o_prev is not None:
        in_specs.append(pl.BlockSpec(memory_space=pl.ANY))
        args.append(o_prev)
        aliases = {len(args) - 1: 0}
    return pl.pallas_call(
        functools.partial(_swa_body, QT=qt, prev_valid=not prompt, aliased=o_prev is not None),
        grid_spec=pltpu.PrefetchScalarGridSpec(
            num_scalar_prefetch=1, grid=(tiles,), in_specs=in_specs,
            out_specs=pl.BlockSpec((qt, D), lambda i, s: (rb0 + i, 0))),
        out_shape=jax.ShapeDtypeStruct((t_all, D), BF16),
        input_output_aliases=aliases,
        compiler_params=_cparams(("arbitrary",)),
        name="swa_core",
    )(*args)


def _gelu_tanh(y):
    return 0.5 * y * (1.0 + jnp.tanh(0.7978845608028654 * (y + 0.044715 * y * y * y)))


def _s5_body(u_ref, wxr_ref, wxi_ref, wyr_ref, wyi_ref, kt_ref, a_ref, dsk_ref, hr0_ref, hi0_ref, *rest,
             S, R, aliased):
    if aliased:
        rest = rest[1:]
    (y_ref, hro_ref, hio_ref, wx_s, wy_s, ktm_s, xr_s, xi_s, pr_s, pi_s, hr_s, hi_s) = rest
    kb = pl.program_id(1)
    half = S5_CW * S5_N // S5_CH

    @pl.when((pl.program_id(0) == 0) & (kb == 0))
    def _zero():
        ktm_s[...] = jnp.zeros_like(ktm_s)

    @pl.when(kb == 0)
    def _build():
        own = (lax.broadcasted_iota(jnp.int32, (S5_CW, half), 0) // S5_CH
               == lax.broadcasted_iota(jnp.int32, (S5_CW, half), 1) // S5_N)
        for s in range(S5_SUB):
            rows = slice(S5_CW * s, S5_CW * (s + 1))
            wx_s[rows, 0:half] = jnp.where(own, jnp.concatenate([wxr_ref[s]] * 4, axis=1), 0.0).astype(BF16)
            wx_s[rows, half:2 * half] = jnp.where(own, jnp.concatenate([wxi_ref[s]] * 4, axis=1), 0.0).astype(BF16)
        own_t = (lax.broadcasted_iota(jnp.int32, (half, S5_CW), 0) // S5_N
                 == lax.broadcasted_iota(jnp.int32, (half, S5_CW), 1) // S5_CH)
        for t in range(S5_SUB):
            cols = slice(S5_CW * t, S5_CW * (t + 1))
            wy_s[0:half, cols] = jnp.where(own_t, jnp.concatenate([wyr_ref[t]] * 8, axis=0), 0.0).astype(BF16)
            wy_s[half:2 * half, cols] = jnp.where(own_t, jnp.concatenate([wyi_ref[t]] * 8, axis=0), 0.0).astype(BF16)
        same = (lax.broadcasted_iota(jnp.int32, (S5_CW, S5_CW), 0) // S5_CH
                == lax.broadcasted_iota(jnp.int32, (S5_CW, S5_CW), 1) // S5_CH)
        taps = [jnp.where(same, kt_ref[tau], 0.0).astype(BF16) for tau in range(S5_SUB)]
        for s in range(S5_SUB):
            for t in range(s, S5_SUB):
                ktm_s[S5_CW * s:S5_CW * (s + 1), S5_CW * t:S5_CW * (t + 1)] = taps[t - s]
        for st in range(S):
            hr_s[st] = hr0_ref[st]
            hi_s[st] = hi0_ref[st]

    kr = S * R
    ucat = jnp.concatenate([u_ref[pl.ds(s, kr, stride=S5_SUB), :] for s in range(S5_SUB)],
                           axis=1)
    ub = ucat.astype(BF16)
    x = _dot(ub, wx_s[...])
    xr_s[...] = x[:, :half]
    xi_s[...] = x[:, half:]
    ar = a_ref[0:1, :]
    ai = a_ref[1:2, :]
    for st in range(S):
        def step(k, carry):
            hr, hi = carry
            row = st * R + k
            pr_s[pl.ds(row, 1), :] = hr
            pi_s[pl.ds(row, 1), :] = hi
            nr = ar * hr - ai * hi + xr_s[pl.ds(row, 1), :]
            ni = ar * hi + ai * hr + xi_s[pl.ds(row, 1), :]
            return nr, ni

        hr, hi = lax.fori_loop(0, R, step, (hr_s[st], hi_s[st]))
        hr_s[st] = hr
        hi_s[st] = hi
    hprev = jnp.concatenate([pr_s[...], pi_s[...]], axis=1).astype(BF16)
    dsk = jnp.concatenate([dsk_ref[...]] * S5_SUB, axis=1)
    y = _gelu_tanh(_dot(ub, ktm_s[...]) + _dot(hprev, wy_s[...]) + dsk * ucat)
    for t in range(S5_SUB):
        y_ref[pl.ds(t, kr, stride=S5_SUB), :] = y[:, S5_CW * t:S5_CW * (t + 1)]

    @pl.when(kb == pl.num_programs(1) - 1)
    def _fin():
        for st in range(S):
            hro_ref[st] = hr_s[st]
            hio_ref[st] = hi_s[st]


def s5_tables(p):
    a_re, a_im, log_dt, b_re, b_im, c_re, c_im, d_skip = p
    lr, li = a_re.astype(F32), a_im.astype(F32)
    dt = jnp.exp(log_dt.astype(F32))[:, None]
    mag = jnp.exp(lr * dt)
    ab_r, ab_i = mag * jnp.cos(li * dt), mag * jnp.sin(li * dt)
    den = lr * lr + li * li
    co_r = ((ab_r - 1.0) * lr + ab_i * li) / den
    co_i = (ab_i * lr - (ab_r - 1.0) * li) / den
    bb_r = co_r[..., None] * b_re - co_i[..., None] * b_im
    bb_i = co_r[..., None] * b_im + co_i[..., None] * b_re
    pw_r, pw_i = [jnp.ones_like(ab_r)], [jnp.zeros_like(ab_i)]
    for _ in range(S5_SUB):
        r, i = pw_r[-1], pw_i[-1]
        pw_r.append(ab_r * r - ab_i * i)
        pw_i.append(ab_r * i + ab_i * r)
    pr, pi = jnp.stack(pw_r, 0), jnp.stack(pw_i, 0)
    er, ei = pr[S5_SUB - 1::-1][:S5_SUB], pi[S5_SUB - 1::-1][:S5_SUB]
    wx_r = er[..., None] * bb_r[None] - ei[..., None] * bb_i[None]
    wx_i = er[..., None] * bb_i[None] + ei[..., None] * bb_r[None]
    gb, gw = S5_G // S5_GB, S5_GB
    wx_r = wx_r.reshape(S5_SUB, gb, gw, S5_N, S5_CH).transpose(1, 0, 2, 4, 3).reshape(gb, S5_SUB, S5_CW, S5_N)
    wx_i = wx_i.reshape(S5_SUB, gb, gw, S5_N, S5_CH).transpose(1, 0, 2, 4, 3).reshape(gb, S5_SUB, S5_CW, S5_N)
    wxr = jnp.concatenate([wx_r, wx_r], axis=-1)
    wxi = jnp.concatenate([wx_i, wx_i], axis=-1)
    qr, qi = pr[1:], pi[1:]
    cr, ci = c_re.astype(F32), c_im.astype(F32)
    wy_r = cr[None] * qr[:, :, None, :] - ci[None] * qi[:, :, None, :]
    wy_i = -(cr[None] * qi[:, :, None, :] + ci[None] * qr[:, :, None, :])
    wyr = wy_r.reshape(S5_SUB, gb, gw, S5_CH, S5_N).transpose(1, 0, 4, 2, 3).reshape(gb, S5_SUB, S5_N, S5_CW)
    wyi = wy_i.reshape(S5_SUB, gb, gw, S5_CH, S5_N).transpose(1, 0, 4, 2, 3).reshape(gb, S5_SUB, S5_N, S5_CW)
    tr = pr[:S5_SUB, :, None, :] * cr[None] - pi[:S5_SUB, :, None, :] * ci[None]
    ti = pr[:S5_SUB, :, None, :] * ci[None] + pi[:S5_SUB, :, None, :] * cr[None]
    taps = jnp.einsum('agjn,gnk->agjk', tr, bb_r) - jnp.einsum('agjn,gnk->agjk', ti, bb_i)
    taps = taps.reshape(S5_SUB, gb, gw, S5_CH, S5_CH).transpose(1, 0, 2, 4, 3)
    kt = jnp.tile(taps.reshape(gb, S5_SUB, S5_CW, S5_CH), (1, 1, 1, gw))
    a16 = jnp.stack([pr[S5_SUB].reshape(gb, gw * S5_N), pi[S5_SUB].reshape(gb, gw * S5_N)], axis=1)
    dsk = d_skip.astype(F32).reshape(gb, 1, S5_CW)
    return wxr, wxi, wyr, wyi, kt, a16, dsk


def s5_core(u, s_re, s_im, tables, *, row0, streams, length, y_prev=None):
    wxr, wxi, wyr, wyi, kt, a16, dsk = tables
    S, L = streams, length
    gb = S5_G // S5_GB
    rows = S * L // S5_SUB
    if S == 1:
        kr = min(S5_KR, rows)
        spb, rps = 1, kr
    else:
        kr = rows
        spb, rps = S, L // S5_SUB
    nkb = rows // kr
    kb0 = row0 // S5_SUB // kr
    half = S5_GB * S5_N
    hr0 = s_re.reshape(S, gb, 1, half)
    hi0 = s_im.reshape(S, gb, 1, half)
    tab = lambda a: pl.BlockSpec((None,) + a.shape[1:], lambda p, k: (p,) + (0,) * (a.ndim - 1))
    st = pl.BlockSpec((S, None, 1, half), lambda p, k: (0, p, 0, 0))
    uspec = pl.BlockSpec((kr * S5_SUB, S5_CW), lambda p, k: (kb0 + k, p))
    in_specs = [uspec, tab(wxr), tab(wxi), tab(wyr), tab(wyi), tab(kt), tab(a16), tab(dsk), st, st]
    args = [u, wxr, wxi, wyr, wyi, kt, a16, dsk, hr0, hi0]
    aliases = {}
    if y_prev is not None:
        in_specs.append(pl.BlockSpec(memory_space=pl.ANY))
        args.append(y_prev)
        aliases = {len(args) - 1: 0}
    wide = S5_SUB * S5_CW
    y, hro, hio = pl.pallas_call(
        functools.partial(_s5_body, S=spb, R=rps, aliased=y_prev is not None),
        grid=(gb, nkb),
        in_specs=in_specs,
        out_specs=[uspec, st, st],
        out_shape=[jax.ShapeDtypeStruct(u.shape, F32), jax.ShapeDtypeStruct(hr0.shape, F32),
                   jax.ShapeDtypeStruct(hi0.shape, F32)],
        scratch_shapes=[pltpu.VMEM((wide, 2 * half), BF16), pltpu.VMEM((2 * half, wide), BF16),
                        pltpu.VMEM((wide, wide), BF16)]
        + [pltpu.VMEM((kr, half), F32) for _ in range(4)]
        + [pltpu.VMEM((spb, 1, half), F32) for _ in range(2)],
        input_output_aliases=aliases,
        compiler_params=_cparams(("arbitrary", "arbitrary")),
        name="s5_core",
    )(*args)
    return y, hro.reshape(S, S5_G, S5_N), hio.reshape(S, S5_G, S5_N)


def _hgrn_body(q_ref, f_ref, i_ref, g_ref, lbp_ref, nw_ref, s0_ref, *rest, C, layer, aliased):
    if aliased:
        rest = rest[1:]
    o_ref, so_ref, st_ref = rest
    c = pl.program_id(1)

    @pl.when(c == 0)
    def _init():
        for h in range(HG_H):
            st_ref[h] = s0_ref[0, h].T

    lbp = lbp_ref[...]
    e = jnp.exp(lbp - jnp.max(lbp, axis=0, keepdims=True))
    lbs = e / jnp.sum(e, axis=0, keepdims=True)
    lb = jnp.zeros((1, D), F32)
    for r in range(1, layer + 1):
        lb = lb + lbs[r:r + 1, :]
    fz = f_ref[...]
    log_sig = jnp.minimum(fz, 0.0) - jnp.log(1.0 + jnp.exp(-jnp.abs(fz)))
    la = jnp.log(lb)
    lbb = jnp.log(1.0 - lb) + log_sig
    mx = jnp.maximum(la, lbb)
    logf = mx + jnp.log(1.0 + jnp.exp(-jnp.abs(la - lbb)))
    kk = 1.0 - jnp.exp(logf)
    qs = _silu(q_ref[...])
    tril = _tri(C)
    cum = _dot(tril.astype(F32), logf, precision=HIGHEST)
    row = lax.broadcasted_iota(jnp.int32, (C, C), 0)
    col = lax.broadcasted_iota(jnp.int32, (C, C), 1)

    levels = []
    b = C
    while b >= HG_LEAF:
        nb = C // b
        ref = jnp.broadcast_to(cum.reshape(nb, b, D)[:, b // 2 - 1:b // 2, :], (nb, b, D)).reshape(C, D)
        ex = cum - ref
        same = (row // b) == (col // b)
        if b == HG_LEAF:
            mask = same & (col <= row)
            qe, ke = jnp.exp(jnp.minimum(ex, 80.0)), jnp.exp(jnp.minimum(-ex, 80.0))
        else:
            mask = same & ((row % b) >= b // 2) & ((col % b) < b // 2)
            qe, ke = jnp.exp(jnp.minimum(ex, 0.0)), jnp.exp(jnp.minimum(-ex, 0.0))
        levels.append((mask, (qs * qe).astype(BF16), (kk * ke).astype(BF16)))
        b //= 2

    last = cum[C - 1:C, :]
    qin = (qs * jnp.exp(cum)).astype(BF16)
    kin = (kk * jnp.exp(last - cum)).astype(BF16)
    dec = jnp.exp(last)
    vv = i_ref[...]
    vb = vv.astype(BF16)
    gate = _silu(g_ref[...])
    nw = nw_ref[...]
    outs = []
    for h in range(HG_H):
        sl = slice(HG_K * h, HG_K * (h + 1))
        att = jnp.zeros((C, C), F32)
        for mask, ql, kl in levels:
            att = att + jnp.where(mask, _dot_nt(ql[:, sl], kl[:, sl]), 0.0)
        st = st_ref[h]
        o = _dot(att.astype(BF16), vb[:, sl]) + _dot_nt(qin[:, sl], st.astype(BF16))
        st_ref[h] = st * dec[:, sl] + _dot(_transpose_rows(vv[:, sl]).astype(BF16), kin[:, sl])
        ms = jnp.mean(o * o, axis=-1, keepdims=True)
        outs.append(o * lax.rsqrt(ms + RMS_EPS) * nw)
    o_ref[...] = (jnp.concatenate(outs, axis=1) * gate).astype(BF16)

    @pl.when(c == pl.num_programs(1) - 1)
    def _fin():
        for h in range(HG_H):
            so_ref[0, h] = st_ref[h].T


def hgrn_core(qfig, lb_param, norm_w, s0, *, layer, row0, streams, length, chunk, o_prev=None):
    S, L, C = streams, length, chunk
    nch = L // C
    rb0 = row0 // C
    t_all = qfig.shape[0]
    rowspec = lambda j: pl.BlockSpec((C, D), lambda s, c: (rb0 + s * nch + c, j))
    in_specs = [rowspec(0), rowspec(1), rowspec(2), rowspec(3),
                pl.BlockSpec((DEPTH, D), lambda s, c: (0, 0)),
                pl.BlockSpec((1, HG_K), lambda s, c: (0, 0)),
                pl.BlockSpec((1, HG_H, HG_K, HG_K), lambda s, c: (s, 0, 0, 0))]
    args = [qfig, qfig, qfig, qfig, lb_param, norm_w.reshape(1, HG_K), s0]
    aliases = {}
    if o_prev is not None:
        in_specs.append(pl.BlockSpec(memory_space=pl.ANY))
        args.append(o_prev)
        aliases = {len(args) - 1: 0}
    return pl.pallas_call(
        functools.partial(_hgrn_body, C=C, layer=layer, aliased=o_prev is not None),
        grid=(S, nch), in_specs=in_specs,
        out_specs=[rowspec(0), pl.BlockSpec((1, HG_H, HG_K, HG_K), lambda s, c: (s, 0, 0, 0))],
        out_shape=[jax.ShapeDtypeStruct((t_all, D), BF16), jax.ShapeDtypeStruct(s0.shape, F32)],
        scratch_shapes=[pltpu.VMEM((HG_H, HG_K, HG_K), F32)],
        input_output_aliases=aliases,
        compiler_params=_cparams(("arbitrary", "arbitrary")),
        name="hgrn_core",
    )(*args)


def _expert_body(te_ref, nt_ref, x_ref, wg_ref, wu_ref, wd_ref, o_ref, wgb_ref, wub_ref, wdb_ref):
    i = pl.program_id(0)
    prev = te_ref[jnp.maximum(i - 1, 0)]

    @pl.when((i == 0) | (te_ref[i] != prev))
    def _():
        wgb_ref[...] = wg_ref[0, 0].astype(BF16)
        wub_ref[...] = wu_ref[0, 0].astype(BF16)
        wdb_ref[...] = wd_ref[0, 0].astype(BF16)

    @pl.when(i < nt_ref[0])
    def _():
        x = x_ref[...].astype(BF16)
        h = _silu(_dot(x, wgb_ref[...])) * _dot(x, wub_ref[...])
        o_ref[...] = _dot(h.astype(BF16), wdb_ref[...])

    @pl.when(i >= nt_ref[0])
    def _():
        o_ref[...] = jnp.zeros_like(o_ref)


def expert_mlp(xs, tile_expert, n_tiles, w_gate, w_up, w_down, layer):
    p = xs.shape[0]
    nt = p // MOE_TM
    wspec = lambda shp: pl.BlockSpec((1, 1) + shp, lambda i, te, n: (layer, te[i], 0, 0))
    return pl.pallas_call(
        _expert_body,
        grid_spec=pltpu.PrefetchScalarGridSpec(
            num_scalar_prefetch=2, grid=(nt,),
            in_specs=[pl.BlockSpec((MOE_TM, D), lambda i, te, n: (i, 0)),
                      wspec((D, D_EXPERT)), wspec((D, D_EXPERT)), wspec((D_EXPERT, D))],
            out_specs=pl.BlockSpec((MOE_TM, D), lambda i, te, n: (i, 0)),
            scratch_shapes=[pltpu.VMEM((D, D_EXPERT), BF16), pltpu.VMEM((D, D_EXPERT), BF16),
                            pltpu.VMEM((D_EXPERT, D), BF16)]),
        out_shape=jax.ShapeDtypeStruct((p, D), F32),
        compiler_params=_cparams(("arbitrary",)),
        name="expert_mlp",
    )(tile_expert, n_tiles, xs, w_gate, w_up, w_down)


def moe_layer(x1, route_t, counts, w_gate, w_up, w_down, layer):
    t = x1.shape[0]
    e0 = route_t[0].astype(jnp.int32)
    e1 = route_t[1].astype(jnp.int32)
    cnt = counts[0, MOE_G:MOE_G + MOE_E].astype(jnp.int32)
    padded = (cnt + MOE_TM - 1) // MOE_TM * MOE_TM
    ends = jnp.cumsum(padded)
    starts = ends - padded
    d0 = starts[e0] + route_t[4].astype(jnp.int32)
    d1 = starts[e1] + route_t[5].astype(jnp.int32)
    p_rows = (2 * t + MOE_E * (MOE_TM - 1)) // 512 * 512 + 512
    nt = p_rows // MOE_TM
    tok = jnp.arange(t, dtype=jnp.int32)
    src = (jnp.arange(p_rows, dtype=jnp.int32) % t).at[jnp.concatenate([d0, d1])].set(
        jnp.concatenate([tok, tok]), mode="promise_in_bounds", unique_indices=True)
    tile_start = jnp.arange(nt, dtype=jnp.int32) * MOE_TM
    tile_expert = jnp.minimum(jnp.sum((tile_start[:, None] >= ends[None, :]).astype(jnp.int32), axis=1),
                              MOE_E - 1).astype(jnp.int32)
    n_tiles = (ends[-1] // MOE_TM).astype(jnp.int32).reshape(1)
    xs = x1.at[src].get(mode="promise_in_bounds")
    ys = expert_mlp(xs, tile_expert, n_tiles, w_gate, w_up, w_down, layer)
    return ys.at[d0].get(mode="promise_in_bounds"), ys.at[d1].get(mode="promise_in_bounds")


def _forward(x_prompt, x_sample, states, ssd_p, swa_p, s5_p, hg_p, ln_p, moe_p):
    (state_ssd_conv, state_ssd, cache_k, cache_v, s5_re, s5_im, state_hgrn) = states
    lp = x_prompt.shape[1]
    sb, ls = x_sample.shape[0], x_sample.shape[1]
    t_all = lp + sb * ls
    ln1_g, ln1_b, ln2_g, ln2_b = ln_p
    w_rg, b_rg, w_re, b_re, w_gate, w_up, w_down = moe_p

    x = jnp.concatenate([x_prompt.reshape(lp, D), x_sample.reshape(sb * ls, D)], axis=0)
    xb = x.astype(BF16)
    outs = {}
    for layer in range(DEPTH):
        kind = layer % 4
        if kind == 0:
            w_in, conv_w, conv_b, dt_bias, a_log, d_skip, norm_w, w_out = ssd_p
            zx = matmul(xb, w_in, col0=0, ncols=2 * SSD_INNER, tn=MM_TN, tiled_out=True, sub=SSD_GW,
                        name="mm_ssd_zx")
            bc = matmul(xb, w_in, col0=2 * SSD_INNER, ncols=2 * SSD_G * SSD_N, tn=MM_TN, tiled_out=True,
                        sub=SSD_N, name="mm_ssd_bc")
            dt = matmul(xb, w_in[:, SSD_INNER + SSD_XBC:], tn=SSD_HEADS, name="mm_ssd_dt")
            prm = (conv_w, conv_b, dt_bias, a_log, d_skip, norm_w)
            zc = jnp.zeros((1, SSD_CONV - 1, SSD_XBC), F32)
            zh = jnp.zeros((1, SSD_HEADS, SSD_INNER // SSD_HEADS, SSD_N), F32)
            y, pc, ph = ssd_core(zx, bc, dt, zc, zh, prm, row0=0, streams=1, length=lp, chunk=128)
            y, sc, sh = ssd_core(zx, bc, dt, state_ssd_conv, state_ssd, prm, row0=lp, streams=sb,
                                 length=ls, chunk=ls, y_prev=y)
            outs['conv'], outs['ssd'] = (pc, sc), (ph, sh)
            h = matmul(y, w_out, name="mm_ssd_out")
        elif kind == 1:
            w_qkv, sinks, w_out = swa_p
            kvw = SWA_KVH * SWA_DH
            q = matmul(xb, w_qkv, col0=0, ncols=D, out_dtype=BF16, name="mm_swa_q")
            kv = matmul(xb, w_qkv, col0=D, ncols=2 * kvw, name="mm_swa_kv")
            o = swa_core(q, kv, None, None, sinks, row0=0, tiles=lp // 256, qt=256, prompt=True)
            ck = cache_k.reshape(sb, WINDOW, kvw)
            cv = cache_v.reshape(sb, WINDOW, kvw)
            o = swa_core(q, kv, ck, cv, sinks, row0=lp, tiles=sb, qt=ls, prompt=False, o_prev=o)
            kshape = (SWA_KVH, SWA_DH)
            pk = kv[lp - WINDOW:lp, :kvw].reshape((1, WINDOW) + kshape)
            pv = kv[lp - WINDOW:lp, kvw:].reshape((1, WINDOW) + kshape)
            kvs = kv[lp:].reshape(sb, ls, 2 * kvw)
            sk = jnp.concatenate([ck, kvs[:, :, :kvw]], axis=1)[:, -WINDOW:].reshape((sb, WINDOW) + kshape)
            sv = jnp.concatenate([cv, kvs[:, :, kvw:]], axis=1)[:, -WINDOW:].reshape((sb, WINDOW) + kshape)
            outs['k'], outs['v'] = (pk, sk), (pv, sv)
            h = matmul(o, w_out, name="mm_swa_out")
        elif kind == 2:
            w_in, w_glu = s5_p[0], s5_p[-1]
            tables = s5_tables(s5_p[1:-1])
            u = matmul(xb, w_in, name="mm_s5_in")
            zs = jnp.zeros((1, S5_G, S5_N), F32)
            y, pr, pi = s5_core(u, zs, zs, tables, row0=0, streams=1, length=lp)
            y, sr, si = s5_core(u, s5_re, s5_im, tables, row0=lp, streams=sb, length=ls, y_prev=y)
            outs['s5r'], outs['s5i'] = (pr, sr), (pi, si)
            h = glu_matmul(y, w_glu)
        else:
            w_in, lb_param, norm_w, w_out = hg_p
            qfig = matmul(xb, w_in, name="mm_hg_in")
            zs = jnp.zeros((1, HG_H, HG_K, HG_K), F32)
            o, ps = hgrn_core(qfig, lb_param, norm_w, zs, layer=layer, row0=0, streams=1, length=lp, chunk=128)
            o, ss = hgrn_core(qfig, lb_param, norm_w, state_hgrn, layer=layer, row0=lp, streams=sb,
                              length=ls, chunk=ls, o_prev=o)
            outs['hg'] = (ps, ss)
            h = matmul(o, w_out, name="mm_hg_out")
        wr = jnp.concatenate([w_rg[layer], w_re[layer], jnp.zeros((D, 128 - MOE_G - MOE_E), F32)], axis=1)
        br = jnp.concatenate([b_rg[layer], b_re[layer], jnp.zeros((128 - MOE_G - MOE_E,), F32)]).reshape(1, 128)
        x1, route, route_t, counts = ln_route(x, h, ln1_g[layer], ln1_b[layer], wr, br)
        y0, y1 = moe_layer(x1, route_t, counts, w_gate, w_up, w_down, layer)
        x, xb = ln_combine(x1, y0, y1, route, ln2_g[layer], ln2_b[layer])
    y_prompt = x[:lp].reshape(1, lp, D)
    y_sample = x[lp:].reshape(sb, ls, D)
    order = ('conv', 'ssd', 'k', 'v', 's5r', 's5i', 'hg')
    return (y_prompt, y_sample) + tuple(outs[k][0] for k in order) + tuple(outs[k][1] for k in order)


def kernel(x_prompt, x_sample, state_ssd_conv, state_ssd, cache_swa_k, cache_swa_v, state_s5_re, state_s5_im, state_hgrn, ssd_w_in, ssd_conv_w, ssd_conv_b, ssd_dt_bias, ssd_a_log, ssd_d, ssd_norm_w, ssd_w_out, swa_w_qkv, swa_sinks, swa_w_out, s5_w_in, s5_a_re, s5_a_im, s5_log_dt, s5_b_re, s5_b_im, s5_c_re, s5_c_im, s5_d, s5_w_glu, hg_w_in, hg_lb, hg_norm_w, hg_w_out, ln1_g, ln1_b, ln2_g, ln2_b, moe_w_rg, moe_b_rg, moe_w_re, moe_b_re, moe_w_gate, moe_w_up, moe_w_down):
    states = (state_ssd_conv, state_ssd, cache_swa_k, cache_swa_v, state_s5_re, state_s5_im, state_hgrn)
    ssd_p = (ssd_w_in, ssd_conv_w, ssd_conv_b, ssd_dt_bias, ssd_a_log, ssd_d, ssd_norm_w, ssd_w_out)
    swa_p = (swa_w_qkv, swa_sinks, swa_w_out)
    s5_p = (s5_w_in, s5_a_re, s5_a_im, s5_log_dt, s5_b_re, s5_b_im, s5_c_re, s5_c_im, s5_d, s5_w_glu)
    hg_p = (hg_w_in, hg_lb, hg_norm_w, hg_w_out)
    ln_p = (ln1_g, ln1_b, ln2_g, ln2_b)
    moe_p = (moe_w_rg, moe_b_rg, moe_w_re, moe_b_re, moe_w_gate, moe_w_up, moe_w_down)
    return _forward(x_prompt, x_sample, states, ssd_p, swa_p, s5_p, hg_p, ln_p, moe_p)
```

```python
import functools

import jax
import jax.numpy as jnp
from jax import lax
from jax.experimental import pallas as pl
from jax.experimental.pallas import tpu as pltpu

F32 = jnp.float32
BF16 = jnp.bfloat16
HIGHEST = lax.Precision.HIGHEST

D = 2048
DEPTH = 4
DN_ALPHA = (2 * DEPTH) ** 0.25
LN_EPS = 1e-5
RMS_EPS = 1e-6
NEG = -1e30

VMEM_LIMIT = 56 * 1024 * 1024
MM_TN = 1024

SSD_INNER = 4096
SSD_HEADS = 64
SSD_G = 8
SSD_HPG = 8
SSD_N = 128
SSD_GW = SSD_INNER // SSD_G
SSD_CONV = 4
SSD_XBC = SSD_INNER + 2 * SSD_G * SSD_N

SWA_DH = 64
SWA_QH = 32
SWA_KVH = 4
SWA_GRP = SWA_QH // SWA_KVH
WINDOW = 128
CHUNK = 64

S5_G = 128
S5_CH = 16
S5_N = 64
S5_SUB = 16
S5_GB = 8
S5_CW = S5_GB * S5_CH
S5_KR = 256

HG_H = 16
HG_K = 128
HG_LEAF = 16

MOE_G = 4
MOE_PG = 8
MOE_E = 32
D_EXPERT = 256
MOE_TM = 256


def _cparams(sem):
    return pltpu.CompilerParams(dimension_semantics=sem, vmem_limit_bytes=VMEM_LIMIT)


def _sigmoid(x):
    return 1.0 / (1.0 + jnp.exp(-x))


def _silu(x):
    return x * _sigmoid(x)


def _softplus(x):
    return jnp.maximum(x, 0.0) + jnp.log(1.0 + jnp.exp(-jnp.abs(x)))


def _dot(a, b, precision=None):
    return jnp.dot(a, b, preferred_element_type=F32, precision=precision)


def _dot_nt(a, b):
    return lax.dot_general(a, b, (((1,), (1,)), ((), ())), preferred_element_type=F32)


def _tri(n, upper=False):
    r = lax.broadcasted_iota(jnp.int32, (n, n), 0)
    c = lax.broadcasted_iota(jnp.int32, (n, n), 1)
    return (r <= c) if upper else (c <= r)


def _transpose_rows(x):
    c = x.shape[0]
    if c == 128:
        return x.T
    pad = jnp.zeros((128 - c, 128), x.dtype)
    return jnp.concatenate([x, pad], axis=0).T[:, :c]


def _pack_halves(x):
    n = x.shape[1] // 2
    lo = lax.bitcast_convert_type(x[:, :n].astype(BF16).astype(F32), jnp.uint32)
    hi = lax.bitcast_convert_type(x[:, n:].astype(BF16).astype(F32), jnp.uint32)
    return (lo >> 16) | hi


def _unpack_halves(w):
    lo = lax.bitcast_convert_type(w << 16, F32)
    hi = lax.bitcast_convert_type(w & jnp.uint32(0xFFFF0000), F32)
    return lo, hi


def _mm_body(x_ref, w_ref, o_ref, wb_ref, *, tiled_out):
    @pl.when(pl.program_id(1) == 0)
    def _():
        wb_ref[...] = w_ref[...].astype(BF16)

    r = _dot(x_ref[...], wb_ref[...]).astype(o_ref.dtype)
    if tiled_out:
        sub = o_ref.shape[2]
        for q in range(o_ref.shape[0]):
            o_ref[q] = r[:, sub * q:sub * (q + 1)]
    else:
        o_ref[...] = r


def matmul(x, w, *, col0=0, ncols=None, tn=None, tm=512, out_dtype=F32, tiled_out=False, sub=None,
           name="matmul"):
    m, k = x.shape
    ncols = w.shape[1] - col0 if ncols is None else ncols
    tm = min(tm, m)
    if tn is None:
        tn = MM_TN if k * MM_TN * 4 <= 8 * 1024 * 1024 else MM_TN // 2
        tn = min(tn, ncols)
    assert col0 % tn == 0 and ncols % tn == 0 and m % tm == 0
    nj = ncols // tn
    j0 = col0 // tn
    if tiled_out:
        sub = tn if sub is None else sub
        out_shape = jax.ShapeDtypeStruct((ncols // sub, m, sub), out_dtype)
        out_spec = pl.BlockSpec((tn // sub, tm, sub), lambda j, i: (j, i, 0))
    else:
        out_shape = jax.ShapeDtypeStruct((m, ncols), out_dtype)
        out_spec = pl.BlockSpec((tm, tn), lambda j, i: (i, j))
    return pl.pallas_call(
        functools.partial(_mm_body, tiled_out=tiled_out),
        grid=(nj, m // tm),
        in_specs=[pl.BlockSpec((tm, k), lambda j, i: (i, 0)),
                  pl.BlockSpec((k, tn), lambda j, i: (0, j + j0))],
        out_specs=out_spec,
        out_shape=out_shape,
        scratch_shapes=[pltpu.VMEM((k, tn), BF16)],
        compiler_params=_cparams(("arbitrary", "arbitrary")),
        name=name,
    )(x, w)


def _glu_body(x_ref, wv_ref, wg_ref, o_ref, wvb_ref, wgb_ref):
    @pl.when(pl.program_id(1) == 0)
    def _():
        wvb_ref[...] = wv_ref[...].astype(BF16)
        wgb_ref[...] = wg_ref[...].astype(BF16)

    x = x_ref[...].astype(BF16)
    o_ref[...] = _dot(x, wvb_ref[...]) * _sigmoid(_dot(x, wgb_ref[...]))


def glu_matmul(x, w, *, tn=512, tm=512):
    m, k = x.shape
    n = w.shape[1] // 2
    tm = min(tm, m)
    nj = n // tn
    return pl.pallas_call(
        _glu_body,
        grid=(nj, m // tm),
        in_specs=[pl.BlockSpec((tm, k), lambda j, i: (i, 0)),
                  pl.BlockSpec((k, tn), lambda j, i: (0, j)),
                  pl.BlockSpec((k, tn), lambda j, i: (0, j + nj))],
        out_specs=pl.BlockSpec((tm, tn), lambda j, i: (i, j)),
        out_shape=jax.ShapeDtypeStruct((m, n), F32),
        scratch_shapes=[pltpu.VMEM((k, tn), BF16), pltpu.VMEM((k, tn), BF16)],
        compiler_params=_cparams(("arbitrary", "arbitrary")),
        name="glu_matmul",
    )(x, w, w)


def _layer_norm(v, g, b):
    mu = jnp.mean(v, axis=-1, keepdims=True)
    vc = v - mu
    var = jnp.mean(vc * vc, axis=-1, keepdims=True)
    return vc * lax.rsqrt(var + LN_EPS) * g + b


def _route(x, wr, br, carry):
    lg = _dot(x, wr, precision=HIGHEST) + br
    tm = lg.shape[0]
    lane = lax.broadcasted_iota(jnp.int32, lg.shape, 1)
    lanef = lane.astype(F32)
    big = jnp.float32(1e9)
    is_g = lane < MOE_G
    gl = jnp.where(is_g, lg, NEG)
    gmax = jnp.max(gl, axis=-1, keepdims=True)
    gsel = jnp.min(jnp.where(is_g & (gl == gmax), lanef, big), axis=-1, keepdims=True)
    gprob = 1.0 / jnp.sum(jnp.where(is_g, jnp.exp(gl - gmax), 0.0), axis=-1, keepdims=True)
    lo = MOE_G + MOE_PG * gsel
    is_e = (lanef >= lo) & (lanef < lo + MOE_PG)
    el = jnp.where(is_e, lg, NEG)
    m1 = jnp.max(el, axis=-1, keepdims=True)
    l1 = jnp.min(jnp.where(is_e & (el == m1), lanef, big), axis=-1, keepdims=True)
    is_e2 = is_e & (lanef != l1)
    el2 = jnp.where(is_e2, lg, NEG)
    m2 = jnp.max(el2, axis=-1, keepdims=True)
    l2 = jnp.min(jnp.where(is_e2 & (el2 == m2), lanef, big), axis=-1, keepdims=True)
    r = jnp.exp(m2 - m1)
    w1 = gprob / (1.0 + r)
    w2 = gprob * r / (1.0 + r)
    hit1 = lanef == l1
    hit2 = lanef == l2
    oh = jnp.where(hit1 | hit2, 1.0, 0.0)
    rr = lax.broadcasted_iota(jnp.int32, (tm, tm), 0)
    cc = lax.broadcasted_iota(jnp.int32, (tm, tm), 1)
    before = _dot(jnp.where(cc < rr, 1.0, 0.0).astype(BF16), oh.astype(BF16)) + carry
    k1 = jnp.sum(jnp.where(hit1, before, 0.0), axis=-1, keepdims=True)
    k2 = jnp.sum(jnp.where(hit2, before, 0.0), axis=-1, keepdims=True)
    table = jnp.where(lane == 0, l1 - MOE_G,
                      jnp.where(lane == 1, l2 - MOE_G,
                                jnp.where(lane == 2, w1,
                                          jnp.where(lane == 3, w2,
                                                    jnp.where(lane == 4, k1, jnp.where(lane == 5, k2, 0.0))))))
    return table, carry + jnp.sum(oh, axis=0, keepdims=True)


def _ln_route_body(x_ref, a_ref, w_ref, g_ref, b_ref, wr_ref, br_ref, o_ref, op_ref, r_ref, rt_ref, cnt_ref,
                   *, glu):
    @pl.when(pl.program_id(0) == 0)
    def _():
        cnt_ref[...] = jnp.zeros_like(cnt_ref)

    h = _dot(a_ref[...].astype(BF16), w_ref[...])
    if glu:
        h = h[:, :D] * _sigmoid(h[:, D:])
    y = _layer_norm(DN_ALPHA * x_ref[...] + h, g_ref[...], b_ref[...])
    o_ref[...] = y
    op_ref[...] = _pack_halves(y)
    table, cnt_ref[...] = _route(y, wr_ref[...], br_ref[...], cnt_ref[...])
    r_ref[...] = table
    rt_ref[...] = jnp.concatenate([table[128 * q:128 * (q + 1)].T[:8] for q in range(table.shape[0] // 128)],
                                  axis=1)


def proj_ln_route(x, a, w, g, b, wr, br, *, glu=False, tm=256):
    m = x.shape[0]
    k, n = w.shape
    row = pl.BlockSpec((tm, D), lambda i: (i, 0))
    vec = pl.BlockSpec((1, D), lambda i: (0, 0))
    one = pl.BlockSpec((1, 128), lambda i: (0, 0))
    resident = pl.Buffered(1)
    return pl.pallas_call(
        functools.partial(_ln_route_body, glu=glu),
        grid=(m // tm,),
        in_specs=[row, pl.BlockSpec((tm, k), lambda i: (i, 0)),
                  pl.BlockSpec((k, n), lambda i: (0, 0), pipeline_mode=resident),
                  vec, vec, pl.BlockSpec((D, 128), lambda i: (0, 0), pipeline_mode=resident), one],
        out_specs=[row, pl.BlockSpec((tm, D // 2), lambda i: (i, 0)), pl.BlockSpec((tm, 128), lambda i: (i, 0)),
                   pl.BlockSpec((8, tm), lambda i: (0, i)), one],
        out_shape=[jax.ShapeDtypeStruct((m, D), F32), jax.ShapeDtypeStruct((m, D // 2), jnp.uint32),
                   jax.ShapeDtypeStruct((m, 128), F32), jax.ShapeDtypeStruct((8, m), F32),
                   jax.ShapeDtypeStruct((1, 128), F32)],
        compiler_params=_cparams(("arbitrary",)),
        name="proj_ln_route",
    )(x, a, w, g.reshape(1, D), b.reshape(1, D), wr, br)


def _ln_combine_body(x_ref, y0_ref, y1_ref, r_ref, g_ref, b_ref, o_ref, ob_ref, *, split_tiles):
    r = r_ref[...]
    a_lo, a_hi = _unpack_halves(y0_ref[...])
    b_lo, b_hi = _unpack_halves(y1_ref[...])
    w0, w1 = r[:, 2:3], r[:, 3:4]
    f = jnp.concatenate([w0 * a_lo + w1 * b_lo, w0 * a_hi + w1 * b_hi], axis=1)
    y = _layer_norm(DN_ALPHA * x_ref[...] + f, g_ref[...], b_ref[...])
    if split_tiles is None:
        o_ref[...] = y
        ob_ref[...] = y.astype(BF16)
    else:
        @pl.when(pl.program_id(0) < split_tiles)
        def _():
            o_ref[...] = y

        @pl.when(pl.program_id(0) >= split_tiles)
        def _():
            ob_ref[...] = y


def ln_combine(x, y0, y1, route, g, b, tm=256, split=None):
    m = x.shape[0]
    row = pl.BlockSpec((tm, D), lambda i: (i, 0))
    half = pl.BlockSpec((tm, D // 2), lambda i: (i, 0))
    vec = pl.BlockSpec((1, D), lambda i: (0, 0))
    if split is None:
        st = None
        out_specs = [row, row]
        out_shape = [jax.ShapeDtypeStruct((m, D), F32), jax.ShapeDtypeStruct((m, D), BF16)]
    else:
        st = split // tm
        out_specs = [pl.BlockSpec((tm, D), lambda i: (jnp.minimum(i, st - 1), 0)),
                     pl.BlockSpec((tm, D), lambda i: (jnp.maximum(i - st, 0), 0))]
        out_shape = [jax.ShapeDtypeStruct((split, D), F32), jax.ShapeDtypeStruct((m - split, D), F32)]
    return pl.pallas_call(
        functools.partial(_ln_combine_body, split_tiles=st),
        grid=(m // tm,),
        in_specs=[row, half, half, pl.BlockSpec((tm, 128), lambda i: (i, 0)), vec, vec],
        out_specs=out_specs,
        out_shape=out_shape,
        compiler_params=_cparams(("arbitrary",)),
        name="ln_combine",
    )(x, y0, y1, route, g.reshape(1, D), b.reshape(1, D))


def _ssd_body(z_ref, x_ref, b_ref, c_ref, dtg_ref, dtt_ref, csx_ref, csb_ref, csc_ref, h0_ref,
              cwx_ref, cwb_ref, cwc_ref, cbx_ref, cbb_ref, cbc_ref, dtbr_ref, dtbc_ref,
              alr_ref, alc_ref, dsk_ref, nw_ref, *rest, C, aliased):
    if aliased:
        rest = rest[1:]
    (y_ref, cox_ref, cob_ref, coc_ref, ho_ref,
     tx_ref, tb_ref, tc_ref, ex_ref, eb_ref, ec_ref, ht_ref, yb_ref) = rest
    c = pl.program_id(1)
    last_chunk = c == pl.num_programs(1) - 1

    @pl.when(c == 0)
    def _init():
        tx_ref[...] = jnp.zeros_like(tx_ref)
        tb_ref[...] = jnp.zeros_like(tb_ref)
        tc_ref[...] = jnp.zeros_like(tc_ref)
        for g in range(SSD_G):
            tx_ref[g, 5:8, :] = csx_ref[0, g]
            tb_ref[g, 5:8, :] = csb_ref[0, g]
            tc_ref[g, 5:8, :] = csc_ref[0, g]
        for p in range(SSD_HEADS // 2):
            ht_ref[p] = h0_ref[0, p].T

    tril = _tri(C)
    tril_f = tril.astype(F32)
    triu_f = _tri(C, upper=True).astype(F32)
    lane = lax.broadcasted_iota(jnp.int32, (C, 128), 1)
    left = lane < 64

    def conv(e_ref, t_ref, raw, w_ref, bias_ref, g):
        e_ref[0:8, :] = t_ref[g]
        e_ref[8:8 + C, :] = raw
        w = w_ref[g]
        acc = bias_ref[g] + w[3:4, :] * raw
        for k in range(SSD_CONV - 1):
            acc = acc + w[k:k + 1, :] * e_ref[5 + k:5 + k + C, :]
        t_ref[g] = e_ref[C:C + 8, :]
        return _silu(acc)

    def group(g, carry):
        xs = conv(ex_ref, tx_ref, x_ref[g], cwx_ref, cbx_ref, g)
        bs = conv(eb_ref, tb_ref, b_ref[g], cwb_ref, cbb_ref, g)
        cs = conv(ec_ref, tc_ref, c_ref[g], cwc_ref, cbc_ref, g)

        @pl.when(last_chunk)
        def _():
            cox_ref[0, g] = ex_ref[C + 5:C + 8, :]
            cob_ref[0, g] = eb_ref[C + 5:C + 8, :]
            coc_ref[0, g] = ec_ref[C + 5:C + 8, :]

        dtv = _softplus(dtg_ref[0, g] + dtbr_ref[g])
        dtvt = _softplus(dtt_ref[0, g] + dtbc_ref[g])
        cum = _dot(tril_f, dtv * (-jnp.exp(alr_ref[g])), precision=HIGHEST)
        cumt = _dot(dtvt * (-jnp.exp(alc_ref[g])), triu_f, precision=HIGHEST)
        bsb = bs.astype(BF16)
        csb = cs.astype(BF16)
        cb = _dot_nt(csb, bsb)
        bst = _transpose_rows(bs).astype(BF16)
        dsk = dsk_ref[g]
        ys = []
        for j in range(SSD_HPG // 2):
            h0, h1 = 2 * j, 2 * j + 1
            c0, c1 = cum[:, h0:h0 + 1], cum[:, h1:h1 + 1]
            l0 = jnp.where(tril, jnp.exp(c0 - cumt[h0:h0 + 1, :]), 0.0) * cb
            l1 = jnp.where(tril, jnp.exp(c1 - cumt[h1:h1 + 1, :]), 0.0) * cb
            lhs = jnp.concatenate([l0, l1], axis=1).astype(BF16)
            xp = xs[:, 128 * j:128 * (j + 1)]
            xdt = xp * jnp.where(left, dtv[:, h0:h0 + 1], dtv[:, h1:h1 + 1])
            rhs = jnp.concatenate([jnp.where(left, xdt, 0.0), jnp.where(left, 0.0, xdt)],
                                  axis=0).astype(BF16)
            htp = ht_ref[g * 4 + j]
            y = _dot(lhs, rhs)
            y = y + _dot(csb, htp.astype(BF16)) * jnp.where(left, jnp.exp(c0), jnp.exp(c1))
            e0, e1 = cum[C - 1:C, h0:h0 + 1], cum[C - 1:C, h1:h1 + 1]
            wgt = (xdt * jnp.where(left, jnp.exp(e0 - c0), jnp.exp(e1 - c1))).astype(BF16)
            ht_ref[g * 4 + j] = jnp.where(left[0:1, :], jnp.exp(e0), jnp.exp(e1)) * htp + _dot(bst, wgt)
            ys.append(y + dsk[:, 128 * j:128 * (j + 1)] * xp)
        y = jnp.concatenate(ys, axis=1) * _silu(z_ref[g])
        ms = jnp.mean(y * y, axis=-1, keepdims=True)
        yb_ref[g] = (y * lax.rsqrt(ms + RMS_EPS) * nw_ref[g]).astype(BF16)
        return carry

    lax.fori_loop(0, SSD_G, group, 0)
    for g in range(SSD_G):
        y_ref[:, SSD_GW * g:SSD_GW * (g + 1)] = yb_ref[g]

    @pl.when(last_chunk)
    def _fin():
        for p in range(SSD_HEADS // 2):
            ho_ref[0, p] = ht_ref[p].T


def ssd_core(zx, bc, dt, conv_state, h0, params, *, row0, streams, length, chunk, y_prev=None):
    conv_w, conv_b, dt_bias, a_log, d_skip, norm_w = params
    S, L, C = streams, length, chunk
    nch = L // C
    rb0 = row0 // C
    t_all = zx.shape[1]
    dseg = dt[row0:row0 + S * L].reshape(S, L, SSD_G, SSD_HPG)
    dtg = dseg.transpose(0, 2, 1, 3)
    dtt = dseg.transpose(0, 2, 3, 1)

    def split(a, lead):
        ax = a[..., :SSD_INNER].reshape(lead + (SSD_G, SSD_GW))
        ab = a[..., SSD_INNER:SSD_INNER + SSD_G * SSD_N].reshape(lead + (SSD_G, SSD_N))
        ac = a[..., SSD_INNER + SSD_G * SSD_N:].reshape(lead + (SSD_G, SSD_N))
        return ax, ab, ac

    csx, csb, csc = (jnp.moveaxis(a, 2, 1) for a in split(conv_state, (S, SSD_CONV - 1)))
    cwx, cwb, cwc = (jnp.moveaxis(a, 1, 0) for a in split(conv_w, (SSD_CONV,)))
    cbx, cbb, cbc = (jnp.moveaxis(a, 1, 0) for a in split(conv_b.reshape(1, -1), (1,)))
    dtbr = dt_bias.reshape(SSD_G, 1, SSD_HPG)
    dtbc = dt_bias.reshape(SSD_G, SSD_HPG, 1)
    alr = a_log.reshape(SSD_G, 1, SSD_HPG)
    alc = a_log.reshape(SSD_G, SSD_HPG, 1)
    dsk = jnp.repeat(d_skip, SSD_INNER // SSD_HEADS).reshape(SSD_G, 1, SSD_GW)
    nw = norm_w.reshape(SSD_G, 1, SSD_GW)
    h0p = h0.reshape(S, SSD_HEADS // 2, 128, SSD_N)

    def rb(s, c):
        return rb0 + s * nch + c

    def full(a):
        nd = a.ndim
        return pl.BlockSpec(a.shape, lambda s, c: (0,) * nd)

    def per_stream(a):
        nd = a.ndim
        return pl.BlockSpec((1,) + a.shape[1:], lambda s, c: (s,) + (0,) * (nd - 1))

    in_specs = [
        pl.BlockSpec((SSD_G, C, SSD_GW), lambda s, c: (0, rb(s, c), 0)),
        pl.BlockSpec((SSD_G, C, SSD_GW), lambda s, c: (1, rb(s, c), 0)),
        pl.BlockSpec((SSD_G, C, SSD_N), lambda s, c: (0, rb(s, c), 0)),
        pl.BlockSpec((SSD_G, C, SSD_N), lambda s, c: (1, rb(s, c), 0)),
        pl.BlockSpec((1, SSD_G, C, SSD_HPG), lambda s, c: (s, 0, c, 0)),
        pl.BlockSpec((1, SSD_G, SSD_HPG, C), lambda s, c: (s, 0, 0, c)),
        per_stream(csx), per_stream(csb), per_stream(csc), per_stream(h0p),
        full(cwx), full(cwb), full(cwc), full(cbx), full(cbb), full(cbc),
        full(dtbr), full(dtbc), full(alr), full(alc), full(dsk), full(nw),
    ]
    args = [zx, zx, bc, bc, dtg, dtt, csx, csb, csc, h0p, cwx, cwb, cwc, cbx, cbb, cbc,
            dtbr, dtbc, alr, alc, dsk, nw]
    aliases = {}
    if y_prev is not None:
        in_specs.append(pl.BlockSpec(memory_space=pl.ANY))
        args.append(y_prev)
        aliases = {len(args) - 1: 0}
    out_shape = [
        jax.ShapeDtypeStruct((t_all, SSD_INNER), BF16),
        jax.ShapeDtypeStruct(csx.shape, F32), jax.ShapeDtypeStruct(csb.shape, F32),
        jax.ShapeDtypeStruct(csc.shape, F32), jax.ShapeDtypeStruct(h0p.shape, F32),
    ]
    out_specs = [
        pl.BlockSpec((C, SSD_INNER), lambda s, c: (rb(s, c), 0)),
        per_stream(csx), per_stream(csb), per_stream(csc), per_stream(h0p),
    ]
    scratch = [
        pltpu.VMEM((SSD_G, 8, SSD_GW), F32), pltpu.VMEM((SSD_G, 8, SSD_N), F32),
        pltpu.VMEM((SSD_G, 8, SSD_N), F32),
        pltpu.VMEM((C + 8, SSD_GW), F32), pltpu.VMEM((C + 8, SSD_N), F32), pltpu.VMEM((C + 8, SSD_N), F32),
        pltpu.VMEM((SSD_HEADS // 2, SSD_N, 128), F32),
        pltpu.VMEM((SSD_G, C, SSD_GW), BF16),
    ]
    y, cox, cob, coc, ho = pl.pallas_call(
        functools.partial(_ssd_body, C=C, aliased=y_prev is not None),
        grid=(S, nch), in_specs=in_specs, out_specs=out_specs, out_shape=out_shape,
        scratch_shapes=scratch, input_output_aliases=aliases,
        compiler_params=_cparams(("arbitrary", "arbitrary")),
        name="ssd_core",
    )(*args)
    conv_out = jnp.concatenate([jnp.moveaxis(a, 1, 2).reshape(S, SSD_CONV - 1, -1) for a in (cox, cob, coc)],
                               axis=-1)
    return y, conv_out, ho.reshape(S, SSD_HEADS, SSD_INNER // SSD_HEADS, SSD_N)


def _swa_body(sink_ref, q_ref, pk_ref, pv_ref, kv_ref, *rest, QT, prev_valid, aliased):
    o_ref = rest[-1]
    i = pl.program_id(0)
    kvw = SWA_KVH * SWA_DH
    nback = WINDOW // CHUNK
    rb = min(QT, 2 * CHUNK)
    span = rb + WINDOW
    kf = jnp.concatenate([pk_ref[...], kv_ref[:, :kvw]], axis=0).astype(BF16)
    vf = jnp.concatenate([pv_ref[...], kv_ref[:, kvw:]], axis=0).astype(BF16)
    r2 = lax.broadcasted_iota(jnp.int32, (128, 128), 0)
    c2 = lax.broadcasted_iota(jnp.int32, (128, 128), 1)
    swap = jnp.where((r2 + SWA_DH) % 128 == c2, 1.0, 0.0).astype(BF16)
    left = lax.broadcasted_iota(jnp.int32, (1, 128), 1) < SWA_DH
    k_side, v_side = [], []
    for kh in range(SWA_KVH):
        blk = slice(128 * (kh // 2), 128 * (kh // 2 + 1))
        mine = left if kh % 2 == 0 else ~left
        kb, vb = kf[:, blk], jnp.where(mine, vf[:, blk], 0.0).astype(BF16)
        ko, vo = _dot(kb, swap).astype(BF16), _dot(vb, swap).astype(BF16)
        k_side.append((kb, ko) if kh % 2 == 0 else (ko, kb))
        v_side.append((vb, vo) if kh % 2 == 0 else (vo, vb))
    scale = SWA_DH ** -0.5
    for b in range(QT // rb):
        rows = slice(rb * b, rb * (b + 1))
        keys = slice(rb * b, rb * b + span)
        qc = lax.broadcasted_iota(jnp.int32, (rb, span), 0) // CHUNK
        kc = lax.broadcasted_iota(jnp.int32, (rb, span), 1) // CHUNK
        ok = (kc >= qc) & (kc <= qc + nback)
        if not prev_valid and rb * b < WINDOW:
            ok = ok & ((kc + (rb // CHUNK) * b >= nback) | (i > 0))
        for pr in range(SWA_QH // 2):
            kh = 2 * pr // SWA_GRP
            q2 = q_ref[rows, 128 * pr:128 * (pr + 1)] * scale
            acc = None
            for side in range(2):
                qm = jnp.where(left if side == 0 else ~left, q2, 0.0).astype(BF16)
                s = jnp.where(ok, _dot_nt(qm, k_side[kh][side][keys]), NEG)
                sink = sink_ref[2 * pr + side]
                m = jnp.maximum(jnp.max(s, axis=-1, keepdims=True), sink)
                p = jnp.exp(s - m)
                den = jnp.sum(p, axis=-1, keepdims=True) + jnp.exp(sink - m)
                o = _dot(p.astype(BF16), v_side[kh][side][keys]) / den
                acc = o if acc is None else acc + o
            o_ref[rows, 128 * pr:128 * (pr + 1)] = acc.astype(BF16)


def swa_core(q, kv, prev_k, prev_v, sinks, *, row0, tiles, qt, prompt, o_prev=None):
    t_all = q.shape[0]
    rb0 = row0 // qt
    kvw = SWA_KVH * SWA_DH
    if prompt:
        wpt = qt // WINDOW
        prev_map_k = lambda i, s: (jnp.maximum(wpt * (rb0 + i) - 1, 0), 0)
        prev_map_v = lambda i, s: (jnp.maximum(wpt * (rb0 + i) - 1, 0), 1)
        pk_spec = pl.BlockSpec((WINDOW, kvw), prev_map_k)
        pv_spec = pl.BlockSpec((WINDOW, kvw), prev_map_v)
        prev_k = prev_v = kv
    else:
        pk_spec = pl.BlockSpec((None, WINDOW, kvw), lambda i, s: (i, 0, 0))
        pv_spec = pl.BlockSpec((None, WINDOW, kvw), lambda i, s: (i, 0, 0))
    in_specs = [pl.BlockSpec((qt, D), lambda i, s: (rb0 + i, 0)), pk_spec, pv_spec,
                pl.BlockSpec((qt, 2 * kvw), lambda i, s: (rb0 + i, 0))]
    args = [sinks, q, prev_k, prev_v, kv]
    aliases = {}
    if o_prev is not None:
        in_specs.append(pl.BlockSpec(memory_space=pl.ANY))
        args.append(o_prev)
        aliases = {len(args) - 1: 0}
    return pl.pallas_call(
        functools.partial(_swa_body, QT=qt, prev_valid=not prompt, aliased=o_prev is not None),
        grid_spec=pltpu.PrefetchScalarGridSpec(
            num_scalar_prefetch=1, grid=(tiles,), in_specs=in_specs,
            out_specs=pl.BlockSpec((qt, D), lambda i, s: (rb0 + i, 0))),
        out_shape=jax.ShapeDtypeStruct((t_all, D), BF16),
        input_output_aliases=aliases,
        compiler_params=_cparams(("arbitrary",)),
        name="swa_core",
    )(*args)


def _gelu_tanh(y):
    return 0.5 * y * (1.0 + jnp.tanh(0.7978845608028654 * (y + 0.044715 * y * y * y)))


def _s5_body(u_ref, wxr_ref, wxi_ref, wyr_ref, wyi_ref, kt_ref, a_ref, dsk_ref, hr0_ref, hi0_ref, *rest,
             S, R, aliased):
    if aliased:
        rest = rest[1:]
    (y_ref, hro_ref, hio_ref, wx_s, wy_s, ktm_s, xr_s, xi_s, pr_s, pi_s, hr_s, hi_s) = rest
    kb = pl.program_id(1)
    half = S5_CW * S5_N // S5_CH

    @pl.when((pl.program_id(0) == 0) & (kb == 0))
    def _zero():
        ktm_s[...] = jnp.zeros_like(ktm_s)

    @pl.when(kb == 0)
    def _build():
        own = (lax.broadcasted_iota(jnp.int32, (S5_CW, half), 0) // S5_CH
               == lax.broadcasted_iota(jnp.int32, (S5_CW, half), 1) // S5_N)
        for s in range(S5_SUB):
            rows = slice(S5_CW * s, S5_CW * (s + 1))
            wx_s[rows, 0:half] = jnp.where(own, jnp.concatenate([wxr_ref[s]] * 4, axis=1), 0.0).astype(BF16)
            wx_s[rows, half:2 * half] = jnp.where(own, jnp.concatenate([wxi_ref[s]] * 4, axis=1), 0.0).astype(BF16)
        own_t = (lax.broadcasted_iota(jnp.int32, (half, S5_CW), 0) // S5_N
                 == lax.broadcasted_iota(jnp.int32, (half, S5_CW), 1) // S5_CH)
        for t in range(S5_SUB):
            cols = slice(S5_CW * t, S5_CW * (t + 1))
            wy_s[0:half, cols] = jnp.where(own_t, jnp.concatenate([wyr_ref[t]] * 8, axis=0), 0.0).astype(BF16)
            wy_s[half:2 * half, cols] = jnp.where(own_t, jnp.concatenate([wyi_ref[t]] * 8, axis=0), 0.0).astype(BF16)
        same = (lax.broadcasted_iota(jnp.int32, (S5_CW, S5_CW), 0) // S5_CH
                == lax.broadcasted_iota(jnp.int32, (S5_CW, S5_CW), 1) // S5_CH)
        taps = [jnp.where(same, kt_ref[tau], 0.0).astype(BF16) for tau in range(S5_SUB)]
        for s in range(S5_SUB):
            for t in range(s, S5_SUB):
                ktm_s[S5_CW * s:S5_CW * (s + 1), S5_CW * t:S5_CW * (t + 1)] = taps[t - s]
        for st in range(S):
            hr_s[st] = hr0_ref[st]
            hi_s[st] = hi0_ref[st]

    kr = S * R
    ucat = jnp.concatenate([u_ref[pl.ds(s, kr, stride=S5_SUB), :] for s in range(S5_SUB)],
                           axis=1)
    ub = ucat.astype(BF16)
    x = _dot(ub, wx_s[...])
    xr_s[...] = x[:, :half]
    xi_s[...] = x[:, half:]
    ar = a_ref[0:1, :]
    ai = a_ref[1:2, :]
    for st in range(S):
        def step(k, carry):
            hr, hi = carry
            row = st * R + k
            pr_s[pl.ds(row, 1), :] = hr
            pi_s[pl.ds(row, 1), :] = hi
            nr = ar * hr - ai * hi + xr_s[pl.ds(row, 1), :]
            ni = ar * hi + ai * hr + xi_s[pl.ds(row, 1), :]
            return nr, ni

        hr, hi = lax.fori_loop(0, R, step, (hr_s[st], hi_s[st]))
        hr_s[st] = hr
        hi_s[st] = hi
    hprev = jnp.concatenate([pr_s[...], pi_s[...]], axis=1).astype(BF16)
    dsk = jnp.concatenate([dsk_ref[...]] * S5_SUB, axis=1)
    cb = 2 * S5_CW
    intra = jnp.concatenate([_dot(ub[:, :cb * (t + 1)], ktm_s[0:cb * (t + 1), cb * t:cb * (t + 1)])
                             for t in range(S5_SUB // 2)], axis=1)
    y = _gelu_tanh(intra + _dot(hprev, wy_s[...]) + dsk * ucat)
    for t in range(S5_SUB):
        y_ref[pl.ds(t, kr, stride=S5_SUB), :] = y[:, S5_CW * t:S5_CW * (t + 1)]

    @pl.when(kb == pl.num_programs(1) - 1)
    def _fin():
        for st in range(S):
            hro_ref[st] = hr_s[st]
            hio_ref[st] = hi_s[st]


def s5_tables(p):
    a_re, a_im, log_dt, b_re, b_im, c_re, c_im, d_skip = p
    lr, li = a_re.astype(F32), a_im.astype(F32)
    dt = jnp.exp(log_dt.astype(F32))[:, None]
    mag = jnp.exp(lr * dt)
    ab_r, ab_i = mag * jnp.cos(li * dt), mag * jnp.sin(li * dt)
    den = lr * lr + li * li
    co_r = ((ab_r - 1.0) * lr + ab_i * li) / den
    co_i = (ab_i * lr - (ab_r - 1.0) * li) / den
    bb_r = co_r[..., None] * b_re - co_i[..., None] * b_im
    bb_i = co_r[..., None] * b_im + co_i[..., None] * b_re
    pw_r, pw_i = [jnp.ones_like(ab_r)], [jnp.zeros_like(ab_i)]
    for _ in range(S5_SUB):
        r, i = pw_r[-1], pw_i[-1]
        pw_r.append(ab_r * r - ab_i * i)
        pw_i.append(ab_r * i + ab_i * r)
    pr, pi = jnp.stack(pw_r, 0), jnp.stack(pw_i, 0)
    er, ei = pr[S5_SUB - 1::-1][:S5_SUB], pi[S5_SUB - 1::-1][:S5_SUB]
    wx_r = er[..., None] * bb_r[None] - ei[..., None] * bb_i[None]
    wx_i = er[..., None] * bb_i[None] + ei[..., None] * bb_r[None]
    gb, gw = S5_G // S5_GB, S5_GB
    wx_r = wx_r.reshape(S5_SUB, gb, gw, S5_N, S5_CH).transpose(1, 0, 2, 4, 3).reshape(gb, S5_SUB, S5_CW, S5_N)
    wx_i = wx_i.reshape(S5_SUB, gb, gw, S5_N, S5_CH).transpose(1, 0, 2, 4, 3).reshape(gb, S5_SUB, S5_CW, S5_N)
    wxr = jnp.concatenate([wx_r, wx_r], axis=-1)
    wxi = jnp.concatenate([wx_i, wx_i], axis=-1)
    qr, qi = pr[1:], pi[1:]
    cr, ci = c_re.astype(F32), c_im.astype(F32)
    wy_r = cr[None] * qr[:, :, None, :] - ci[None] * qi[:, :, None, :]
    wy_i = -(cr[None] * qi[:, :, None, :] + ci[None] * qr[:, :, None, :])
    wyr = wy_r.reshape(S5_SUB, gb, gw, S5_CH, S5_N).transpose(1, 0, 4, 2, 3).reshape(gb, S5_SUB, S5_N, S5_CW)
    wyi = wy_i.reshape(S5_SUB, gb, gw, S5_CH, S5_N).transpose(1, 0, 4, 2, 3).reshape(gb, S5_SUB, S5_N, S5_CW)
    tr = pr[:S5_SUB, :, None, :] * cr[None] - pi[:S5_SUB, :, None, :] * ci[None]
    ti = pr[:S5_SUB, :, None, :] * ci[None] + pi[:S5_SUB, :, None, :] * cr[None]
    taps = jnp.einsum('agjn,gnk->agjk', tr, bb_r) - jnp.einsum('agjn,gnk->agjk', ti, bb_i)
    taps = taps.reshape(S5_SUB, gb, gw, S5_CH, S5_CH).transpose(1, 0, 2, 4, 3)
    kt = jnp.tile(taps.reshape(gb, S5_SUB, S5_CW, S5_CH), (1, 1, 1, gw))
    a16 = jnp.stack([pr[S5_SUB].reshape(gb, gw * S5_N), pi[S5_SUB].reshape(gb, gw * S5_N)], axis=1)
    dsk = d_skip.astype(F32).reshape(gb, 1, S5_CW)
    return wxr, wxi, wyr, wyi, kt, a16, dsk


def s5_core(u, s_re, s_im, tables, *, row0, streams, length, y_prev=None):
    wxr, wxi, wyr, wyi, kt, a16, dsk = tables
    S, L = streams, length
    gb = S5_G // S5_GB
    rows = S * L // S5_SUB
    if S == 1:
        kr = min(S5_KR, rows)
        spb, rps = 1, kr
    else:
        kr = rows
        spb, rps = S, L // S5_SUB
    nkb = rows // kr
    kb0 = row0 // S5_SUB // kr
    half = S5_GB * S5_N
    hr0 = s_re.reshape(S, gb, 1, half)
    hi0 = s_im.reshape(S, gb, 1, half)
    tab = lambda a: pl.BlockSpec((None,) + a.shape[1:], lambda p, k: (p,) + (0,) * (a.ndim - 1))
    st = pl.BlockSpec((S, None, 1, half), lambda p, k: (0, p, 0, 0))
    uspec = pl.BlockSpec((kr * S5_SUB, S5_CW), lambda p, k: (kb0 + k, p))
    in_specs = [uspec, tab(wxr), tab(wxi), tab(wyr), tab(wyi), tab(kt), tab(a16), tab(dsk), st, st]
    args = [u, wxr, wxi, wyr, wyi, kt, a16, dsk, hr0, hi0]
    aliases = {}
    if y_prev is not None:
        in_specs.append(pl.BlockSpec(memory_space=pl.ANY))
        args.append(y_prev)
        aliases = {len(args) - 1: 0}
    wide = S5_SUB * S5_CW
    y, hro, hio = pl.pallas_call(
        functools.partial(_s5_body, S=spb, R=rps, aliased=y_prev is not None),
        grid=(gb, nkb),
        in_specs=in_specs,
        out_specs=[uspec, st, st],
        out_shape=[jax.ShapeDtypeStruct(u.shape, F32), jax.ShapeDtypeStruct(hr0.shape, F32),
                   jax.ShapeDtypeStruct(hi0.shape, F32)],
        scratch_shapes=[pltpu.VMEM((wide, 2 * half), BF16), pltpu.VMEM((2 * half, wide), BF16),
                        pltpu.VMEM((wide, wide), BF16)]
        + [pltpu.VMEM((kr, half), F32) for _ in range(4)]
        + [pltpu.VMEM((spb, 1, half), F32) for _ in range(2)],
        input_output_aliases=aliases,
        compiler_params=_cparams(("arbitrary", "arbitrary")),
        name="s5_core",
    )(*args)
    return y, hro.reshape(S, S5_G, S5_N), hio.reshape(S, S5_G, S5_N)


def _hgrn_body(q_ref, f_ref, i_ref, g_ref, lbp_ref, nw_ref, s0_ref, *rest, C, layer, aliased):
    if aliased:
        rest = rest[1:]
    o_ref, so_ref, st_ref = rest
    c = pl.program_id(1)

    @pl.when(c == 0)
    def _init():
        for h in range(HG_H):
            st_ref[h] = s0_ref[0, h].T

    lbp = lbp_ref[...]
    e = jnp.exp(lbp - jnp.max(lbp, axis=0, keepdims=True))
    lbs = e / jnp.sum(e, axis=0, keepdims=True)
    lb = jnp.zeros((1, D), F32)
    for r in range(1, layer + 1):
        lb = lb + lbs[r:r + 1, :]
    fz = f_ref[...]
    log_sig = jnp.minimum(fz, 0.0) - jnp.log(1.0 + jnp.exp(-jnp.abs(fz)))
    la = jnp.log(lb)
    lbb = jnp.log(1.0 - lb) + log_sig
    mx = jnp.maximum(la, lbb)
    logf = mx + jnp.log(1.0 + jnp.exp(-jnp.abs(la - lbb)))
    kk = 1.0 - jnp.exp(logf)
    qs = _silu(q_ref[...])
    tril = _tri(C)
    cum = _dot(tril.astype(F32), logf, precision=HIGHEST)
    row = lax.broadcasted_iota(jnp.int32, (C, C), 0)
    col = lax.broadcasted_iota(jnp.int32, (C, C), 1)

    levels = []
    b = C
    while b >= HG_LEAF:
        nb = C // b
        ref = jnp.broadcast_to(cum.reshape(nb, b, D)[:, b // 2 - 1:b // 2, :], (nb, b, D)).reshape(C, D)
        ex = cum - ref
        same = (row // b) == (col // b)
        if b == HG_LEAF:
            mask = same & (col <= row)
            qe, ke = jnp.exp(jnp.minimum(ex, 80.0)), jnp.exp(jnp.minimum(-ex, 80.0))
        else:
            mask = same & ((row % b) >= b // 2) & ((col % b) < b // 2)
            e = jnp.exp(-jnp.abs(ex))
            qe, ke = jnp.where(ex <= 0.0, e, 1.0), jnp.where(ex <= 0.0, 1.0, e)
        levels.append((mask, (qs * qe).astype(BF16), (kk * ke).astype(BF16)))
        b //= 2

    last = cum[C - 1:C, :]
    qin = (qs * jnp.exp(cum)).astype(BF16)
    kin = (kk * jnp.exp(last - cum)).astype(BF16)
    dec = jnp.exp(last)
    vv = i_ref[...]
    vb = vv.astype(BF16)
    gate = _silu(g_ref[...])
    nw = nw_ref[...]
    outs = []
    for h in range(HG_H):
        sl = slice(HG_K * h, HG_K * (h + 1))
        att = jnp.zeros((C, C), F32)
        for mask, ql, kl in levels:
            att = att + jnp.where(mask, _dot_nt(ql[:, sl], kl[:, sl]), 0.0)
        st = st_ref[h]
        o = _dot(att.astype(BF16), vb[:, sl]) + _dot_nt(qin[:, sl], st.astype(BF16))
        st_ref[h] = st * dec[:, sl] + _dot(_transpose_rows(vv[:, sl]).astype(BF16), kin[:, sl])
        ms = jnp.mean(o * o, axis=-1, keepdims=True)
        outs.append(o * lax.rsqrt(ms + RMS_EPS) * nw)
    o_ref[...] = (jnp.concatenate(outs, axis=1) * gate).astype(BF16)

    @pl.when(c == pl.num_programs(1) - 1)
    def _fin():
        for h in range(HG_H):
            so_ref[0, h] = st_ref[h].T


def hgrn_core(qfig, lb_param, norm_w, s0, *, layer, row0, streams, length, chunk, o_prev=None):
    S, L, C = streams, length, chunk
    nch = L // C
    rb0 = row0 // C
    t_all = qfig.shape[0]
    rowspec = lambda j: pl.BlockSpec((C, D), lambda s, c: (rb0 + s * nch + c, j))
    in_specs = [rowspec(0), rowspec(1), rowspec(2), rowspec(3),
                pl.BlockSpec((DEPTH, D), lambda s, c: (0, 0)),
                pl.BlockSpec((1, HG_K), lambda s, c: (0, 0)),
                pl.BlockSpec((1, HG_H, HG_K, HG_K), lambda s, c: (s, 0, 0, 0))]
    args = [qfig, qfig, qfig, qfig, lb_param, norm_w.reshape(1, HG_K), s0]
    aliases = {}
    if o_prev is not None:
        in_specs.append(pl.BlockSpec(memory_space=pl.ANY))
        args.append(o_prev)
        aliases = {len(args) - 1: 0}
    return pl.pallas_call(
        functools.partial(_hgrn_body, C=C, layer=layer, aliased=o_prev is not None),
        grid=(S, nch), in_specs=in_specs,
        out_specs=[rowspec(0), pl.BlockSpec((1, HG_H, HG_K, HG_K), lambda s, c: (s, 0, 0, 0))],
        out_shape=[jax.ShapeDtypeStruct((t_all, D), BF16), jax.ShapeDtypeStruct(s0.shape, F32)],
        scratch_shapes=[pltpu.VMEM((HG_H, HG_K, HG_K), F32)],
        input_output_aliases=aliases,
        compiler_params=_cparams(("arbitrary", "arbitrary")),
        name="hgrn_core",
    )(*args)


def _expert_body(te_ref, nt_ref, x_ref, wg_ref, wu_ref, wd_ref, o_ref, wgb_ref, wub_ref, wdb_ref):
    i = pl.program_id(0)
    prev = te_ref[jnp.maximum(i - 1, 0)]

    @pl.when((i == 0) | (te_ref[i] != prev))
    def _():
        wgb_ref[...] = wg_ref[0, 0].astype(BF16)
        wub_ref[...] = wu_ref[0, 0].astype(BF16)
        wdb_ref[...] = wd_ref[0, 0].astype(BF16)

    @pl.when(i < nt_ref[0])
    def _():
        lo, hi = _unpack_halves(x_ref[...])
        lo, hi = lo.astype(BF16), hi.astype(BF16)
        k = D // 2
        gate = _dot(lo, wgb_ref[0:k, :]) + _dot(hi, wgb_ref[k:D, :])
        up = _dot(lo, wub_ref[0:k, :]) + _dot(hi, wub_ref[k:D, :])
        o_ref[...] = _pack_halves(_dot((_silu(gate) * up).astype(BF16), wdb_ref[...]))

    @pl.when(i >= nt_ref[0])
    def _():
        o_ref[...] = jnp.zeros_like(o_ref)


def expert_mlp(xs, tile_expert, n_tiles, w_gate, w_up, w_down, layer):
    p = xs.shape[0]
    nt = p // MOE_TM
    wspec = lambda shp: pl.BlockSpec((1, 1) + shp, lambda i, te, n: (layer, te[i], 0, 0))
    return pl.pallas_call(
        _expert_body,
        grid_spec=pltpu.PrefetchScalarGridSpec(
            num_scalar_prefetch=2, grid=(nt,),
            in_specs=[pl.BlockSpec((MOE_TM, D // 2), lambda i, te, n: (i, 0)),
                      wspec((D, D_EXPERT)), wspec((D, D_EXPERT)), wspec((D_EXPERT, D))],
            out_specs=pl.BlockSpec((MOE_TM, D // 2), lambda i, te, n: (i, 0)),
            scratch_shapes=[pltpu.VMEM((D, D_EXPERT), BF16), pltpu.VMEM((D, D_EXPERT), BF16),
                            pltpu.VMEM((D_EXPERT, D), BF16)]),
        out_shape=jax.ShapeDtypeStruct((p, D // 2), jnp.uint32),
        compiler_params=_cparams(("arbitrary",)),
        name="expert_mlp",
    )(tile_expert, n_tiles, xs, w_gate, w_up, w_down)


def moe_layer(x1, route_t, counts, w_gate, w_up, w_down, layer):
    t = x1.shape[0]
    e0 = route_t[0].astype(jnp.int32)
    e1 = route_t[1].astype(jnp.int32)
    cnt = counts[0, MOE_G:MOE_G + MOE_E].astype(jnp.int32)
    padded = (cnt + MOE_TM - 1) // MOE_TM * MOE_TM
    ends = jnp.cumsum(padded)
    starts = ends - padded
    d0 = starts[e0] + route_t[4].astype(jnp.int32)
    d1 = starts[e1] + route_t[5].astype(jnp.int32)
    p_rows = (2 * t + MOE_E * (MOE_TM - 1)) // 512 * 512 + 512
    nt = p_rows // MOE_TM
    tok = jnp.arange(t, dtype=jnp.int32)
    src = (jnp.arange(p_rows, dtype=jnp.int32) % t).at[jnp.concatenate([d0, d1])].set(
        jnp.concatenate([tok, tok]), mode="promise_in_bounds", unique_indices=True)
    tile_start = jnp.arange(nt, dtype=jnp.int32) * MOE_TM
    tile_expert = jnp.minimum(jnp.sum((tile_start[:, None] >= ends[None, :]).astype(jnp.int32), axis=1),
                              MOE_E - 1).astype(jnp.int32)
    n_tiles = (ends[-1] // MOE_TM).astype(jnp.int32).reshape(1)
    xs = x1.at[src].get(mode="promise_in_bounds")
    ys = expert_mlp(xs, tile_expert, n_tiles, w_gate, w_up, w_down, layer)
    return ys.at[d0].get(mode="promise_in_bounds"), ys.at[d1].get(mode="promise_in_bounds")


def _forward(x_prompt, x_sample, states, ssd_p, swa_p, s5_p, hg_p, ln_p, moe_p):
    (state_ssd_conv, state_ssd, cache_k, cache_v, s5_re, s5_im, state_hgrn) = states
    lp = x_prompt.shape[1]
    sb, ls = x_sample.shape[0], x_sample.shape[1]
    t_all = lp + sb * ls
    ln1_g, ln1_b, ln2_g, ln2_b = ln_p
    w_rg, b_rg, w_re, b_re, w_gate, w_up, w_down = moe_p

    x = jnp.concatenate([x_prompt.reshape(lp, D), x_sample.reshape(sb * ls, D)], axis=0)
    xb = x.astype(BF16)
    outs = {}
    for layer in range(DEPTH):
        kind = layer % 4
        if kind == 0:
            w_in, conv_w, conv_b, dt_bias, a_log, d_skip, norm_w, w_out = ssd_p
            zx = matmul(xb, w_in, col0=0, ncols=2 * SSD_INNER, tn=MM_TN, tiled_out=True, sub=SSD_GW,
                        name="mm_ssd_zx")
            bc = matmul(xb, w_in, col0=2 * SSD_INNER, ncols=2 * SSD_G * SSD_N, tn=MM_TN, tiled_out=True,
                        sub=SSD_N, name="mm_ssd_bc")
            dt = matmul(xb, w_in[:, SSD_INNER + SSD_XBC:], tn=SSD_HEADS, name="mm_ssd_dt")
            prm = (conv_w, conv_b, dt_bias, a_log, d_skip, norm_w)
            zc = jnp.zeros((1, SSD_CONV - 1, SSD_XBC), F32)
            zh = jnp.zeros((1, SSD_HEADS, SSD_INNER // SSD_HEADS, SSD_N), F32)
            y, pc, ph = ssd_core(zx, bc, dt, zc, zh, prm, row0=0, streams=1, length=lp, chunk=128)
            y, sc, sh = ssd_core(zx, bc, dt, state_ssd_conv, state_ssd, prm, row0=lp, streams=sb,
                                 length=ls, chunk=ls, y_prev=y)
            outs['conv'], outs['ssd'] = (pc, sc), (ph, sh)
            act, w_proj, glu = y, w_out, False
        elif kind == 1:
            w_qkv, sinks, w_out = swa_p
            kvw = SWA_KVH * SWA_DH
            q = matmul(xb, w_qkv, col0=0, ncols=D, out_dtype=BF16, name="mm_swa_q")
            kv = matmul(xb, w_qkv, col0=D, ncols=2 * kvw, name="mm_swa_kv")
            o = swa_core(q, kv, None, None, sinks, row0=0, tiles=lp // 256, qt=256, prompt=True)
            ck = cache_k.reshape(sb, WINDOW, kvw)
            cv = cache_v.reshape(sb, WINDOW, kvw)
            o = swa_core(q, kv, ck, cv, sinks, row0=lp, tiles=sb, qt=ls, prompt=False, o_prev=o)
            kshape = (SWA_KVH, SWA_DH)
            pk = kv[lp - WINDOW:lp, :kvw].reshape((1, WINDOW) + kshape)
            pv = kv[lp - WINDOW:lp, kvw:].reshape((1, WINDOW) + kshape)
            kvs = kv[lp:].reshape(sb, ls, 2 * kvw)
            sk = jnp.concatenate([ck, kvs[:, :, :kvw]], axis=1)[:, -WINDOW:].reshape((sb, WINDOW) + kshape)
            sv = jnp.concatenate([cv, kvs[:, :, kvw:]], axis=1)[:, -WINDOW:].reshape((sb, WINDOW) + kshape)
            outs['k'], outs['v'] = (pk, sk), (pv, sv)
            act, w_proj, glu = o, w_out, False
        elif kind == 2:
            w_in, w_glu = s5_p[0], s5_p[-1]
            tables = s5_tables(s5_p[1:-1])
            u = matmul(xb, w_in, name="mm_s5_in")
            zs = jnp.zeros((1, S5_G, S5_N), F32)
            y, pr, pi = s5_core(u, zs, zs, tables, row0=0, streams=1, length=lp)
            y, sr, si = s5_core(u, s5_re, s5_im, tables, row0=lp, streams=sb, length=ls, y_prev=y)
            outs['s5r'], outs['s5i'] = (pr, sr), (pi, si)
            act, w_proj, glu = y, w_glu, True
        else:
            w_in, lb_param, norm_w, w_out = hg_p
            qfig = matmul(xb, w_in, name="mm_hg_in")
            zs = jnp.zeros((1, HG_H, HG_K, HG_K), F32)
            o, ps = hgrn_core(qfig, lb_param, norm_w, zs, layer=layer, row0=0, streams=1, length=lp, chunk=128)
            o, ss = hgrn_core(qfig, lb_param, norm_w, state_hgrn, layer=layer, row0=lp, streams=sb,
                              length=ls, chunk=ls, o_prev=o)
            outs['hg'] = (ps, ss)
            act, w_proj, glu = o, w_out, False
        wr = jnp.concatenate([w_rg[layer], w_re[layer], jnp.zeros((D, 128 - MOE_G - MOE_E), F32)], axis=1)
        br = jnp.concatenate([b_rg[layer], b_re[layer], jnp.zeros((128 - MOE_G - MOE_E,), F32)]).reshape(1, 128)
        x1, x1p, route, route_t, counts = proj_ln_route(x, act, w_proj.astype(BF16), ln1_g[layer], ln1_b[layer],
                                                        wr, br, glu=glu)
        y0, y1 = moe_layer(x1p, route_t, counts, w_gate, w_up, w_down, layer)
        x, xb = ln_combine(x1, y0, y1, route, ln2_g[layer], ln2_b[layer],
                           split=lp if layer == DEPTH - 1 else None)
    y_prompt = x.reshape(1, lp, D)
    y_sample = xb.reshape(sb, ls, D)
    order = ('conv', 'ssd', 'k', 'v', 's5r', 's5i', 'hg')
    return (y_prompt, y_sample) + tuple(outs[k][0] for k in order) + tuple(outs[k][1] for k in order)


def kernel(x_prompt, x_sample, state_ssd_conv, state_ssd, cache_swa_k, cache_swa_v, state_s5_re, state_s5_im, state_hgrn, ssd_w_in, ssd_conv_w, ssd_conv_b, ssd_dt_bias, ssd_a_log, ssd_d, ssd_norm_w, ssd_w_out, swa_w_qkv, swa_sinks, swa_w_out, s5_w_in, s5_a_re, s5_a_im, s5_log_dt, s5_b_re, s5_b_im, s5_c_re, s5_c_im, s5_d, s5_w_glu, hg_w_in, hg_lb, hg_norm_w, hg_w_out, ln1_g, ln1_b, ln2_g, ln2_b, moe_w_rg, moe_b_rg, moe_w_re, moe_b_re, moe_w_gate, moe_w_up, moe_w_down):
    states = (state_ssd_conv, state_ssd, cache_swa_k, cache_swa_v, state_s5_re, state_s5_im, state_hgrn)
    ssd_p = (ssd_w_in, ssd_conv_w, ssd_conv_b, ssd_dt_bias, ssd_a_log, ssd_d, ssd_norm_w, ssd_w_out)
    swa_p = (swa_w_qkv, swa_sinks, swa_w_out)
    s5_p = (s5_w_in, s5_a_re, s5_a_im, s5_log_dt, s5_b_re, s5_b_im, s5_c_re, s5_c_im, s5_d, s5_w_glu)
    hg_p = (hg_w_in, hg_lb, hg_norm_w, hg_w_out)
    ln_p = (ln1_g, ln1_b, ln2_g, ln2_b)
    moe_p = (moe_w_rg, moe_b_rg, moe_w_re, moe_b_re, moe_w_gate, moe_w_up, moe_w_down)
    return _forward(x_prompt, x_sample, states, ssd_p, swa_p, s5_p, hg_p, ln_p, moe_p)
```

```python
import functools

import jax
import jax.numpy as jnp
from jax import lax
from jax.experimental import pallas as pl
from jax.experimental.pallas import tpu as pltpu

F32 = jnp.float32
BF16 = jnp.bfloat16
HIGHEST = lax.Precision.HIGHEST

D = 2048
DEPTH = 4
DN_ALPHA = (2 * DEPTH) ** 0.25
LN_EPS = 1e-5
RMS_EPS = 1e-6
NEG = -1e30

VMEM_LIMIT = 56 * 1024 * 1024
MM_TN = 1024

SSD_INNER = 4096
SSD_HEADS = 64
SSD_G = 8
SSD_HPG = 8
SSD_N = 128
SSD_GW = SSD_INNER // SSD_G
SSD_CONV = 4
SSD_XBC = SSD_INNER + 2 * SSD_G * SSD_N

SWA_DH = 64
SWA_QH = 32
SWA_KVH = 4
SWA_GRP = SWA_QH // SWA_KVH
WINDOW = 128
CHUNK = 64

S5_G = 128
S5_CH = 16
S5_N = 64
S5_SUB = 16
S5_GB = 8
S5_CW = S5_GB * S5_CH
S5_KR = 256

HG_H = 16
HG_K = 128
HG_LEAF = 16

MOE_G = 4
MOE_PG = 8
MOE_E = 32
D_EXPERT = 256
MOE_TM = 256


def _cparams(sem):
    return pltpu.CompilerParams(dimension_semantics=sem, vmem_limit_bytes=VMEM_LIMIT)


def _sigmoid(x):
    return 1.0 / (1.0 + jnp.exp(-x))


def _silu(x):
    return x * _sigmoid(x)


def _softplus(x):
    return jnp.maximum(x, 0.0) + jnp.log(1.0 + jnp.exp(-jnp.abs(x)))


def _dot(a, b, precision=None):
    return jnp.dot(a, b, preferred_element_type=F32, precision=precision)


def _dot_nt(a, b):
    return lax.dot_general(a, b, (((1,), (1,)), ((), ())), preferred_element_type=F32)


def _tri(n, upper=False):
    r = lax.broadcasted_iota(jnp.int32, (n, n), 0)
    c = lax.broadcasted_iota(jnp.int32, (n, n), 1)
    return (r <= c) if upper else (c <= r)


def _transpose_rows(x):
    c = x.shape[0]
    if c == 128:
        return x.T
    pad = jnp.zeros((128 - c, 128), x.dtype)
    return jnp.concatenate([x, pad], axis=0).T[:, :c]


def _pack_halves(x):
    n = x.shape[1] // 2
    lo = lax.bitcast_convert_type(x[:, :n].astype(BF16).astype(F32), jnp.uint32)
    hi = lax.bitcast_convert_type(x[:, n:].astype(BF16).astype(F32), jnp.uint32)
    return (lo >> 16) | hi


def _unpack_halves(w):
    lo = lax.bitcast_convert_type(w << 16, F32)
    hi = lax.bitcast_convert_type(w & jnp.uint32(0xFFFF0000), F32)
    return lo, hi


def _mm_body(x_ref, w_ref, o_ref, wb_ref, *, tiled_out):
    @pl.when(pl.program_id(1) == 0)
    def _():
        wb_ref[...] = w_ref[...].astype(BF16)

    r = _dot(x_ref[...], wb_ref[...]).astype(o_ref.dtype)
    if tiled_out:
        sub = o_ref.shape[2]
        for q in range(o_ref.shape[0]):
            o_ref[q] = r[:, sub * q:sub * (q + 1)]
    else:
        o_ref[...] = r


def matmul(x, w, *, col0=0, ncols=None, tn=None, tm=512, out_dtype=F32, tiled_out=False, sub=None,
           name="matmul"):
    m, k = x.shape
    ncols = w.shape[1] - col0 if ncols is None else ncols
    tm = min(tm, m)
    if tn is None:
        tn = MM_TN if k * MM_TN * 4 <= 8 * 1024 * 1024 else MM_TN // 2
        tn = min(tn, ncols)
    assert col0 % tn == 0 and ncols % tn == 0 and m % tm == 0
    nj = ncols // tn
    j0 = col0 // tn
    if tiled_out:
        sub = tn if sub is None else sub
        out_shape = jax.ShapeDtypeStruct((ncols // sub, m, sub), out_dtype)
        out_spec = pl.BlockSpec((tn // sub, tm, sub), lambda j, i: (j, i, 0))
    else:
        out_shape = jax.ShapeDtypeStruct((m, ncols), out_dtype)
        out_spec = pl.BlockSpec((tm, tn), lambda j, i: (i, j))
    return pl.pallas_call(
        functools.partial(_mm_body, tiled_out=tiled_out),
        grid=(nj, m // tm),
        in_specs=[pl.BlockSpec((tm, k), lambda j, i: (i, 0)),
                  pl.BlockSpec((k, tn), lambda j, i: (0, j + j0))],
        out_specs=out_spec,
        out_shape=out_shape,
        scratch_shapes=[pltpu.VMEM((k, tn), BF16)],
        compiler_params=_cparams(("arbitrary", "arbitrary")),
        name=name,
    )(x, w)


def _glu_body(x_ref, wv_ref, wg_ref, o_ref, wvb_ref, wgb_ref):
    @pl.when(pl.program_id(1) == 0)
    def _():
        wvb_ref[...] = wv_ref[...].astype(BF16)
        wgb_ref[...] = wg_ref[...].astype(BF16)

    x = x_ref[...].astype(BF16)
    o_ref[...] = _dot(x, wvb_ref[...]) * _sigmoid(_dot(x, wgb_ref[...]))


def glu_matmul(x, w, *, tn=512, tm=512):
    m, k = x.shape
    n = w.shape[1] // 2
    tm = min(tm, m)
    nj = n // tn
    return pl.pallas_call(
        _glu_body,
        grid=(nj, m // tm),
        in_specs=[pl.BlockSpec((tm, k), lambda j, i: (i, 0)),
                  pl.BlockSpec((k, tn), lambda j, i: (0, j)),
                  pl.BlockSpec((k, tn), lambda j, i: (0, j + nj))],
        out_specs=pl.BlockSpec((tm, tn), lambda j, i: (i, j)),
        out_shape=jax.ShapeDtypeStruct((m, n), F32),
        scratch_shapes=[pltpu.VMEM((k, tn), BF16), pltpu.VMEM((k, tn), BF16)],
        compiler_params=_cparams(("arbitrary", "arbitrary")),
        name="glu_matmul",
    )(x, w, w)


def _layer_norm(v, g, b):
    mu = jnp.mean(v, axis=-1, keepdims=True)
    vc = v - mu
    var = jnp.mean(vc * vc, axis=-1, keepdims=True)
    return vc * lax.rsqrt(var + LN_EPS) * g + b


def _route(x, wh, wl, br, carry):
    xh = x.astype(BF16)
    xl = (x - xh.astype(F32)).astype(BF16)
    lg = _dot(xh, wh) + (_dot(xh, wl) + _dot(xl, wh)) + br
    tm = lg.shape[0]
    lane = lax.broadcasted_iota(jnp.int32, lg.shape, 1)
    lanef = lane.astype(F32)
    big = jnp.float32(1e9)
    is_g = lane < MOE_G
    gl = jnp.where(is_g, lg, NEG)
    gmax = jnp.max(gl, axis=-1, keepdims=True)
    gsel = jnp.min(jnp.where(is_g & (gl == gmax), lanef, big), axis=-1, keepdims=True)
    gprob = 1.0 / jnp.sum(jnp.where(is_g, jnp.exp(gl - gmax), 0.0), axis=-1, keepdims=True)
    lo = MOE_G + MOE_PG * gsel
    is_e = (lanef >= lo) & (lanef < lo + MOE_PG)
    el = jnp.where(is_e, lg, NEG)
    m1 = jnp.max(el, axis=-1, keepdims=True)
    l1 = jnp.min(jnp.where(is_e & (el == m1), lanef, big), axis=-1, keepdims=True)
    is_e2 = is_e & (lanef != l1)
    el2 = jnp.where(is_e2, lg, NEG)
    m2 = jnp.max(el2, axis=-1, keepdims=True)
    l2 = jnp.min(jnp.where(is_e2 & (el2 == m2), lanef, big), axis=-1, keepdims=True)
    r = jnp.exp(m2 - m1)
    w1 = gprob / (1.0 + r)
    w2 = gprob * r / (1.0 + r)
    hit1 = lanef == l1
    hit2 = lanef == l2
    oh = jnp.where(hit1 | hit2, 1.0, 0.0)
    rr = lax.broadcasted_iota(jnp.int32, (tm, tm), 0)
    cc = lax.broadcasted_iota(jnp.int32, (tm, tm), 1)
    before = _dot(jnp.where(cc < rr, 1.0, 0.0).astype(BF16), oh.astype(BF16)) + carry
    k1 = jnp.sum(jnp.where(hit1, before, 0.0), axis=-1, keepdims=True)
    k2 = jnp.sum(jnp.where(hit2, before, 0.0), axis=-1, keepdims=True)
    table = jnp.where(lane == 0, l1 - MOE_G,
                      jnp.where(lane == 1, l2 - MOE_G,
                                jnp.where(lane == 2, w1,
                                          jnp.where(lane == 3, w2,
                                                    jnp.where(lane == 4, k1, jnp.where(lane == 5, k2, 0.0))))))
    return table, carry + jnp.sum(oh, axis=0, keepdims=True)


def _ln_route_body(x_ref, h_ref, g_ref, b_ref, wh_ref, wl_ref, br_ref, o_ref, op_ref, r_ref, rt_ref, cnt_ref):
    @pl.when(pl.program_id(0) == 0)
    def _():
        cnt_ref[...] = jnp.zeros_like(cnt_ref)

    y = _layer_norm(DN_ALPHA * x_ref[...] + h_ref[...], g_ref[...], b_ref[...])
    o_ref[...] = y
    op_ref[...] = _pack_halves(y)
    table, cnt_ref[...] = _route(y, wh_ref[...], wl_ref[...], br_ref[...], cnt_ref[...])
    r_ref[...] = table
    rt_ref[...] = jnp.concatenate([table[128 * q:128 * (q + 1)].T[:8] for q in range(table.shape[0] // 128)],
                                  axis=1)


def ln_route(x, h, g, b, wr, br, tm=256):
    m = x.shape[0]
    row = pl.BlockSpec((tm, D), lambda i: (i, 0))
    vec = pl.BlockSpec((1, D), lambda i: (0, 0))
    one = pl.BlockSpec((1, 128), lambda i: (0, 0))
    wmat = pl.BlockSpec((D, 128), lambda i: (0, 0))
    wh = wr.astype(BF16)
    wl = (wr - wh.astype(F32)).astype(BF16)
    return pl.pallas_call(
        _ln_route_body,
        grid=(m // tm,),
        in_specs=[row, row, vec, vec, wmat, wmat, one],
        out_specs=[row, pl.BlockSpec((tm, D // 2), lambda i: (i, 0)), pl.BlockSpec((tm, 128), lambda i: (i, 0)),
                   pl.BlockSpec((8, tm), lambda i: (0, i)), one],
        out_shape=[jax.ShapeDtypeStruct((m, D), F32), jax.ShapeDtypeStruct((m, D // 2), jnp.uint32),
                   jax.ShapeDtypeStruct((m, 128), F32), jax.ShapeDtypeStruct((8, m), F32),
                   jax.ShapeDtypeStruct((1, 128), F32)],
        compiler_params=_cparams(("arbitrary",)),
        name="ln_route",
    )(x, h, g.reshape(1, D), b.reshape(1, D), wh, wl, br)


def _ln_combine_body(x_ref, y0_ref, y1_ref, r_ref, g_ref, b_ref, o_ref, ob_ref, *, split_tiles):
    r = r_ref[...]
    a_lo, a_hi = _unpack_halves(y0_ref[...])
    b_lo, b_hi = _unpack_halves(y1_ref[...])
    w0, w1 = r[:, 2:3], r[:, 3:4]
    f = jnp.concatenate([w0 * a_lo + w1 * b_lo, w0 * a_hi + w1 * b_hi], axis=1)
    y = _layer_norm(DN_ALPHA * x_ref[...] + f, g_ref[...], b_ref[...])
    if split_tiles is None:
        o_ref[...] = y
        ob_ref[...] = y.astype(BF16)
    else:
        @pl.when(pl.program_id(0) < split_tiles)
        def _():
            o_ref[...] = y

        @pl.when(pl.program_id(0) >= split_tiles)
        def _():
            ob_ref[...] = y


def ln_combine(x, y0, y1, route, g, b, tm=256, split=None):
    m = x.shape[0]
    row = pl.BlockSpec((tm, D), lambda i: (i, 0))
    half = pl.BlockSpec((tm, D // 2), lambda i: (i, 0))
    vec = pl.BlockSpec((1, D), lambda i: (0, 0))
    if split is None:
        st = None
        out_specs = [row, row]
        out_shape = [jax.ShapeDtypeStruct((m, D), F32), jax.ShapeDtypeStruct((m, D), BF16)]
    else:
        st = split // tm
        out_specs = [pl.BlockSpec((tm, D), lambda i: (jnp.minimum(i, st - 1), 0)),
                     pl.BlockSpec((tm, D), lambda i: (jnp.maximum(i - st, 0), 0))]
        out_shape = [jax.ShapeDtypeStruct((split, D), F32), jax.ShapeDtypeStruct((m - split, D), F32)]
    return pl.pallas_call(
        functools.partial(_ln_combine_body, split_tiles=st),
        grid=(m // tm,),
        in_specs=[row, half, half, pl.BlockSpec((tm, 128), lambda i: (i, 0)), vec, vec],
        out_specs=out_specs,
        out_shape=out_shape,
        compiler_params=_cparams(("arbitrary",)),
        name="ln_combine",
    )(x, y0, y1, route, g.reshape(1, D), b.reshape(1, D))


def _ssd_body(z_ref, x_ref, b_ref, c_ref, dtg_ref, dtt_ref, csx_ref, csb_ref, csc_ref, h0_ref,
              cwx_ref, cwb_ref, cwc_ref, cbx_ref, cbb_ref, cbc_ref, dtbr_ref, dtbc_ref,
              alr_ref, alc_ref, dsk_ref, nw_ref, *rest, C, aliased):
    if aliased:
        rest = rest[1:]
    (y_ref, cox_ref, cob_ref, coc_ref, ho_ref,
     tx_ref, tb_ref, tc_ref, ex_ref, eb_ref, ec_ref, ht_ref, yb_ref) = rest
    c = pl.program_id(1)
    last_chunk = c == pl.num_programs(1) - 1

    @pl.when(c == 0)
    def _init():
        tx_ref[...] = jnp.zeros_like(tx_ref)
        tb_ref[...] = jnp.zeros_like(tb_ref)
        tc_ref[...] = jnp.zeros_like(tc_ref)
        for g in range(SSD_G):
            tx_ref[g, 5:8, :] = csx_ref[0, g]
            tb_ref[g, 5:8, :] = csb_ref[0, g]
            tc_ref[g, 5:8, :] = csc_ref[0, g]
        for p in range(SSD_HEADS // 2):
            ht_ref[p] = h0_ref[0, p].T

    tril = _tri(C)
    tril_f = tril.astype(F32)
    triu_f = _tri(C, upper=True).astype(F32)
    lane = lax.broadcasted_iota(jnp.int32, (C, 128), 1)
    left = lane < 64

    def conv(e_ref, t_ref, raw, w_ref, bias_ref, g):
        e_ref[0:8, :] = t_ref[g]
        e_ref[8:8 + C, :] = raw
        w = w_ref[g]
        acc = bias_ref[g] + w[3:4, :] * raw
        for k in range(SSD_CONV - 1):
            acc = acc + w[k:k + 1, :] * e_ref[5 + k:5 + k + C, :]
        t_ref[g] = e_ref[C:C + 8, :]
        return _silu(acc)

    def group(g, carry):
        xs = conv(ex_ref, tx_ref, x_ref[g], cwx_ref, cbx_ref, g)
        bs = conv(eb_ref, tb_ref, b_ref[g], cwb_ref, cbb_ref, g)
        cs = conv(ec_ref, tc_ref, c_ref[g], cwc_ref, cbc_ref, g)

        @pl.when(last_chunk)
        def _():
            cox_ref[0, g] = ex_ref[C + 5:C + 8, :]
            cob_ref[0, g] = eb_ref[C + 5:C + 8, :]
            coc_ref[0, g] = ec_ref[C + 5:C + 8, :]

        dtv = _softplus(dtg_ref[0, g] + dtbr_ref[g])
        dtvt = _softplus(dtt_ref[0, g] + dtbc_ref[g])
        cum = _dot(tril_f, dtv * (-jnp.exp(alr_ref[g])), precision=HIGHEST)
        cumt = _dot(dtvt * (-jnp.exp(alc_ref[g])), triu_f, precision=HIGHEST)
        bsb = bs.astype(BF16)
        csb = cs.astype(BF16)
        cb = _dot_nt(csb, bsb)
        bst = _transpose_rows(bs).astype(BF16)
        dsk = dsk_ref[g]
        ys = []
        for j in range(SSD_HPG // 2):
            h0, h1 = 2 * j, 2 * j + 1
            c0, c1 = cum[:, h0:h0 + 1], cum[:, h1:h1 + 1]
            l0 = jnp.where(tril, jnp.exp(c0 - cumt[h0:h0 + 1, :]), 0.0) * cb
            l1 = jnp.where(tril, jnp.exp(c1 - cumt[h1:h1 + 1, :]), 0.0) * cb
            lhs = jnp.concatenate([l0, l1], axis=1).astype(BF16)
            xp = xs[:, 128 * j:128 * (j + 1)]
            xdt = xp * jnp.where(left, dtv[:, h0:h0 + 1], dtv[:, h1:h1 + 1])
            rhs = jnp.concatenate([jnp.where(left, xdt, 0.0), jnp.where(left, 0.0, xdt)],
                                  axis=0).astype(BF16)
            htp = ht_ref[g * 4 + j]
            y = _dot(lhs, rhs)
            y = y + _dot(csb, htp.astype(BF16)) * jnp.where(left, jnp.exp(c0), jnp.exp(c1))
            e0, e1 = cum[C - 1:C, h0:h0 + 1], cum[C - 1:C, h1:h1 + 1]
            wgt = (xdt * jnp.where(left, jnp.exp(e0 - c0), jnp.exp(e1 - c1))).astype(BF16)
            ht_ref[g * 4 + j] = jnp.where(left[0:1, :], jnp.exp(e0), jnp.exp(e1)) * htp + _dot(bst, wgt)
            ys.append(y + dsk[:, 128 * j:128 * (j + 1)] * xp)
        y = jnp.concatenate(ys, axis=1) * _silu(z_ref[g])
        ms = jnp.mean(y * y, axis=-1, keepdims=True)
        yb_ref[g] = (y * lax.rsqrt(ms + RMS_EPS) * nw_ref[g]).astype(BF16)
        return carry

    lax.fori_loop(0, SSD_G, group, 0)
    for g in range(SSD_G):
        y_ref[:, SSD_GW * g:SSD_GW * (g + 1)] = yb_ref[g]

    @pl.when(last_chunk)
    def _fin():
        for p in range(SSD_HEADS // 2):
            ho_ref[0, p] = ht_ref[p].T


def ssd_core(zx, bc, dt, conv_state, h0, params, *, row0, streams, length, chunk, y_prev=None):
    conv_w, conv_b, dt_bias, a_log, d_skip, norm_w = params
    S, L, C = streams, length, chunk
    nch = L // C
    rb0 = row0 // C
    t_all = zx.shape[1]
    dseg = dt[row0:row0 + S * L].reshape(S, L, SSD_G, SSD_HPG)
    dtg = dseg.transpose(0, 2, 1, 3)
    dtt = dseg.transpose(0, 2, 3, 1)

    def split(a, lead):
        ax = a[..., :SSD_INNER].reshape(lead + (SSD_G, SSD_GW))
        ab = a[..., SSD_INNER:SSD_INNER + SSD_G * SSD_N].reshape(lead + (SSD_G, SSD_N))
        ac = a[..., SSD_INNER + SSD_G * SSD_N:].reshape(lead + (SSD_G, SSD_N))
        return ax, ab, ac

    csx, csb, csc = (jnp.moveaxis(a, 2, 1) for a in split(conv_state, (S, SSD_CONV - 1)))
    cwx, cwb, cwc = (jnp.moveaxis(a, 1, 0) for a in split(conv_w, (SSD_CONV,)))
    cbx, cbb, cbc = (jnp.moveaxis(a, 1, 0) for a in split(conv_b.reshape(1, -1), (1,)))
    dtbr = dt_bias.reshape(SSD_G, 1, SSD_HPG)
    dtbc = dt_bias.reshape(SSD_G, SSD_HPG, 1)
    alr = a_log.reshape(SSD_G, 1, SSD_HPG)
    alc = a_log.reshape(SSD_G, SSD_HPG, 1)
    dsk = jnp.repeat(d_skip, SSD_INNER // SSD_HEADS).reshape(SSD_G, 1, SSD_GW)
    nw = norm_w.reshape(SSD_G, 1, SSD_GW)
    h0p = h0.reshape(S, SSD_HEADS // 2, 128, SSD_N)

    def rb(s, c):
        return rb0 + s * nch + c

    def full(a):
        nd = a.ndim
        return pl.BlockSpec(a.shape, lambda s, c: (0,) * nd)

    def per_stream(a):
        nd = a.ndim
        return pl.BlockSpec((1,) + a.shape[1:], lambda s, c: (s,) + (0,) * (nd - 1))

    in_specs = [
        pl.BlockSpec((SSD_G, C, SSD_GW), lambda s, c: (0, rb(s, c), 0)),
        pl.BlockSpec((SSD_G, C, SSD_GW), lambda s, c: (1, rb(s, c), 0)),
        pl.BlockSpec((SSD_G, C, SSD_N), lambda s, c: (0, rb(s, c), 0)),
        pl.BlockSpec((SSD_G, C, SSD_N), lambda s, c: (1, rb(s, c), 0)),
        pl.BlockSpec((1, SSD_G, C, SSD_HPG), lambda s, c: (s, 0, c, 0)),
        pl.BlockSpec((1, SSD_G, SSD_HPG, C), lambda s, c: (s, 0, 0, c)),
        per_stream(csx), per_stream(csb), per_stream(csc), per_stream(h0p),
        full(cwx), full(cwb), full(cwc), full(cbx), full(cbb), full(cbc),
        full(dtbr), full(dtbc), full(alr), full(alc), full(dsk), full(nw),
    ]
    args = [zx, zx, bc, bc, dtg, dtt, csx, csb, csc, h0p, cwx, cwb, cwc, cbx, cbb, cbc,
            dtbr, dtbc, alr, alc, dsk, nw]
    aliases = {}
    if y_prev is not None:
        in_specs.append(pl.BlockSpec(memory_space=pl.ANY))
        args.append(y_prev)
        aliases = {len(args) - 1: 0}
    out_shape = [
        jax.ShapeDtypeStruct((t_all, SSD_INNER), BF16),
        jax.ShapeDtypeStruct(csx.shape, F32), jax.ShapeDtypeStruct(csb.shape, F32),
        jax.ShapeDtypeStruct(csc.shape, F32), jax.ShapeDtypeStruct(h0p.shape, F32),
    ]
    out_specs = [
        pl.BlockSpec((C, SSD_INNER), lambda s, c: (rb(s, c), 0)),
        per_stream(csx), per_stream(csb), per_stream(csc), per_stream(h0p),
    ]
    scratch = [
        pltpu.VMEM((SSD_G, 8, SSD_GW), F32), pltpu.VMEM((SSD_G, 8, SSD_N), F32),
        pltpu.VMEM((SSD_G, 8, SSD_N), F32),
        pltpu.VMEM((C + 8, SSD_GW), F32), pltpu.VMEM((C + 8, SSD_N), F32), pltpu.VMEM((C + 8, SSD_N), F32),
        pltpu.VMEM((SSD_HEADS // 2, SSD_N, 128), F32),
        pltpu.VMEM((SSD_G, C, SSD_GW), BF16),
    ]
    y, cox, cob, coc, ho = pl.pallas_call(
        functools.partial(_ssd_body, C=C, aliased=y_prev is not None),
        grid=(S, nch), in_specs=in_specs, out_specs=out_specs, out_shape=out_shape,
        scratch_shapes=scratch, input_output_aliases=aliases,
        compiler_params=_cparams(("arbitrary", "arbitrary")),
        name="ssd_core",
    )(*args)
    conv_out = jnp.concatenate([jnp.moveaxis(a, 1, 2).reshape(S, SSD_CONV - 1, -1) for a in (cox, cob, coc)],
                               axis=-1)
    return y, conv_out, ho.reshape(S, SSD_HEADS, SSD_INNER // SSD_HEADS, SSD_N)


def _swa_body(sink_ref, q_ref, pk_ref, pv_ref, kv_ref, *rest, QT, prev_valid, aliased):
    o_ref = rest[-1]
    i = pl.program_id(0)
    kvw = SWA_KVH * SWA_DH
    nback = WINDOW // CHUNK
    rb = min(QT, 2 * CHUNK)
    span = rb + WINDOW
    kf = jnp.concatenate([pk_ref[...], kv_ref[:, :kvw]], axis=0).astype(BF16)
    vf = jnp.concatenate([pv_ref[...], kv_ref[:, kvw:]], axis=0).astype(BF16)
    r2 = lax.broadcasted_iota(jnp.int32, (128, 128), 0)
    c2 = lax.broadcasted_iota(jnp.int32, (128, 128), 1)
    swap = jnp.where((r2 + SWA_DH) % 128 == c2, 1.0, 0.0).astype(BF16)
    left = lax.broadcasted_iota(jnp.int32, (1, 128), 1) < SWA_DH
    k_side, v_side = [], []
    for kh in range(SWA_KVH):
        blk = slice(128 * (kh // 2), 128 * (kh // 2 + 1))
        mine = left if kh % 2 == 0 else ~left
        kb, vb = kf[:, blk], jnp.where(mine, vf[:, blk], 0.0).astype(BF16)
        ko, vo = _dot(kb, swap).astype(BF16), _dot(vb, swap).astype(BF16)
        k_side.append((kb, ko) if kh % 2 == 0 else (ko, kb))
        v_side.append((vb, vo) if kh % 2 == 0 else (vo, vb))
    scale = SWA_DH ** -0.5
    for b in range(QT // rb):
        rows = slice(rb * b, rb * (b + 1))
        keys = slice(rb * b, rb * b + span)
        qc = lax.broadcasted_iota(jnp.int32, (rb, span), 0) // CHUNK
        kc = lax.broadcasted_iota(jnp.int32, (rb, span), 1) // CHUNK
        ok = (kc >= qc) & (kc <= qc + nback)
        if not prev_valid and rb * b < WINDOW:
            ok = ok & ((kc + (rb // CHUNK) * b >= nback) | (i > 0))
        for pr in range(SWA_QH // 2):
            kh = 2 * pr // SWA_GRP
            q2 = q_ref[rows, 128 * pr:128 * (pr + 1)] * scale
            acc = None
            for side in range(2):
                qm = jnp.where(left if side == 0 else ~left, q2, 0.0).astype(BF16)
                s = jnp.where(ok, _dot_nt(qm, k_side[kh][side][keys]), NEG)
                sink = sink_ref[2 * pr + side]
                m = jnp.maximum(jnp.max(s, axis=-1, keepdims=True), sink)
                p = jnp.exp(s - m)
                den = jnp.sum(p, axis=-1, keepdims=True) + jnp.exp(sink - m)
                o = _dot(p.astype(BF16), v_side[kh][side][keys]) / den
                acc = o if acc is None else acc + o
            o_ref[rows, 128 * pr:128 * (pr + 1)] = acc.astype(BF16)


def swa_core(q, kv, prev_k, prev_v, sinks, *, row0, tiles, qt, prompt, o_prev=None):
    t_all = q.shape[0]
    rb0 = row0 // qt
    kvw = SWA_KVH * SWA_DH
    if prompt:
        wpt = qt // WINDOW
        prev_map_k = lambda i, s: (jnp.maximum(wpt * (rb0 + i) - 1, 0), 0)
        prev_map_v = lambda i, s: (jnp.maximum(wpt * (rb0 + i) - 1, 0), 1)
        pk_spec = pl.BlockSpec((WINDOW, kvw), prev_map_k)
        pv_spec = pl.BlockSpec((WINDOW, kvw), prev_map_v)
        prev_k = prev_v = kv
    else:
        pk_spec = pl.BlockSpec((None, WINDOW, kvw), lambda i, s: (i, 0, 0))
        pv_spec = pl.BlockSpec((None, WINDOW, kvw), lambda i, s: (i, 0, 0))
    in_specs = [pl.BlockSpec((qt, D), lambda i, s: (rb0 + i, 0)), pk_spec, pv_spec,
                pl.BlockSpec((qt, 2 * kvw), lambda i, s: (rb0 + i, 0))]
    args = [sinks, q, prev_k, prev_v, kv]
    aliases = {}
    if o_prev is not None:
        in_specs.append(pl.BlockSpec(memory_space=pl.ANY))
        args.append(o_prev)
        aliases = {len(args) - 1: 0}
    return pl.pallas_call(
        functools.partial(_swa_body, QT=qt, prev_valid=not prompt, aliased=o_prev is not None),
        grid_spec=pltpu.PrefetchScalarGridSpec(
            num_scalar_prefetch=1, grid=(tiles,), in_specs=in_specs,
            out_specs=pl.BlockSpec((qt, D), lambda i, s: (rb0 + i, 0))),
        out_shape=jax.ShapeDtypeStruct((t_all, D), BF16),
        input_output_aliases=aliases,
        compiler_params=_cparams(("arbitrary",)),
        name="swa_core",
    )(*args)


def _gelu_tanh(y):
    return 0.5 * y * (1.0 + jnp.tanh(0.7978845608028654 * (y + 0.044715 * y * y * y)))


def _s5_body(u_ref, wxr_ref, wxi_ref, wyr_ref, wyi_ref, kt_ref, a_ref, dsk_ref, hr0_ref, hi0_ref, *rest,
             S, R, aliased):
    if aliased:
        rest = rest[1:]
    (y_ref, hro_ref, hio_ref, wx_s, wy_s, ktm_s, xr_s, xi_s, pr_s, pi_s, hr_s, hi_s) = rest
    kb = pl.program_id(1)
    half = S5_CW * S5_N // S5_CH

    @pl.when((pl.program_id(0) == 0) & (kb == 0))
    def _zero():
        ktm_s[...] = jnp.zeros_like(ktm_s)

    @pl.when(kb == 0)
    def _build():
        own = (lax.broadcasted_iota(jnp.int32, (S5_CW, half), 0) // S5_CH
               == lax.broadcasted_iota(jnp.int32, (S5_CW, half), 1) // S5_N)
        for s in range(S5_SUB):
            rows = slice(S5_CW * s, S5_CW * (s + 1))
            wx_s[rows, 0:half] = jnp.where(own, jnp.concatenate([wxr_ref[s]] * 4, axis=1), 0.0).astype(BF16)
            wx_s[rows, half:2 * half] = jnp.where(own, jnp.concatenate([wxi_ref[s]] * 4, axis=1), 0.0).astype(BF16)
        own_t = (lax.broadcasted_iota(jnp.int32, (half, S5_CW), 0) // S5_N
                 == lax.broadcasted_iota(jnp.int32, (half, S5_CW), 1) // S5_CH)
        for t in range(S5_SUB):
            cols = slice(S5_CW * t, S5_CW * (t + 1))
            wy_s[0:half, cols] = jnp.where(own_t, jnp.concatenate([wyr_ref[t]] * 8, axis=0), 0.0).astype(BF16)
            wy_s[half:2 * half, cols] = jnp.where(own_t, jnp.concatenate([wyi_ref[t]] * 8, axis=0), 0.0).astype(BF16)
        same = (lax.broadcasted_iota(jnp.int32, (S5_CW, S5_CW), 0) // S5_CH
                == lax.broadcasted_iota(jnp.int32, (S5_CW, S5_CW), 1) // S5_CH)
        taps = [jnp.where(same, kt_ref[tau], 0.0).astype(BF16) for tau in range(S5_SUB)]
        for s in range(S5_SUB):
            for t in range(s, S5_SUB):
                ktm_s[S5_CW * s:S5_CW * (s + 1), S5_CW * t:S5_CW * (t + 1)] = taps[t - s]
        for st in range(S):
            hr_s[st] = hr0_ref[st]
            hi_s[st] = hi0_ref[st]

    kr = S * R
    ucat = jnp.concatenate([u_ref[pl.ds(s, kr, stride=S5_SUB), :] for s in range(S5_SUB)],
                           axis=1)
    ub = ucat.astype(BF16)
    x = _dot(ub, wx_s[...])
    xr_s[...] = x[:, :half]
    xi_s[...] = x[:, half:]
    ar = a_ref[0:1, :]
    ai = a_ref[1:2, :]
    for st in range(S):
        def step(k, carry):
            hr, hi = carry
            row = st * R + k
            pr_s[pl.ds(row, 1), :] = hr
            pi_s[pl.ds(row, 1), :] = hi
            nr = ar * hr - ai * hi + xr_s[pl.ds(row, 1), :]
            ni = ar * hi + ai * hr + xi_s[pl.ds(row, 1), :]
            return nr, ni

        hr, hi = lax.fori_loop(0, R, step, (hr_s[st], hi_s[st]))
        hr_s[st] = hr
        hi_s[st] = hi
    hprev = jnp.concatenate([pr_s[...], pi_s[...]], axis=1).astype(BF16)
    dsk = jnp.concatenate([dsk_ref[...]] * S5_SUB, axis=1)
    cb = 2 * S5_CW
    intra = jnp.concatenate([_dot(ub[:, :cb * (t + 1)], ktm_s[0:cb * (t + 1), cb * t:cb * (t + 1)])
                             for t in range(S5_SUB // 2)], axis=1)
    y = _gelu_tanh(intra + _dot(hprev, wy_s[...]) + dsk * ucat)
    for t in range(S5_SUB):
        y_ref[pl.ds(t, kr, stride=S5_SUB), :] = y[:, S5_CW * t:S5_CW * (t + 1)]

    @pl.when(kb == pl.num_programs(1) - 1)
    def _fin():
        for st in range(S):
            hro_ref[st] = hr_s[st]
            hio_ref[st] = hi_s[st]


def s5_tables(p):
    a_re, a_im, log_dt, b_re, b_im, c_re, c_im, d_skip = p
    lr, li = a_re.astype(F32), a_im.astype(F32)
    dt = jnp.exp(log_dt.astype(F32))[:, None]
    mag = jnp.exp(lr * dt)
    ab_r, ab_i = mag * jnp.cos(li * dt), mag * jnp.sin(li * dt)
    den = lr * lr + li * li
    co_r = ((ab_r - 1.0) * lr + ab_i * li) / den
    co_i = (ab_i * lr - (ab_r - 1.0) * li) / den
    bb_r = co_r[..., None] * b_re - co_i[..., None] * b_im
    bb_i = co_r[..., None] * b_im + co_i[..., None] * b_re
    pw_r, pw_i = [jnp.ones_like(ab_r)], [jnp.zeros_like(ab_i)]
    for _ in range(S5_SUB):
        r, i = pw_r[-1], pw_i[-1]
        pw_r.append(ab_r * r - ab_i * i)
        pw_i.append(ab_r * i + ab_i * r)
    pr, pi = jnp.stack(pw_r, 0), jnp.stack(pw_i, 0)
    er, ei = pr[S5_SUB - 1::-1][:S5_SUB], pi[S5_SUB - 1::-1][:S5_SUB]
    wx_r = er[..., None] * bb_r[None] - ei[..., None] * bb_i[None]
    wx_i = er[..., None] * bb_i[None] + ei[..., None] * bb_r[None]
    gb, gw = S5_G // S5_GB, S5_GB
    wx_r = wx_r.reshape(S5_SUB, gb, gw, S5_N, S5_CH).transpose(1, 0, 2, 4, 3).reshape(gb, S5_SUB, S5_CW, S5_N)
    wx_i = wx_i.reshape(S5_SUB, gb, gw, S5_N, S5_CH).transpose(1, 0, 2, 4, 3).reshape(gb, S5_SUB, S5_CW, S5_N)
    wxr = jnp.concatenate([wx_r, wx_r], axis=-1)
    wxi = jnp.concatenate([wx_i, wx_i], axis=-1)
    qr, qi = pr[1:], pi[1:]
    cr, ci = c_re.astype(F32), c_im.astype(F32)
    wy_r = cr[None] * qr[:, :, None, :] - ci[None] * qi[:, :, None, :]
    wy_i = -(cr[None] * qi[:, :, None, :] + ci[None] * qr[:, :, None, :])
    wyr = wy_r.reshape(S5_SUB, gb, gw, S5_CH, S5_N).transpose(1, 0, 4, 2, 3).reshape(gb, S5_SUB, S5_N, S5_CW)
    wyi = wy_i.reshape(S5_SUB, gb, gw, S5_CH, S5_N).transpose(1, 0, 4, 2, 3).reshape(gb, S5_SUB, S5_N, S5_CW)
    tr = pr[:S5_SUB, :, None, :] * cr[None] - pi[:S5_SUB, :, None, :] * ci[None]
    ti = pr[:S5_SUB, :, None, :] * ci[None] + pi[:S5_SUB, :, None, :] * cr[None]
    taps = jnp.einsum('agjn,gnk->agjk', tr, bb_r) - jnp.einsum('agjn,gnk->agjk', ti, bb_i)
    taps = taps.reshape(S5_SUB, gb, gw, S5_CH, S5_CH).transpose(1, 0, 2, 4, 3)
    kt = jnp.tile(taps.reshape(gb, S5_SUB, S5_CW, S5_CH), (1, 1, 1, gw))
    a16 = jnp.stack([pr[S5_SUB].reshape(gb, gw * S5_N), pi[S5_SUB].reshape(gb, gw * S5_N)], axis=1)
    dsk = d_skip.astype(F32).reshape(gb, 1, S5_CW)
    return wxr, wxi, wyr, wyi, kt, a16, dsk


def s5_core(u, s_re, s_im, tables, *, row0, streams, length, y_prev=None):
    wxr, wxi, wyr, wyi, kt, a16, dsk = tables
    S, L = streams, length
    gb = S5_G // S5_GB
    rows = S * L // S5_SUB
    if S == 1:
        kr = min(S5_KR, rows)
        spb, rps = 1, kr
    else:
        kr = rows
        spb, rps = S, L // S5_SUB
    nkb = rows // kr
    kb0 = row0 // S5_SUB // kr
    half = S5_GB * S5_N
    hr0 = s_re.reshape(S, gb, 1, half)
    hi0 = s_im.reshape(S, gb, 1, half)
    tab = lambda a: pl.BlockSpec((None,) + a.shape[1:], lambda p, k: (p,) + (0,) * (a.ndim - 1))
    st = pl.BlockSpec((S, None, 1, half), lambda p, k: (0, p, 0, 0))
    uspec = pl.BlockSpec((kr * S5_SUB, S5_CW), lambda p, k: (kb0 + k, p))
    in_specs = [uspec, tab(wxr), tab(wxi), tab(wyr), tab(wyi), tab(kt), tab(a16), tab(dsk), st, st]
    args = [u, wxr, wxi, wyr, wyi, kt, a16, dsk, hr0, hi0]
    aliases = {}
    if y_prev is not None:
        in_specs.append(pl.BlockSpec(memory_space=pl.ANY))
        args.append(y_prev)
        aliases = {len(args) - 1: 0}
    wide = S5_SUB * S5_CW
    y, hro, hio = pl.pallas_call(
        functools.partial(_s5_body, S=spb, R=rps, aliased=y_prev is not None),
        grid=(gb, nkb),
        in_specs=in_specs,
        out_specs=[uspec, st, st],
        out_shape=[jax.ShapeDtypeStruct(u.shape, F32), jax.ShapeDtypeStruct(hr0.shape, F32),
                   jax.ShapeDtypeStruct(hi0.shape, F32)],
        scratch_shapes=[pltpu.VMEM((wide, 2 * half), BF16), pltpu.VMEM((2 * half, wide), BF16),
                        pltpu.VMEM((wide, wide), BF16)]
        + [pltpu.VMEM((kr, half), F32) for _ in range(4)]
        + [pltpu.VMEM((spb, 1, half), F32) for _ in range(2)],
        input_output_aliases=aliases,
        compiler_params=_cparams(("arbitrary", "arbitrary")),
        name="s5_core",
    )(*args)
    return y, hro.reshape(S, S5_G, S5_N), hio.reshape(S, S5_G, S5_N)


def _hgrn_body(q_ref, f_ref, i_ref, g_ref, lbp_ref, nw_ref, s0_ref, *rest, C, layer, aliased):
    if aliased:
        rest = rest[1:]
    o_ref, so_ref, st_ref = rest
    c = pl.program_id(1)

    @pl.when(c == 0)
    def _init():
        for h in range(HG_H):
            st_ref[h] = s0_ref[0, h].T

    lbp = lbp_ref[...]
    e = jnp.exp(lbp - jnp.max(lbp, axis=0, keepdims=True))
    lbs = e / jnp.sum(e, axis=0, keepdims=True)
    lb = jnp.zeros((1, D), F32)
    for r in range(1, layer + 1):
        lb = lb + lbs[r:r + 1, :]
    fz = f_ref[...]
    log_sig = jnp.minimum(fz, 0.0) - jnp.log(1.0 + jnp.exp(-jnp.abs(fz)))
    la = jnp.log(lb)
    lbb = jnp.log(1.0 - lb) + log_sig
    mx = jnp.maximum(la, lbb)
    logf = mx + jnp.log(1.0 + jnp.exp(-jnp.abs(la - lbb)))
    kk = 1.0 - jnp.exp(logf)
    qs = _silu(q_ref[...])
    tril = _tri(C)
    cum = _dot(tril.astype(F32), logf, precision=HIGHEST)
    row = lax.broadcasted_iota(jnp.int32, (C, C), 0)
    col = lax.broadcasted_iota(jnp.int32, (C, C), 1)

    levels = []
    b = C
    while b >= HG_LEAF:
        nb = C // b
        ref = jnp.broadcast_to(cum.reshape(nb, b, D)[:, b // 2 - 1:b // 2, :], (nb, b, D)).reshape(C, D)
        ex = cum - ref
        same = (row // b) == (col // b)
        if b == HG_LEAF:
            mask = same & (col <= row)
            qe, ke = jnp.exp(jnp.minimum(ex, 80.0)), jnp.exp(jnp.minimum(-ex, 80.0))
        else:
            mask = same & ((row % b) >= b // 2) & ((col % b) < b // 2)
            e = jnp.exp(-jnp.abs(ex))
            qe, ke = jnp.where(ex <= 0.0, e, 1.0), jnp.where(ex <= 0.0, 1.0, e)
        levels.append((mask, (qs * qe).astype(BF16), (kk * ke).astype(BF16)))
        b //= 2

    last = cum[C - 1:C, :]
    qin = (qs * jnp.exp(cum)).astype(BF16)
    kin = (kk * jnp.exp(last - cum)).astype(BF16)
    dec = jnp.exp(last)
    vv = i_ref[...]
    vb = vv.astype(BF16)
    gate = _silu(g_ref[...])
    nw = nw_ref[...]
    outs = []
    for h in range(HG_H):
        sl = slice(HG_K * h, HG_K * (h + 1))
        att = jnp.zeros((C, C), F32)
        for mask, ql, kl in levels:
            att = att + jnp.where(mask, _dot_nt(ql[:, sl], kl[:, sl]), 0.0)
        st = st_ref[h]
        o = _dot(att.astype(BF16), vb[:, sl]) + _dot_nt(qin[:, sl], st.astype(BF16))
        st_ref[h] = st * dec[:, sl] + _dot(_transpose_rows(vv[:, sl]).astype(BF16), kin[:, sl])
        ms = jnp.mean(o * o, axis=-1, keepdims=True)
        outs.append(o * lax.rsqrt(ms + RMS_EPS) * nw)
    o_ref[...] = (jnp.concatenate(outs, axis=1) * gate).astype(BF16)

    @pl.when(c == pl.num_programs(1) - 1)
    def _fin():
        for h in range(HG_H):
            so_ref[0, h] = st_ref[h].T


def hgrn_core(qfig, lb_param, norm_w, s0, *, layer, row0, streams, length, chunk, o_prev=None):
    S, L, C = streams, length, chunk
    nch = L // C
    rb0 = row0 // C
    t_all = qfig.shape[0]
    rowspec = lambda j: pl.BlockSpec((C, D), lambda s, c: (rb0 + s * nch + c, j))
    in_specs = [rowspec(0), rowspec(1), rowspec(2), rowspec(3),
                pl.BlockSpec((DEPTH, D), lambda s, c: (0, 0)),
                pl.BlockSpec((1, HG_K), lambda s, c: (0, 0)),
                pl.BlockSpec((1, HG_H, HG_K, HG_K), lambda s, c: (s, 0, 0, 0))]
    args = [qfig, qfig, qfig, qfig, lb_param, norm_w.reshape(1, HG_K), s0]
    aliases = {}
    if o_prev is not None:
        in_specs.append(pl.BlockSpec(memory_space=pl.ANY))
        args.append(o_prev)
        aliases = {len(args) - 1: 0}
    return pl.pallas_call(
        functools.partial(_hgrn_body, C=C, layer=layer, aliased=o_prev is not None),
        grid=(S, nch), in_specs=in_specs,
        out_specs=[rowspec(0), pl.BlockSpec((1, HG_H, HG_K, HG_K), lambda s, c: (s, 0, 0, 0))],
        out_shape=[jax.ShapeDtypeStruct((t_all, D), BF16), jax.ShapeDtypeStruct(s0.shape, F32)],
        scratch_shapes=[pltpu.VMEM((HG_H, HG_K, HG_K), F32)],
        input_output_aliases=aliases,
        compiler_params=_cparams(("arbitrary", "arbitrary")),
        name="hgrn_core",
    )(*args)


def _expert_body(te_ref, nt_ref, x_ref, wg_ref, wu_ref, wd_ref, o_ref, wgb_ref, wub_ref, wdb_ref):
    i = pl.program_id(0)
    prev = te_ref[jnp.maximum(i - 1, 0)]

    @pl.when((i == 0) | (te_ref[i] != prev))
    def _():
        wgb_ref[...] = wg_ref[0, 0].astype(BF16)
        wub_ref[...] = wu_ref[0, 0].astype(BF16)
        wdb_ref[...] = wd_ref[0, 0].astype(BF16)

    @pl.when(i < nt_ref[0])
    def _():
        lo, hi = _unpack_halves(x_ref[...])
        lo, hi = lo.astype(BF16), hi.astype(BF16)
        k = D // 2
        gate = _dot(lo, wgb_ref[0:k, :]) + _dot(hi, wgb_ref[k:D, :])
        up = _dot(lo, wub_ref[0:k, :]) + _dot(hi, wub_ref[k:D, :])
        o_ref[...] = _pack_halves(_dot((_silu(gate) * up).astype(BF16), wdb_ref[...]))

    @pl.when(i >= nt_ref[0])
    def _():
        o_ref[...] = jnp.zeros_like(o_ref)


def expert_mlp(xs, tile_expert, n_tiles, w_gate, w_up, w_down, layer):
    p = xs.shape[0]
    nt = p // MOE_TM
    wspec = lambda shp: pl.BlockSpec((1, 1) + shp, lambda i, te, n: (layer, te[i], 0, 0))
    return pl.pallas_call(
        _expert_body,
        grid_spec=pltpu.PrefetchScalarGridSpec(
            num_scalar_prefetch=2, grid=(nt,),
            in_specs=[pl.BlockSpec((MOE_TM, D // 2), lambda i, te, n: (i, 0)),
                      wspec((D, D_EXPERT)), wspec((D, D_EXPERT)), wspec((D_EXPERT, D))],
            out_specs=pl.BlockSpec((MOE_TM, D // 2), lambda i, te, n: (i, 0)),
            scratch_shapes=[pltpu.VMEM((D, D_EXPERT), BF16), pltpu.VMEM((D, D_EXPERT), BF16),
                            pltpu.VMEM((D_EXPERT, D), BF16)]),
        out_shape=jax.ShapeDtypeStruct((p, D // 2), jnp.uint32),
        compiler_params=_cparams(("arbitrary",)),
        name="expert_mlp",
    )(tile_expert, n_tiles, xs, w_gate, w_up, w_down)


def moe_layer(x1, route_t, counts, w_gate, w_up, w_down, layer):
    t = x1.shape[0]
    e0 = route_t[0].astype(jnp.int32)
    e1 = route_t[1].astype(jnp.int32)
    cnt = counts[0, MOE_G:MOE_G + MOE_E].astype(jnp.int32)
    padded = (cnt + MOE_TM - 1) // MOE_TM * MOE_TM
    ends = jnp.cumsum(padded)
    starts = ends - padded
    d0 = starts[e0] + route_t[4].astype(jnp.int32)
    d1 = starts[e1] + route_t[5].astype(jnp.int32)
    p_rows = (2 * t + MOE_E * (MOE_TM - 1)) // 512 * 512 + 512
    nt = p_rows // MOE_TM
    tok = jnp.arange(t, dtype=jnp.int32)
    src = (jnp.arange(p_rows, dtype=jnp.int32) % t).at[jnp.concatenate([d0, d1])].set(
        jnp.concatenate([tok, tok]), mode="promise_in_bounds", unique_indices=True)
    tile_start = jnp.arange(nt, dtype=jnp.int32) * MOE_TM
    tile_expert = jnp.minimum(jnp.sum((tile_start[:, None] >= ends[None, :]).astype(jnp.int32), axis=1),
                              MOE_E - 1).astype(jnp.int32)
    n_tiles = (ends[-1] // MOE_TM).astype(jnp.int32).reshape(1)
    xs = x1.at[src].get(mode="promise_in_bounds")
    ys = expert_mlp(xs, tile_expert, n_tiles, w_gate, w_up, w_down, layer)
    return ys.at[d0].get(mode="promise_in_bounds"), ys.at[d1].get(mode="promise_in_bounds")


def _forward(x_prompt, x_sample, states, ssd_p, swa_p, s5_p, hg_p, ln_p, moe_p):
    (state_ssd_conv, state_ssd, cache_k, cache_v, s5_re, s5_im, state_hgrn) = states
    lp = x_prompt.shape[1]
    sb, ls = x_sample.shape[0], x_sample.shape[1]
    t_all = lp + sb * ls
    ln1_g, ln1_b, ln2_g, ln2_b = ln_p
    w_rg, b_rg, w_re, b_re, w_gate, w_up, w_down = moe_p

    x = jnp.concatenate([x_prompt.reshape(lp, D), x_sample.reshape(sb * ls, D)], axis=0)
    xb = x.astype(BF16)
    outs = {}
    for layer in range(DEPTH):
        kind = layer % 4
        if kind == 0:
            w_in, conv_w, conv_b, dt_bias, a_log, d_skip, norm_w, w_out = ssd_p
            zx = matmul(xb, w_in, col0=0, ncols=2 * SSD_INNER, tn=MM_TN, tiled_out=True, sub=SSD_GW,
                        name="mm_ssd_zx")
            bc = matmul(xb, w_in, col0=2 * SSD_INNER, ncols=2 * SSD_G * SSD_N, tn=MM_TN, tiled_out=True,
                        sub=SSD_N, name="mm_ssd_bc")
            dt = matmul(xb, w_in[:, SSD_INNER + SSD_XBC:], tn=SSD_HEADS, name="mm_ssd_dt")
            prm = (conv_w, conv_b, dt_bias, a_log, d_skip, norm_w)
            zc = jnp.zeros((1, SSD_CONV - 1, SSD_XBC), F32)
            zh = jnp.zeros((1, SSD_HEADS, SSD_INNER // SSD_HEADS, SSD_N), F32)
            y, pc, ph = ssd_core(zx, bc, dt, zc, zh, prm, row0=0, streams=1, length=lp, chunk=128)
            y, sc, sh = ssd_core(zx, bc, dt, state_ssd_conv, state_ssd, prm, row0=lp, streams=sb,
                                 length=ls, chunk=ls, y_prev=y)
            outs['conv'], outs['ssd'] = (pc, sc), (ph, sh)
            h = matmul(y, w_out, name="mm_ssd_out")
        elif kind == 1:
            w_qkv, sinks, w_out = swa_p
            kvw = SWA_KVH * SWA_DH
            q = matmul(xb, w_qkv, col0=0, ncols=D, out_dtype=BF16, name="mm_swa_q")
            kv = matmul(xb, w_qkv, col0=D, ncols=2 * kvw, name="mm_swa_kv")
            o = swa_core(q, kv, None, None, sinks, row0=0, tiles=lp // 256, qt=256, prompt=True)
            ck = cache_k.reshape(sb, WINDOW, kvw)
            cv = cache_v.reshape(sb, WINDOW, kvw)
            o = swa_core(q, kv, ck, cv, sinks, row0=lp, tiles=sb, qt=ls, prompt=False, o_prev=o)
            kshape = (SWA_KVH, SWA_DH)
            pk = kv[lp - WINDOW:lp, :kvw].reshape((1, WINDOW) + kshape)
            pv = kv[lp - WINDOW:lp, kvw:].reshape((1, WINDOW) + kshape)
            kvs = kv[lp:].reshape(sb, ls, 2 * kvw)
            sk = jnp.concatenate([ck, kvs[:, :, :kvw]], axis=1)[:, -WINDOW:].reshape((sb, WINDOW) + kshape)
            sv = jnp.concatenate([cv, kvs[:, :, kvw:]], axis=1)[:, -WINDOW:].reshape((sb, WINDOW) + kshape)
            outs['k'], outs['v'] = (pk, sk), (pv, sv)
            h = matmul(o, w_out, name="mm_swa_out")
        elif kind == 2:
            w_in, w_glu = s5_p[0], s5_p[-1]
            tables = s5_tables(s5_p[1:-1])
            u = matmul(xb, w_in, name="mm_s5_in")
            zs = jnp.zeros((1, S5_G, S5_N), F32)
            y, pr, pi = s5_core(u, zs, zs, tables, row0=0, streams=1, length=lp)
            y, sr, si = s5_core(u, s5_re, s5_im, tables, row0=lp, streams=sb, length=ls, y_prev=y)
            outs['s5r'], outs['s5i'] = (pr, sr), (pi, si)
            h = glu_matmul(y, w_glu)
        else:
            w_in, lb_param, norm_w, w_out = hg_p
            qfig = matmul(xb, w_in, name="mm_hg_in")
            zs = jnp.zeros((1, HG_H, HG_K, HG_K), F32)
            o, ps = hgrn_core(qfig, lb_param, norm_w, zs, layer=layer, row0=0, streams=1, length=lp, chunk=128)
            o, ss = hgrn_core(qfig, lb_param, norm_w, state_hgrn, layer=layer, row0=lp, streams=sb,
                              length=ls, chunk=ls, o_prev=o)
            outs['hg'] = (ps, ss)
            h = matmul(o, w_out, name="mm_hg_out")
        wr = jnp.concatenate([w_rg[layer], w_re[layer], jnp.zeros((D, 128 - MOE_G - MOE_E), F32)], axis=1)
        br = jnp.concatenate([b_rg[layer], b_re[layer], jnp.zeros((128 - MOE_G - MOE_E,), F32)]).reshape(1, 128)
        x1, x1p, route, route_t, counts = ln_route(x, h, ln1_g[layer], ln1_b[layer], wr, br)
        y0, y1 = moe_layer(x1p, route_t, counts, w_gate, w_up, w_down, layer)
        x, xb = ln_combine(x1, y0, y1, route, ln2_g[layer], ln2_b[layer],
                           split=lp if layer == DEPTH - 1 else None)
    y_prompt = x.reshape(1, lp, D)
    y_sample = xb.reshape(sb, ls, D)
    order = ('conv', 'ssd', 'k', 'v', 's5r', 's5i', 'hg')
    return (y_prompt, y_sample) + tuple(outs[k][0] for k in order) + tuple(outs[k][1] for k in order)


def kernel(x_prompt, x_sample, state_ssd_conv, state_ssd, cache_swa_k, cache_swa_v, state_s5_re, state_s5_im, state_hgrn, ssd_w_in, ssd_conv_w, ssd_conv_b, ssd_dt_bias, ssd_a_log, ssd_d, ssd_norm_w, ssd_w_out, swa_w_qkv, swa_sinks, swa_w_out, s5_w_in, s5_a_re, s5_a_im, s5_log_dt, s5_b_re, s5_b_im, s5_c_re, s5_c_im, s5_d, s5_w_glu, hg_w_in, hg_lb, hg_norm_w, hg_w_out, ln1_g, ln1_b, ln2_g, ln2_b, moe_w_rg, moe_b_rg, moe_w_re, moe_b_re, moe_w_gate, moe_w_up, moe_w_down):
    states = (state_ssd_conv, state_ssd, cache_swa_k, cache_swa_v, state_s5_re, state_s5_im, state_hgrn)
    ssd_p = (ssd_w_in, ssd_conv_w, ssd_conv_b, ssd_dt_bias, ssd_a_log, ssd_d, ssd_norm_w, ssd_w_out)
    swa_p = (swa_w_qkv, swa_sinks, swa_w_out)
    s5_p = (s5_w_in, s5_a_re, s5_a_im, s5_log_dt, s5_b_re, s5_b_im, s5_c_re, s5_c_im, s5_d, s5_w_glu)
    hg_p = (hg_w_in, hg_lb, hg_norm_w, hg_w_out)
    ln_p = (ln1_g, ln1_b, ln2_g, ln2_b)
    moe_p = (moe_w_rg, moe_b_rg, moe_w_re, moe_b_re, moe_w_gate, moe_w_up, moe_w_down)
    return _forward(x_prompt, x_sample, states, ssd_p, swa_p, s5_p, hg_p, ln_p, moe_p)
```

```python
import functools

import jax
import jax.numpy as jnp
from jax import lax
from jax.experimental import pallas as pl
from jax.experimental.pallas import tpu as pltpu

F32 = jnp.float32
BF16 = jnp.bfloat16
HIGHEST = lax.Precision.HIGHEST

D = 2048
DEPTH = 4
DN_ALPHA = (2 * DEPTH) ** 0.25
LN_EPS = 1e-5
RMS_EPS = 1e-6
NEG = -1e30

VMEM_LIMIT = 56 * 1024 * 1024
MM_TN = 1024
MM_TM = 768

SSD_INNER = 4096
SSD_HEADS = 64
SSD_G = 8
SSD_HPG = 8
SSD_N = 128
SSD_GW = SSD_INNER // SSD_G
SSD_CONV = 4
SSD_XBC = SSD_INNER + 2 * SSD_G * SSD_N

SWA_DH = 64
SWA_QH = 32
SWA_KVH = 4
SWA_GRP = SWA_QH // SWA_KVH
WINDOW = 128
CHUNK = 64

S5_G = 128
S5_CH = 16
S5_N = 64
S5_SUB = 16
S5_GB = 8
S5_CW = S5_GB * S5_CH
S5_KR = 256

HG_H = 16
HG_K = 128
HG_LEAF = 16

MOE_G = 4
MOE_PG = 8
MOE_E = 32
D_EXPERT = 256
MOE_TM = 256


def _cparams(sem):
    return pltpu.CompilerParams(dimension_semantics=sem, vmem_limit_bytes=VMEM_LIMIT)


def _sigmoid(x):
    return 1.0 / (1.0 + jnp.exp(-x))


def _silu(x):
    return x * _sigmoid(x)


def _softplus(x):
    return jnp.maximum(x, 0.0) + jnp.log(1.0 + jnp.exp(-jnp.abs(x)))


def _dot(a, b, precision=None):
    return jnp.dot(a, b, preferred_element_type=F32, precision=precision)


def _dot_nt(a, b):
    return lax.dot_general(a, b, (((1,), (1,)), ((), ())), preferred_element_type=F32)


def _tri(n, upper=False):
    r = lax.broadcasted_iota(jnp.int32, (n, n), 0)
    c = lax.broadcasted_iota(jnp.int32, (n, n), 1)
    return (r <= c) if upper else (c <= r)


def _transpose_rows(x):
    c = x.shape[0]
    if c == 128:
        return x.T
    pad = jnp.zeros((128 - c, 128), x.dtype)
    return jnp.concatenate([x, pad], axis=0).T[:, :c]


def _pack_halves(x):
    n = x.shape[1] // 2
    lo = lax.bitcast_convert_type(x[:, :n].astype(BF16).astype(F32), jnp.uint32)
    hi = lax.bitcast_convert_type(x[:, n:].astype(BF16).astype(F32), jnp.uint32)
    return (lo >> 16) | hi


def _unpack_halves(w):
    lo = lax.bitcast_convert_type(w << 16, F32)
    hi = lax.bitcast_convert_type(w & jnp.uint32(0xFFFF0000), F32)
    return lo, hi


def _mm_body(x_ref, w_ref, o_ref, wb_ref, *, tiled_out):
    @pl.when(pl.program_id(1) == 0)
    def _():
        wb_ref[...] = w_ref[...].astype(BF16)

    r = _dot(x_ref[...], wb_ref[...]).astype(o_ref.dtype)
    if tiled_out:
        sub = o_ref.shape[2]
        for q in range(o_ref.shape[0]):
            o_ref[q] = r[:, sub * q:sub * (q + 1)]
    else:
        o_ref[...] = r


def _row_tile(m):
    return next(t for t in (MM_TM, 512, 256, 128, 64, 8) if m % t == 0)


def matmul(x, w, *, col0=0, ncols=None, tn=None, out_dtype=F32, tiled_out=False, sub=None, name="matmul"):
    m, k = x.shape
    ncols = w.shape[1] - col0 if ncols is None else ncols
    tm = _row_tile(m)
    if tn is None:
        tn = MM_TN if k * MM_TN * 4 <= 8 * 1024 * 1024 else MM_TN // 2
        tn = min(tn, ncols)
    assert col0 % tn == 0 and ncols % tn == 0 and m % tm == 0
    nj = ncols // tn
    j0 = col0 // tn
    if tiled_out:
        sub = tn if sub is None else sub
        out_shape = jax.ShapeDtypeStruct((ncols // sub, m, sub), out_dtype)
        out_spec = pl.BlockSpec((tn // sub, tm, sub), lambda j, i: (j, i, 0))
    else:
        out_shape = jax.ShapeDtypeStruct((m, ncols), out_dtype)
        out_spec = pl.BlockSpec((tm, tn), lambda j, i: (i, j))
    return pl.pallas_call(
        functools.partial(_mm_body, tiled_out=tiled_out),
        grid=(nj, m // tm),
        in_specs=[pl.BlockSpec((tm, k), lambda j, i: (i, 0)),
                  pl.BlockSpec((k, tn), lambda j, i: (0, j + j0))],
        out_specs=out_spec,
        out_shape=out_shape,
        scratch_shapes=[pltpu.VMEM((k, tn), BF16)],
        compiler_params=_cparams(("arbitrary", "arbitrary")),
        name=name,
    )(x, w)


def _glu_body(x_ref, wv_ref, wg_ref, o_ref, wvb_ref, wgb_ref):
    @pl.when(pl.program_id(1) == 0)
    def _():
        wvb_ref[...] = wv_ref[...].astype(BF16)
        wgb_ref[...] = wg_ref[...].astype(BF16)

    x = x_ref[...].astype(BF16)
    o_ref[...] = _dot(x, wvb_ref[...]) * _sigmoid(_dot(x, wgb_ref[...]))


def glu_matmul(x, w, *, tn=512):
    m, k = x.shape
    n = w.shape[1] // 2
    tm = _row_tile(m)
    nj = n // tn
    return pl.pallas_call(
        _glu_body,
        grid=(nj, m // tm),
        in_specs=[pl.BlockSpec((tm, k), lambda j, i: (i, 0)),
                  pl.BlockSpec((k, tn), lambda j, i: (0, j)),
                  pl.BlockSpec((k, tn), lambda j, i: (0, j + nj))],
        out_specs=pl.BlockSpec((tm, tn), lambda j, i: (i, j)),
        out_shape=jax.ShapeDtypeStruct((m, n), F32),
        scratch_shapes=[pltpu.VMEM((k, tn), BF16), pltpu.VMEM((k, tn), BF16)],
        compiler_params=_cparams(("arbitrary", "arbitrary")),
        name="glu_matmul",
    )(x, w, w)


def _layer_norm(v, g, b):
    mu = jnp.mean(v, axis=-1, keepdims=True)
    vc = v - mu
    var = jnp.mean(vc * vc, axis=-1, keepdims=True)
    return vc * lax.rsqrt(var + LN_EPS) * g + b


def _route(x, wh, wl, br, carry):
    xh = x.astype(BF16)
    xl = (x - xh.astype(F32)).astype(BF16)
    lg = _dot(xh, wh) + (_dot(xh, wl) + _dot(xl, wh)) + br
    tm = lg.shape[0]
    lane = lax.broadcasted_iota(jnp.int32, lg.shape, 1)
    lanef = lane.astype(F32)
    big = jnp.float32(1e9)
    is_g = lane < MOE_G
    gl = jnp.where(is_g, lg, NEG)
    gmax = jnp.max(gl, axis=-1, keepdims=True)
    gsel = jnp.min(jnp.where(is_g & (gl == gmax), lanef, big), axis=-1, keepdims=True)
    gprob = 1.0 / jnp.sum(jnp.where(is_g, jnp.exp(gl - gmax), 0.0), axis=-1, keepdims=True)
    lo = MOE_G + MOE_PG * gsel
    is_e = (lanef >= lo) & (lanef < lo + MOE_PG)
    el = jnp.where(is_e, lg, NEG)
    m1 = jnp.max(el, axis=-1, keepdims=True)
    l1 = jnp.min(jnp.where(is_e & (el == m1), lanef, big), axis=-1, keepdims=True)
    is_e2 = is_e & (lanef != l1)
    el2 = jnp.where(is_e2, lg, NEG)
    m2 = jnp.max(el2, axis=-1, keepdims=True)
    l2 = jnp.min(jnp.where(is_e2 & (el2 == m2), lanef, big), axis=-1, keepdims=True)
    r = jnp.exp(m2 - m1)
    w1 = gprob / (1.0 + r)
    w2 = gprob * r / (1.0 + r)
    hit1 = lanef == l1
    hit2 = lanef == l2
    oh = jnp.where(hit1 | hit2, 1.0, 0.0)
    rr = lax.broadcasted_iota(jnp.int32, (tm, tm), 0)
    cc = lax.broadcasted_iota(jnp.int32, (tm, tm), 1)
    before = _dot(jnp.where(cc < rr, 1.0, 0.0).astype(BF16), oh.astype(BF16)) + carry
    k1 = jnp.sum(jnp.where(hit1, before, 0.0), axis=-1, keepdims=True)
    k2 = jnp.sum(jnp.where(hit2, before, 0.0), axis=-1, keepdims=True)
    table = jnp.where(lane == 0, l1 - MOE_G,
                      jnp.where(lane == 1, l2 - MOE_G,
                                jnp.where(lane == 2, w1,
                                          jnp.where(lane == 3, w2,
                                                    jnp.where(lane == 4, k1, jnp.where(lane == 5, k2, 0.0))))))
    return table, carry + jnp.sum(oh, axis=0, keepdims=True)


def _ln_route_body(x_ref, h_ref, g_ref, b_ref, wh_ref, wl_ref, br_ref, o_ref, op_ref, r_ref, rt_ref, cnt_ref):
    @pl.when(pl.program_id(0) == 0)
    def _():
        cnt_ref[...] = jnp.zeros_like(cnt_ref)

    y = _layer_norm(DN_ALPHA * x_ref[...] + h_ref[...], g_ref[...], b_ref[...])
    o_ref[...] = y
    op_ref[...] = _pack_halves(y)
    table, cnt_ref[...] = _route(y, wh_ref[...], wl_ref[...], br_ref[...], cnt_ref[...])
    r_ref[...] = table
    rt_ref[...] = jnp.concatenate([table[128 * q:128 * (q + 1)].T[:8] for q in range(table.shape[0] // 128)],
                                  axis=1)


def ln_route(x, h, g, b, wr, br, tm=256):
    m = x.shape[0]
    row = pl.BlockSpec((tm, D), lambda i: (i, 0))
    vec = pl.BlockSpec((1, D), lambda i: (0, 0))
    one = pl.BlockSpec((1, 128), lambda i: (0, 0))
    wmat = pl.BlockSpec((D, 128), lambda i: (0, 0))
    wh = wr.astype(BF16)
    wl = (wr - wh.astype(F32)).astype(BF16)
    return pl.pallas_call(
        _ln_route_body,
        grid=(m // tm,),
        in_specs=[row, row, vec, vec, wmat, wmat, one],
        out_specs=[row, pl.BlockSpec((tm, D // 2), lambda i: (i, 0)), pl.BlockSpec((tm, 128), lambda i: (i, 0)),
                   pl.BlockSpec((8, tm), lambda i: (0, i)), one],
        out_shape=[jax.ShapeDtypeStruct((m, D), F32), jax.ShapeDtypeStruct((m, D // 2), jnp.uint32),
                   jax.ShapeDtypeStruct((m, 128), F32), jax.ShapeDtypeStruct((8, m), F32),
                   jax.ShapeDtypeStruct((1, 128), F32)],
        compiler_params=_cparams(("arbitrary",)),
        name="ln_route",
    )(x, h, g.reshape(1, D), b.reshape(1, D), wh, wl, br)


def _ln_combine_body(x_ref, y0_ref, y1_ref, r_ref, g_ref, b_ref, o_ref, ob_ref, *, split_tiles):
    r = r_ref[...]
    a_lo, a_hi = _unpack_halves(y0_ref[...])
    b_lo, b_hi = _unpack_halves(y1_ref[...])
    w0, w1 = r[:, 2:3], r[:, 3:4]
    f = jnp.concatenate([w0 * a_lo + w1 * b_lo, w0 * a_hi + w1 * b_hi], axis=1)
    y = _layer_norm(DN_ALPHA * x_ref[...] + f, g_ref[...], b_ref[...])
    if split_tiles is None:
        o_ref[...] = y
        ob_ref[...] = y.astype(BF16)
    else:
        @pl.when(pl.program_id(0) < split_tiles)
        def _():
            o_ref[...] = y

        @pl.when(pl.program_id(0) >= split_tiles)
        def _():
            ob_ref[...] = y


def ln_combine(x, y0, y1, route, g, b, tm=256, split=None):
    m = x.shape[0]
    row = pl.BlockSpec((tm, D), lambda i: (i, 0))
    half = pl.BlockSpec((tm, D // 2), lambda i: (i, 0))
    vec = pl.BlockSpec((1, D), lambda i: (0, 0))
    if split is None:
        st = None
        out_specs = [row, row]
        out_shape = [jax.ShapeDtypeStruct((m, D), F32), jax.ShapeDtypeStruct((m, D), BF16)]
    else:
        st = split // tm
        out_specs = [pl.BlockSpec((tm, D), lambda i: (jnp.minimum(i, st - 1), 0)),
                     pl.BlockSpec((tm, D), lambda i: (jnp.maximum(i - st, 0), 0))]
        out_shape = [jax.ShapeDtypeStruct((split, D), F32), jax.ShapeDtypeStruct((m - split, D), F32)]
    return pl.pallas_call(
        functools.partial(_ln_combine_body, split_tiles=st),
        grid=(m // tm,),
        in_specs=[row, half, half, pl.BlockSpec((tm, 128), lambda i: (i, 0)), vec, vec],
        out_specs=out_specs,
        out_shape=out_shape,
        compiler_params=_cparams(("arbitrary",)),
        name="ln_combine",
    )(x, y0, y1, route, g.reshape(1, D), b.reshape(1, D))


def _ssd_body(z_ref, x_ref, b_ref, c_ref, dtg_ref, dtt_ref, csx_ref, csb_ref, csc_ref, h0_ref,
              cwx_ref, cwb_ref, cwc_ref, cbx_ref, cbb_ref, cbc_ref, dtbr_ref, dtbc_ref,
              alr_ref, alc_ref, dsk_ref, nw_ref, *rest, C, aliased):
    if aliased:
        rest = rest[1:]
    (y_ref, cox_ref, cob_ref, coc_ref, ho_ref,
     tx_ref, tb_ref, tc_ref, ex_ref, eb_ref, ec_ref, ht_ref, yb_ref) = rest
    c = pl.program_id(1)
    last_chunk = c == pl.num_programs(1) - 1

    @pl.when(c == 0)
    def _init():
        tx_ref[...] = jnp.zeros_like(tx_ref)
        tb_ref[...] = jnp.zeros_like(tb_ref)
        tc_ref[...] = jnp.zeros_like(tc_ref)
        for g in range(SSD_G):
            tx_ref[g, 5:8, :] = csx_ref[0, g]
            tb_ref[g, 5:8, :] = csb_ref[0, g]
            tc_ref[g, 5:8, :] = csc_ref[0, g]
        for p in range(SSD_HEADS // 2):
            ht_ref[p] = h0_ref[0, p].T

    tril = _tri(C)
    tril_f = tril.astype(F32)
    triu_f = _tri(C, upper=True).astype(F32)
    lane = lax.broadcasted_iota(jnp.int32, (C, 128), 1)
    left = lane < 64

    def conv(e_ref, t_ref, raw, w_ref, bias_ref, g):
        e_ref[0:8, :] = t_ref[g]
        e_ref[8:8 + C, :] = raw
        w = w_ref[g]
        acc = bias_ref[g] + w[3:4, :] * raw
        for k in range(SSD_CONV - 1):
            acc = acc + w[k:k + 1, :] * e_ref[5 + k:5 + k + C, :]
        t_ref[g] = e_ref[C:C + 8, :]
        return _silu(acc)

    def group(g, carry):
        xs = conv(ex_ref, tx_ref, x_ref[g], cwx_ref, cbx_ref, g)
        bs = conv(eb_ref, tb_ref, b_ref[g], cwb_ref, cbb_ref, g)
        cs = conv(ec_ref, tc_ref, c_ref[g], cwc_ref, cbc_ref, g)

        @pl.when(last_chunk)
        def _():
            cox_ref[0, g] = ex_ref[C + 5:C + 8, :]
            cob_ref[0, g] = eb_ref[C + 5:C + 8, :]
            coc_ref[0, g] = ec_ref[C + 5:C + 8, :]

        dtv = _softplus(dtg_ref[0, g] + dtbr_ref[g])
        dtvt = _softplus(dtt_ref[0, g] + dtbc_ref[g])
        cum = _dot(tril_f, dtv * (-jnp.exp(alr_ref[g])), precision=HIGHEST)
        cumt = _dot(dtvt * (-jnp.exp(alc_ref[g])), triu_f, precision=HIGHEST)
        bsb = bs.astype(BF16)
        csb = cs.astype(BF16)
        cb = _dot_nt(csb, bsb)
        bst = _transpose_rows(bs).astype(BF16)
        dsk = dsk_ref[g]
        ys = []
        for j in range(SSD_HPG // 2):
            h0, h1 = 2 * j, 2 * j + 1
            c0, c1 = cum[:, h0:h0 + 1], cum[:, h1:h1 + 1]
            l0 = jnp.where(tril, jnp.exp(c0 - cumt[h0:h0 + 1, :]), 0.0) * cb
            l1 = jnp.where(tril, jnp.exp(c1 - cumt[h1:h1 + 1, :]), 0.0) * cb
            lhs = jnp.concatenate([l0, l1], axis=1).astype(BF16)
            xp = xs[:, 128 * j:128 * (j + 1)]
            xdt = xp * jnp.where(left, dtv[:, h0:h0 + 1], dtv[:, h1:h1 + 1])
            rhs = jnp.concatenate([jnp.where(left, xdt, 0.0), jnp.where(left, 0.0, xdt)],
                                  axis=0).astype(BF16)
            htp = ht_ref[g * 4 + j]
            y = _dot(lhs, rhs)
            y = y + _dot(csb, htp.astype(BF16)) * jnp.where(left, jnp.exp(c0), jnp.exp(c1))
            e0, e1 = cum[C - 1:C, h0:h0 + 1], cum[C - 1:C, h1:h1 + 1]
            wgt = (xdt * jnp.where(left, jnp.exp(e0 - c0), jnp.exp(e1 - c1))).astype(BF16)
            ht_ref[g * 4 + j] = jnp.where(left[0:1, :], jnp.exp(e0), jnp.exp(e1)) * htp + _dot(bst, wgt)
            ys.append(y + dsk[:, 128 * j:128 * (j + 1)] * xp)
        y = jnp.concatenate(ys, axis=1) * _silu(z_ref[g])
        ms = jnp.mean(y * y, axis=-1, keepdims=True)
        yb_ref[g] = (y * lax.rsqrt(ms + RMS_EPS) * nw_ref[g]).astype(BF16)
        return carry

    lax.fori_loop(0, SSD_G, group, 0)
    for g in range(SSD_G):
        y_ref[:, SSD_GW * g:SSD_GW * (g + 1)] = yb_ref[g]

    @pl.when(last_chunk)
    def _fin():
        for p in range(SSD_HEADS // 2):
            ho_ref[0, p] = ht_ref[p].T


def ssd_core(zx, bc, dt, conv_state, h0, params, *, row0, streams, length, chunk, y_prev=None):
    conv_w, conv_b, dt_bias, a_log, d_skip, norm_w = params
    S, L, C = streams, length, chunk
    nch = L // C
    rb0 = row0 // C
    t_all = zx.shape[1]
    dseg = dt[row0:row0 + S * L].reshape(S, L, SSD_G, SSD_HPG)
    dtg = dseg.transpose(0, 2, 1, 3)
    dtt = dseg.transpose(0, 2, 3, 1)

    def split(a, lead):
        ax = a[..., :SSD_INNER].reshape(lead + (SSD_G, SSD_GW))
        ab = a[..., SSD_INNER:SSD_INNER + SSD_G * SSD_N].reshape(lead + (SSD_G, SSD_N))
        ac = a[..., SSD_INNER + SSD_G * SSD_N:].reshape(lead + (SSD_G, SSD_N))
        return ax, ab, ac

    csx, csb, csc = (jnp.moveaxis(a, 2, 1) for a in split(conv_state, (S, SSD_CONV - 1)))
    cwx, cwb, cwc = (jnp.moveaxis(a, 1, 0) for a in split(conv_w, (SSD_CONV,)))
    cbx, cbb, cbc = (jnp.moveaxis(a, 1, 0) for a in split(conv_b.reshape(1, -1), (1,)))
    dtbr = dt_bias.reshape(SSD_G, 1, SSD_HPG)
    dtbc = dt_bias.reshape(SSD_G, SSD_HPG, 1)
    alr = a_log.reshape(SSD_G, 1, SSD_HPG)
    alc = a_log.reshape(SSD_G, SSD_HPG, 1)
    dsk = jnp.repeat(d_skip, SSD_INNER // SSD_HEADS).reshape(SSD_G, 1, SSD_GW)
    nw = norm_w.reshape(SSD_G, 1, SSD_GW)
    h0p = h0.reshape(S, SSD_HEADS // 2, 128, SSD_N)

    def rb(s, c):
        return rb0 + s * nch + c

    def full(a):
        nd = a.ndim
        return pl.BlockSpec(a.shape, lambda s, c: (0,) * nd)

    def per_stream(a):
        nd = a.ndim
        return pl.BlockSpec((1,) + a.shape[1:], lambda s, c: (s,) + (0,) * (nd - 1))

    in_specs = [
        pl.BlockSpec((SSD_G, C, SSD_GW), lambda s, c: (0, rb(s, c), 0)),
        pl.BlockSpec((SSD_G, C, SSD_GW), lambda s, c: (1, rb(s, c), 0)),
        pl.BlockSpec((SSD_G, C, SSD_N), lambda s, c: (0, rb(s, c), 0)),
        pl.BlockSpec((SSD_G, C, SSD_N), lambda s, c: (1, rb(s, c), 0)),
        pl.BlockSpec((1, SSD_G, C, SSD_HPG), lambda s, c: (s, 0, c, 0)),
        pl.BlockSpec((1, SSD_G, SSD_HPG, C), lambda s, c: (s, 0, 0, c)),
        per_stream(csx), per_stream(csb), per_stream(csc), per_stream(h0p),
        full(cwx), full(cwb), full(cwc), full(cbx), full(cbb), full(cbc),
        full(dtbr), full(dtbc), full(alr), full(alc), full(dsk), full(nw),
    ]
    args = [zx, zx, bc, bc, dtg, dtt, csx, csb, csc, h0p, cwx, cwb, cwc, cbx, cbb, cbc,
            dtbr, dtbc, alr, alc, dsk, nw]
    aliases = {}
    if y_prev is not None:
        in_specs.append(pl.BlockSpec(memory_space=pl.ANY))
        args.append(y_prev)
        aliases = {len(args) - 1: 0}
    out_shape = [
        jax.ShapeDtypeStruct((t_all, SSD_INNER), BF16),
        jax.ShapeDtypeStruct(csx.shape, F32), jax.ShapeDtypeStruct(csb.shape, F32),
        jax.ShapeDtypeStruct(csc.shape, F32), jax.ShapeDtypeStruct(h0p.shape, F32),
    ]
    out_specs = [
        pl.BlockSpec((C, SSD_INNER), lambda s, c: (rb(s, c), 0)),
        per_stream(csx), per_stream(csb), per_stream(csc), per_stream(h0p),
    ]
    scratch = [
        pltpu.VMEM((SSD_G, 8, SSD_GW), F32), pltpu.VMEM((SSD_G, 8, SSD_N), F32),
        pltpu.VMEM((SSD_G, 8, SSD_N), F32),
        pltpu.VMEM((C + 8, SSD_GW), F32), pltpu.VMEM((C + 8, SSD_N), F32), pltpu.VMEM((C + 8, SSD_N), F32),
        pltpu.VMEM((SSD_HEADS // 2, SSD_N, 128), F32),
        pltpu.VMEM((SSD_G, C, SSD_GW), BF16),
    ]
    y, cox, cob, coc, ho = pl.pallas_call(
        functools.partial(_ssd_body, C=C, aliased=y_prev is not None),
        grid=(S, nch), in_specs=in_specs, out_specs=out_specs, out_shape=out_shape,
        scratch_shapes=scratch, input_output_aliases=aliases,
        compiler_params=_cparams(("arbitrary", "arbitrary")),
        name="ssd_core",
    )(*args)
    conv_out = jnp.concatenate([jnp.moveaxis(a, 1, 2).reshape(S, SSD_CONV - 1, -1) for a in (cox, cob, coc)],
                               axis=-1)
    return y, conv_out, ho.reshape(S, SSD_HEADS, SSD_INNER // SSD_HEADS, SSD_N)


def _swa_body(sink_ref, q_ref, pk_ref, pv_ref, kv_ref, *rest, QT, prev_valid, aliased):
    o_ref = rest[-1]
    i = pl.program_id(0)
    kvw = SWA_KVH * SWA_DH
    nback = WINDOW // CHUNK
    rb = min(QT, 2 * CHUNK)
    span = rb + WINDOW
    kf = jnp.concatenate([pk_ref[...], kv_ref[:, :kvw]], axis=0).astype(BF16)
    vf = jnp.concatenate([pv_ref[...], kv_ref[:, kvw:]], axis=0).astype(BF16)
    r2 = lax.broadcasted_iota(jnp.int32, (128, 128), 0)
    c2 = lax.broadcasted_iota(jnp.int32, (128, 128), 1)
    swap = jnp.where((r2 + SWA_DH) % 128 == c2, 1.0, 0.0).astype(BF16)
    left = lax.broadcasted_iota(jnp.int32, (1, 128), 1) < SWA_DH
    k_side, v_side = [], []
    for kh in range(SWA_KVH):
        blk = slice(128 * (kh // 2), 128 * (kh // 2 + 1))
        mine = left if kh % 2 == 0 else ~left
        kb, vb = kf[:, blk], jnp.where(mine, vf[:, blk], 0.0).astype(BF16)
        ko, vo = _dot(kb, swap).astype(BF16), _dot(vb, swap).astype(BF16)
        k_side.append((kb, ko) if kh % 2 == 0 else (ko, kb))
        v_side.append((vb, vo) if kh % 2 == 0 else (vo, vb))
    scale = SWA_DH ** -0.5
    for b in range(QT // rb):
        rows = slice(rb * b, rb * (b + 1))
        keys = slice(rb * b, rb * b + span)
        qc = lax.broadcasted_iota(jnp.int32, (rb, span), 0) // CHUNK
        kc = lax.broadcasted_iota(jnp.int32, (rb, span), 1) // CHUNK
        ok = (kc >= qc) & (kc <= qc + nback)
        if not prev_valid and rb * b < WINDOW:
            ok = ok & ((kc + (rb // CHUNK) * b >= nback) | (i > 0))
        for pr in range(SWA_QH // 2):
            kh = 2 * pr // SWA_GRP
            q2 = q_ref[rows, 128 * pr:128 * (pr + 1)] * scale
            acc = None
            for side in range(2):
                qm = jnp.where(left if side == 0 else ~left, q2, 0.0).astype(BF16)
                s = jnp.where(ok, _dot_nt(qm, k_side[kh][side][keys]), NEG)
                sink = sink_ref[2 * pr + side]
                m = jnp.maximum(jnp.max(s, axis=-1, keepdims=True), sink)
                p = jnp.exp(s - m)
                den = jnp.sum(p, axis=-1, keepdims=True) + jnp.exp(sink - m)
                o = _dot(p.astype(BF16), v_side[kh][side][keys]) / den
                acc = o if acc is None else acc + o
            o_ref[rows, 128 * pr:128 * (pr + 1)] = acc.astype(BF16)


def swa_core(q, kv, prev_k, prev_v, sinks, *, row0, tiles, qt, prompt, o_prev=None):
    t_all = q.shape[0]
    rb0 = row0 // qt
    kvw = SWA_KVH * SWA_DH
    if prompt:
        wpt = qt // WINDOW
        prev_map_k = lambda i, s: (jnp.maximum(wpt * (rb0 + i) - 1, 0), 0)
        prev_map_v = lambda i, s: (jnp.maximum(wpt * (rb0 + i) - 1, 0), 1)
        pk_spec = pl.BlockSpec((WINDOW, kvw), prev_map_k)
        pv_spec = pl.BlockSpec((WINDOW, kvw), prev_map_v)
        prev_k = prev_v = kv
    else:
        pk_spec = pl.BlockSpec((None, WINDOW, kvw), lambda i, s: (i, 0, 0))
        pv_spec = pl.BlockSpec((None, WINDOW, kvw), lambda i, s: (i, 0, 0))
    in_specs = [pl.BlockSpec((qt, D), lambda i, s: (rb0 + i, 0)), pk_spec, pv_spec,
                pl.BlockSpec((qt, 2 * kvw), lambda i, s: (rb0 + i, 0))]
    args = [sinks, q, prev_k, prev_v, kv]
    aliases = {}
    if o_prev is not None:
        in_specs.append(pl.BlockSpec(memory_space=pl.ANY))
        args.append(o_prev)
        aliases = {len(args) - 1: 0}
    return pl.pallas_call(
        functools.partial(_swa_body, QT=qt, prev_valid=not prompt, aliased=o_prev is not None),
        grid_spec=pltpu.PrefetchScalarGridSpec(
            num_scalar_prefetch=1, grid=(tiles,), in_specs=in_specs,
            out_specs=pl.BlockSpec((qt, D), lambda i, s: (rb0 + i, 0))),
        out_shape=jax.ShapeDtypeStruct((t_all, D), BF16),
        input_output_aliases=aliases,
        compiler_params=_cparams(("arbitrary",)),
        name="swa_core",
    )(*args)


def _gelu_tanh(y):
    return 0.5 * y * (1.0 + jnp.tanh(0.7978845608028654 * (y + 0.044715 * y * y * y)))


def _s5_body(u_ref, wxr_ref, wxi_ref, wyr_ref, wyi_ref, kt_ref, a_ref, dsk_ref, hr0_ref, hi0_ref, *rest,
             S, R, aliased):
    if aliased:
        rest = rest[1:]
    (y_ref, hro_ref, hio_ref, wx_s, wy_s, ktm_s, xr_s, xi_s, pr_s, pi_s, hr_s, hi_s) = rest
    kb = pl.program_id(1)
    half = S5_CW * S5_N // S5_CH

    @pl.when((pl.program_id(0) == 0) & (kb == 0))
    def _zero():
        ktm_s[...] = jnp.zeros_like(ktm_s)

    @pl.when(kb == 0)
    def _build():
        own = (lax.broadcasted_iota(jnp.int32, (S5_CW, half), 0) // S5_CH
               == lax.broadcasted_iota(jnp.int32, (S5_CW, half), 1) // S5_N)
        for s in range(S5_SUB):
            rows = slice(S5_CW * s, S5_CW * (s + 1))
            wx_s[rows, 0:half] = jnp.where(own, jnp.concatenate([wxr_ref[s]] * 4, axis=1), 0.0).astype(BF16)
            wx_s[rows, half:2 * half] = jnp.where(own, jnp.concatenate([wxi_ref[s]] * 4, axis=1), 0.0).astype(BF16)
        own_t = (lax.broadcasted_iota(jnp.int32, (half, S5_CW), 0) // S5_N
                 == lax.broadcasted_iota(jnp.int32, (half, S5_CW), 1) // S5_CH)
        for t in range(S5_SUB):
            cols = slice(S5_CW * t, S5_CW * (t + 1))
            wy_s[0:half, cols] = jnp.where(own_t, jnp.concatenate([wyr_ref[t]] * 8, axis=0), 0.0).astype(BF16)
            wy_s[half:2 * half, cols] = jnp.where(own_t, jnp.concatenate([wyi_ref[t]] * 8, axis=0), 0.0).astype(BF16)
        same = (lax.broadcasted_iota(jnp.int32, (S5_CW, S5_CW), 0) // S5_CH
                == lax.broadcasted_iota(jnp.int32, (S5_CW, S5_CW), 1) // S5_CH)
        taps = [jnp.where(same, kt_ref[tau], 0.0).astype(BF16) for tau in range(S5_SUB)]
        for s in range(S5_SUB):
            for t in range(s, S5_SUB):
                ktm_s[S5_CW * s:S5_CW * (s + 1), S5_CW * t:S5_CW * (t + 1)] = taps[t - s]
        for st in range(S):
            hr_s[st] = hr0_ref[st]
            hi_s[st] = hi0_ref[st]

    kr = S * R
    ucat = jnp.concatenate([u_ref[pl.ds(s, kr, stride=S5_SUB), :] for s in range(S5_SUB)],
                           axis=1)
    ub = ucat.astype(BF16)
    x = _dot(ub, wx_s[...])
    xr_s[...] = x[:, :half]
    xi_s[...] = x[:, half:]
    ar = a_ref[0:1, :]
    ai = a_ref[1:2, :]
    for st in range(S):
        def step(k, carry):
            hr, hi = carry
            row = st * R + k
            pr_s[pl.ds(row, 1), :] = hr
            pi_s[pl.ds(row, 1), :] = hi
            nr = ar * hr - ai * hi + xr_s[pl.ds(row, 1), :]
            ni = ar * hi + ai * hr + xi_s[pl.ds(row, 1), :]
            return nr, ni

        hr, hi = lax.fori_loop(0, R, step, (hr_s[st], hi_s[st]))
        hr_s[st] = hr
        hi_s[st] = hi
    hprev = jnp.concatenate([pr_s[...], pi_s[...]], axis=1).astype(BF16)
    dsk = jnp.concatenate([dsk_ref[...]] * S5_SUB, axis=1)
    cb = 2 * S5_CW
    intra = jnp.concatenate([_dot(ub[:, :cb * (t + 1)], ktm_s[0:cb * (t + 1), cb * t:cb * (t + 1)])
                             for t in range(S5_SUB // 2)], axis=1)
    y = _gelu_tanh(intra + _dot(hprev, wy_s[...]) + dsk * ucat)
    for t in range(S5_SUB):
        y_ref[pl.ds(t, kr, stride=S5_SUB), :] = y[:, S5_CW * t:S5_CW * (t + 1)]

    @pl.when(kb == pl.num_programs(1) - 1)
    def _fin():
        for st in range(S):
            hro_ref[st] = hr_s[st]
            hio_ref[st] = hi_s[st]


def s5_tables(p):
    a_re, a_im, log_dt, b_re, b_im, c_re, c_im, d_skip = p
    lr, li = a_re.astype(F32), a_im.astype(F32)
    dt = jnp.exp(log_dt.astype(F32))[:, None]
    mag = jnp.exp(lr * dt)
    ab_r, ab_i = mag * jnp.cos(li * dt), mag * jnp.sin(li * dt)
    den = lr * lr + li * li
    co_r = ((ab_r - 1.0) * lr + ab_i * li) / den
    co_i = (ab_i * lr - (ab_r - 1.0) * li) / den
    bb_r = co_r[..., None] * b_re - co_i[..., None] * b_im
    bb_i = co_r[..., None] * b_im + co_i[..., None] * b_re
    pw_r, pw_i = [jnp.ones_like(ab_r)], [jnp.zeros_like(ab_i)]
    for _ in range(S5_SUB):
        r, i = pw_r[-1], pw_i[-1]
        pw_r.append(ab_r * r - ab_i * i)
        pw_i.append(ab_r * i + ab_i * r)
    pr, pi = jnp.stack(pw_r, 0), jnp.stack(pw_i, 0)
    er, ei = pr[S5_SUB - 1::-1][:S5_SUB], pi[S5_SUB - 1::-1][:S5_SUB]
    wx_r = er[..., None] * bb_r[None] - ei[..., None] * bb_i[None]
    wx_i = er[..., None] * bb_i[None] + ei[..., None] * bb_r[None]
    gb, gw = S5_G // S5_GB, S5_GB
    wx_r = wx_r.reshape(S5_SUB, gb, gw, S5_N, S5_CH).transpose(1, 0, 2, 4, 3).reshape(gb, S5_SUB, S5_CW, S5_N)
    wx_i = wx_i.reshape(S5_SUB, gb, gw, S5_N, S5_CH).transpose(1, 0, 2, 4, 3).reshape(gb, S5_SUB, S5_CW, S5_N)
    wxr = jnp.concatenate([wx_r, wx_r], axis=-1)
    wxi = jnp.concatenate([wx_i, wx_i], axis=-1)
    qr, qi = pr[1:], pi[1:]
    cr, ci = c_re.astype(F32), c_im.astype(F32)
    wy_r = cr[None] * qr[:, :, None, :] - ci[None] * qi[:, :, None, :]
    wy_i = -(cr[None] * qi[:, :, None, :] + ci[None] * qr[:, :, None, :])
    wyr = wy_r.reshape(S5_SUB, gb, gw, S5_CH, S5_N).transpose(1, 0, 4, 2, 3).reshape(gb, S5_SUB, S5_N, S5_CW)
    wyi = wy_i.reshape(S5_SUB, gb, gw, S5_CH, S5_N).transpose(1, 0, 4, 2, 3).reshape(gb, S5_SUB, S5_N, S5_CW)
    tr = pr[:S5_SUB, :, None, :] * cr[None] - pi[:S5_SUB, :, None, :] * ci[None]
    ti = pr[:S5_SUB, :, None, :] * ci[None] + pi[:S5_SUB, :, None, :] * cr[None]
    taps = jnp.einsum('agjn,gnk->agjk', tr, bb_r) - jnp.einsum('agjn,gnk->agjk', ti, bb_i)
    taps = taps.reshape(S5_SUB, gb, gw, S5_CH, S5_CH).transpose(1, 0, 2, 4, 3)
    kt = jnp.tile(taps.reshape(gb, S5_SUB, S5_CW, S5_CH), (1, 1, 1, gw))
    a16 = jnp.stack([pr[S5_SUB].reshape(gb, gw * S5_N), pi[S5_SUB].reshape(gb, gw * S5_N)], axis=1)
    dsk = d_skip.astype(F32).reshape(gb, 1, S5_CW)
    return wxr, wxi, wyr, wyi, kt, a16, dsk


def s5_core(u, s_re, s_im, tables, *, row0, streams, length, y_prev=None):
    wxr, wxi, wyr, wyi, kt, a16, dsk = tables
    S, L = streams, length
    gb = S5_G // S5_GB
    rows = S * L // S5_SUB
    if S == 1:
        kr = min(S5_KR, rows)
        spb, rps = 1, kr
    else:
        kr = rows
        spb, rps = S, L // S5_SUB
    nkb = rows // kr
    kb0 = row0 // S5_SUB // kr
    half = S5_GB * S5_N
    hr0 = s_re.reshape(S, gb, 1, half)
    hi0 = s_im.reshape(S, gb, 1, half)
    tab = lambda a: pl.BlockSpec((None,) + a.shape[1:], lambda p, k: (p,) + (0,) * (a.ndim - 1))
    st = pl.BlockSpec((S, None, 1, half), lambda p, k: (0, p, 0, 0))
    uspec = pl.BlockSpec((kr * S5_SUB, S5_CW), lambda p, k: (kb0 + k, p))
    in_specs = [uspec, tab(wxr), tab(wxi), tab(wyr), tab(wyi), tab(kt), tab(a16), tab(dsk), st, st]
    args = [u, wxr, wxi, wyr, wyi, kt, a16, dsk, hr0, hi0]
    aliases = {}
    if y_prev is not None:
        in_specs.append(pl.BlockSpec(memory_space=pl.ANY))
        args.append(y_prev)
        aliases = {len(args) - 1: 0}
    wide = S5_SUB * S5_CW
    y, hro, hio = pl.pallas_call(
        functools.partial(_s5_body, S=spb, R=rps, aliased=y_prev is not None),
        grid=(gb, nkb),
        in_specs=in_specs,
        out_specs=[uspec, st, st],
        out_shape=[jax.ShapeDtypeStruct(u.shape, F32), jax.ShapeDtypeStruct(hr0.shape, F32),
                   jax.ShapeDtypeStruct(hi0.shape, F32)],
        scratch_shapes=[pltpu.VMEM((wide, 2 * half), BF16), pltpu.VMEM((2 * half, wide), BF16),
                        pltpu.VMEM((wide, wide), BF16)]
        + [pltpu.VMEM((kr, half), F32) for _ in range(4)]
        + [pltpu.VMEM((spb, 1, half), F32) for _ in range(2)],
        input_output_aliases=aliases,
        compiler_params=_cparams(("arbitrary", "arbitrary")),
        name="s5_core",
    )(*args)
    return y, hro.reshape(S, S5_G, S5_N), hio.reshape(S, S5_G, S5_N)


def _hgrn_body(q_ref, f_ref, i_ref, g_ref, lbp_ref, nw_ref, s0_ref, *rest, C, layer, aliased):
    if aliased:
        rest = rest[1:]
    o_ref, so_ref, st_ref = rest
    c = pl.program_id(1)

    @pl.when(c == 0)
    def _init():
        for h in range(HG_H):
            st_ref[h] = s0_ref[0, h].T

    lbp = lbp_ref[...]
    e = jnp.exp(lbp - jnp.max(lbp, axis=0, keepdims=True))
    lbs = e / jnp.sum(e, axis=0, keepdims=True)
    lb = jnp.zeros((1, D), F32)
    for r in range(1, layer + 1):
        lb = lb + lbs[r:r + 1, :]
    fz = f_ref[...]
    log_sig = jnp.minimum(fz, 0.0) - jnp.log(1.0 + jnp.exp(-jnp.abs(fz)))
    la = jnp.log(lb)
    lbb = jnp.log(1.0 - lb) + log_sig
    mx = jnp.maximum(la, lbb)
    logf = mx + jnp.log(1.0 + jnp.exp(-jnp.abs(la - lbb)))
    kk = 1.0 - jnp.exp(logf)
    qs = _silu(q_ref[...])
    tril = _tri(C)
    cum = _dot(tril.astype(F32), logf, precision=HIGHEST)
    row = lax.broadcasted_iota(jnp.int32, (C, C), 0)
    col = lax.broadcasted_iota(jnp.int32, (C, C), 1)

    levels = []
    b = C
    while b >= HG_LEAF:
        nb = C // b
        ref = jnp.broadcast_to(cum.reshape(nb, b, D)[:, b // 2 - 1:b // 2, :], (nb, b, D)).reshape(C, D)
        ex = cum - ref
        same = (row // b) == (col // b)
        if b == HG_LEAF:
            mask = same & (col <= row)
            qe, ke = jnp.exp(jnp.minimum(ex, 80.0)), jnp.exp(jnp.minimum(-ex, 80.0))
        else:
            mask = same & ((row % b) >= b // 2) & ((col % b) < b // 2)
            e = jnp.exp(-jnp.abs(ex))
            qe, ke = jnp.where(ex <= 0.0, e, 1.0), jnp.where(ex <= 0.0, 1.0, e)
        levels.append((mask, (qs * qe).astype(BF16), (kk * ke).astype(BF16)))
        b //= 2

    last = cum[C - 1:C, :]
    qin = (qs * jnp.exp(cum)).astype(BF16)
    kin = (kk * jnp.exp(last - cum)).astype(BF16)
    dec = jnp.exp(last)
    vv = i_ref[...]
    vb = vv.astype(BF16)
    gate = _silu(g_ref[...])
    nw = nw_ref[...]
    outs = []
    for h in range(HG_H):
        sl = slice(HG_K * h, HG_K * (h + 1))
        att = jnp.zeros((C, C), F32)
        for mask, ql, kl in levels:
            att = att + jnp.where(mask, _dot_nt(ql[:, sl], kl[:, sl]), 0.0)
        st = st_ref[h]
        o = _dot(att.astype(BF16), vb[:, sl]) + _dot_nt(qin[:, sl], st.astype(BF16))
        st_ref[h] = st * dec[:, sl] + _dot(_transpose_rows(vv[:, sl]).astype(BF16), kin[:, sl])
        ms = jnp.mean(o * o, axis=-1, keepdims=True)
        outs.append(o * lax.rsqrt(ms + RMS_EPS) * nw)
    o_ref[...] = (jnp.concatenate(outs, axis=1) * gate).astype(BF16)

    @pl.when(c == pl.num_programs(1) - 1)
    def _fin():
        for h in range(HG_H):
            so_ref[0, h] = st_ref[h].T


def hgrn_core(qfig, lb_param, norm_w, s0, *, layer, row0, streams, length, chunk, o_prev=None):
    S, L, C = streams, length, chunk
    nch = L // C
    rb0 = row0 // C
    t_all = qfig.shape[0]
    rowspec = lambda j: pl.BlockSpec((C, D), lambda s, c: (rb0 + s * nch + c, j))
    in_specs = [rowspec(0), rowspec(1), rowspec(2), rowspec(3),
                pl.BlockSpec((DEPTH, D), lambda s, c: (0, 0)),
                pl.BlockSpec((1, HG_K), lambda s, c: (0, 0)),
                pl.BlockSpec((1, HG_H, HG_K, HG_K), lambda s, c: (s, 0, 0, 0))]
    args = [qfig, qfig, qfig, qfig, lb_param, norm_w.reshape(1, HG_K), s0]
    aliases = {}
    if o_prev is not None:
        in_specs.append(pl.BlockSpec(memory_space=pl.ANY))
        args.append(o_prev)
        aliases = {len(args) - 1: 0}
    return pl.pallas_call(
        functools.partial(_hgrn_body, C=C, layer=layer, aliased=o_prev is not None),
        grid=(S, nch), in_specs=in_specs,
        out_specs=[rowspec(0), pl.BlockSpec((1, HG_H, HG_K, HG_K), lambda s, c: (s, 0, 0, 0))],
        out_shape=[jax.ShapeDtypeStruct((t_all, D), BF16), jax.ShapeDtypeStruct(s0.shape, F32)],
        scratch_shapes=[pltpu.VMEM((HG_H, HG_K, HG_K), F32)],
        input_output_aliases=aliases,
        compiler_params=_cparams(("arbitrary", "arbitrary")),
        name="hgrn_core",
    )(*args)


def _expert_body(te_ref, nt_ref, x_ref, wg_ref, wu_ref, wd_ref, o_ref, wgb_ref, wub_ref, wdb_ref):
    i = pl.program_id(0)
    prev = te_ref[jnp.maximum(i - 1, 0)]

    @pl.when((i == 0) | (te_ref[i] != prev))
    def _():
        wgb_ref[...] = wg_ref[0, 0].astype(BF16)
        wub_ref[...] = wu_ref[0, 0].astype(BF16)
        wdb_ref[...] = wd_ref[0, 0].astype(BF16)

    @pl.when(i < nt_ref[0])
    def _():
        lo, hi = _unpack_halves(x_ref[...])
        lo, hi = lo.astype(BF16), hi.astype(BF16)
        k = D // 2
        gate = _dot(lo, wgb_ref[0:k, :]) + _dot(hi, wgb_ref[k:D, :])
        up = _dot(lo, wub_ref[0:k, :]) + _dot(hi, wub_ref[k:D, :])
        o_ref[...] = _pack_halves(_dot((_silu(gate) * up).astype(BF16), wdb_ref[...]))

    @pl.when(i >= nt_ref[0])
    def _():
        o_ref[...] = jnp.zeros_like(o_ref)


def expert_mlp(xs, tile_expert, n_tiles, w_gate, w_up, w_down, layer):
    p = xs.shape[0]
    nt = p // MOE_TM
    wspec = lambda shp: pl.BlockSpec((1, 1) + shp, lambda i, te, n: (layer, te[i], 0, 0))
    return pl.pallas_call(
        _expert_body,
        grid_spec=pltpu.PrefetchScalarGridSpec(
            num_scalar_prefetch=2, grid=(nt,),
            in_specs=[pl.BlockSpec((MOE_TM, D // 2), lambda i, te, n: (i, 0)),
                      wspec((D, D_EXPERT)), wspec((D, D_EXPERT)), wspec((D_EXPERT, D))],
            out_specs=pl.BlockSpec((MOE_TM, D // 2), lambda i, te, n: (i, 0)),
            scratch_shapes=[pltpu.VMEM((D, D_EXPERT), BF16), pltpu.VMEM((D, D_EXPERT), BF16),
                            pltpu.VMEM((D_EXPERT, D), BF16)]),
        out_shape=jax.ShapeDtypeStruct((p, D // 2), jnp.uint32),
        compiler_params=_cparams(("arbitrary",)),
        name="expert_mlp",
    )(tile_expert, n_tiles, xs, w_gate, w_up, w_down)


def moe_layer(x1, route_t, counts, w_gate, w_up, w_down, layer):
    t = x1.shape[0]
    e0 = route_t[0].astype(jnp.int32)
    e1 = route_t[1].astype(jnp.int32)
    cnt = counts[0, MOE_G:MOE_G + MOE_E].astype(jnp.int32)
    padded = (cnt + MOE_TM - 1) // MOE_TM * MOE_TM
    ends = jnp.cumsum(padded)
    starts = ends - padded
    d0 = starts[e0] + route_t[4].astype(jnp.int32)
    d1 = starts[e1] + route_t[5].astype(jnp.int32)
    p_rows = (2 * t + MOE_E * (MOE_TM - 1)) // 512 * 512 + 512
    nt = p_rows // MOE_TM
    tok = jnp.arange(t, dtype=jnp.int32)
    src = (jnp.arange(p_rows, dtype=jnp.int32) % t).at[jnp.concatenate([d0, d1])].set(
        jnp.concatenate([tok, tok]), mode="promise_in_bounds", unique_indices=True)
    tile_start = jnp.arange(nt, dtype=jnp.int32) * MOE_TM
    tile_expert = jnp.minimum(jnp.sum((tile_start[:, None] >= ends[None, :]).astype(jnp.int32), axis=1),
                              MOE_E - 1).astype(jnp.int32)
    n_tiles = (ends[-1] // MOE_TM).astype(jnp.int32).reshape(1)
    xs = x1.at[src].get(mode="promise_in_bounds")
    ys = expert_mlp(xs, tile_expert, n_tiles, w_gate, w_up, w_down, layer)
    return ys.at[d0].get(mode="promise_in_bounds"), ys.at[d1].get(mode="promise_in_bounds")


def _forward(x_prompt, x_sample, states, ssd_p, swa_p, s5_p, hg_p, ln_p, moe_p):
    (state_ssd_conv, state_ssd, cache_k, cache_v, s5_re, s5_im, state_hgrn) = states
    lp = x_prompt.shape[1]
    sb, ls = x_sample.shape[0], x_sample.shape[1]
    t_all = lp + sb * ls
    ln1_g, ln1_b, ln2_g, ln2_b = ln_p
    w_rg, b_rg, w_re, b_re, w_gate, w_up, w_down = moe_p

    x = jnp.concatenate([x_prompt.reshape(lp, D), x_sample.reshape(sb * ls, D)], axis=0)
    xb = x.astype(BF16)
    outs = {}
    for layer in range(DEPTH):
        kind = layer % 4
        if kind == 0:
            w_in, conv_w, conv_b, dt_bias, a_log, d_skip, norm_w, w_out = ssd_p
            zx = matmul(xb, w_in, col0=0, ncols=2 * SSD_INNER, tn=MM_TN, tiled_out=True, sub=SSD_GW,
                        name="mm_ssd_zx")
            bc = matmul(xb, w_in, col0=2 * SSD_INNER, ncols=2 * SSD_G * SSD_N, tn=MM_TN, tiled_out=True,
                        sub=SSD_N, name="mm_ssd_bc")
            dt = matmul(xb, w_in[:, SSD_INNER + SSD_XBC:], tn=SSD_HEADS, name="mm_ssd_dt")
            prm = (conv_w, conv_b, dt_bias, a_log, d_skip, norm_w)
            zc = jnp.zeros((1, SSD_CONV - 1, SSD_XBC), F32)
            zh = jnp.zeros((1, SSD_HEADS, SSD_INNER // SSD_HEADS, SSD_N), F32)
            y, pc, ph = ssd_core(zx, bc, dt, zc, zh, prm, row0=0, streams=1, length=lp, chunk=128)
            y, sc, sh = ssd_core(zx, bc, dt, state_ssd_conv, state_ssd, prm, row0=lp, streams=sb,
                                 length=ls, chunk=ls, y_prev=y)
            outs['conv'], outs['ssd'] = (pc, sc), (ph, sh)
            h = matmul(y, w_out, name="mm_ssd_out")
        elif kind == 1:
            w_qkv, sinks, w_out = swa_p
            kvw = SWA_KVH * SWA_DH
            q = matmul(xb, w_qkv, col0=0, ncols=D, out_dtype=BF16, name="mm_swa_q")
            kv = matmul(xb, w_qkv, col0=D, ncols=2 * kvw, name="mm_swa_kv")
            o = swa_core(q, kv, None, None, sinks, row0=0, tiles=lp // 256, qt=256, prompt=True)
            ck = cache_k.reshape(sb, WINDOW, kvw)
            cv = cache_v.reshape(sb, WINDOW, kvw)
            o = swa_core(q, kv, ck, cv, sinks, row0=lp, tiles=sb, qt=ls, prompt=False, o_prev=o)
            kshape = (SWA_KVH, SWA_DH)
            pk = kv[lp - WINDOW:lp, :kvw].reshape((1, WINDOW) + kshape)
            pv = kv[lp - WINDOW:lp, kvw:].reshape((1, WINDOW) + kshape)
            kvs = kv[lp:].reshape(sb, ls, 2 * kvw)
            sk = jnp.concatenate([ck, kvs[:, :, :kvw]], axis=1)[:, -WINDOW:].reshape((sb, WINDOW) + kshape)
            sv = jnp.concatenate([cv, kvs[:, :, kvw:]], axis=1)[:, -WINDOW:].reshape((sb, WINDOW) + kshape)
            outs['k'], outs['v'] = (pk, sk), (pv, sv)
            h = matmul(o, w_out, name="mm_swa_out")
        elif kind == 2:
            w_in, w_glu = s5_p[0], s5_p[-1]
            tables = s5_tables(s5_p[1:-1])
            u = matmul(xb, w_in, name="mm_s5_in")
            zs = jnp.zeros((1, S5_G, S5_N), F32)
            y, pr, pi = s5_core(u, zs, zs, tables, row0=0, streams=1, length=lp)
            y, sr, si = s5_core(u, s5_re, s5_im, tables, row0=lp, streams=sb, length=ls, y_prev=y)
            outs['s5r'], outs['s5i'] = (pr, sr), (pi, si)
            h = glu_matmul(y, w_glu)
        else:
            w_in, lb_param, norm_w, w_out = hg_p
            qfig = matmul(xb, w_in, name="mm_hg_in")
            zs = jnp.zeros((1, HG_H, HG_K, HG_K), F32)
            o, ps = hgrn_core(qfig, lb_param, norm_w, zs, layer=layer, row0=0, streams=1, length=lp, chunk=128)
            o, ss = hgrn_core(qfig, lb_param, norm_w, state_hgrn, layer=layer, row0=lp, streams=sb,
                              length=ls, chunk=ls, o_prev=o)
            outs['hg'] = (ps, ss)
            h = matmul(o, w_out, name="mm_hg_out")
        wr = jnp.concatenate([w_rg[layer], w_re[layer], jnp.zeros((D, 128 - MOE_G - MOE_E), F32)], axis=1)
        br = jnp.concatenate([b_rg[layer], b_re[layer], jnp.zeros((128 - MOE_G - MOE_E,), F32)]).reshape(1, 128)
        x1, x1p, route, route_t, counts = ln_route(x, h, ln1_g[layer], ln1_b[layer], wr, br)
        y0, y1 = moe_layer(x1p, route_t, counts, w_gate, w_up, w_down, layer)
        x, xb = ln_combine(x1, y0, y1, route, ln2_g[layer], ln2_b[layer],
                           split=lp if layer == DEPTH - 1 else None)
    y_prompt = x.reshape(1, lp, D)
    y_sample = xb.reshape(sb, ls, D)
    order = ('conv', 'ssd', 'k', 'v', 's5r', 's5i', 'hg')
    return (y_prompt, y_sample) + tuple(outs[k][0] for k in order) + tuple(outs[k][1] for k in order)


def kernel(x_prompt, x_sample, state_ssd_conv, state_ssd, cache_swa_k, cache_swa_v, state_s5_re, state_s5_im, state_hgrn, ssd_w_in, ssd_conv_w, ssd_conv_b, ssd_dt_bias, ssd_a_log, ssd_d, ssd_norm_w, ssd_w_out, swa_w_qkv, swa_sinks, swa_w_out, s5_w_in, s5_a_re, s5_a_im, s5_log_dt, s5_b_re, s5_b_im, s5_c_re, s5_c_im, s5_d, s5_w_glu, hg_w_in, hg_lb, hg_norm_w, hg_w_out, ln1_g, ln1_b, ln2_g, ln2_b, moe_w_rg, moe_b_rg, moe_w_re, moe_b_re, moe_w_gate, moe_w_up, moe_w_down):
    states = (state_ssd_conv, state_ssd, cache_swa_k, cache_swa_v, state_s5_re, state_s5_im, state_hgrn)
    ssd_p = (ssd_w_in, ssd_conv_w, ssd_conv_b, ssd_dt_bias, ssd_a_log, ssd_d, ssd_norm_w, ssd_w_out)
    swa_p = (swa_w_qkv, swa_sinks, swa_w_out)
    s5_p = (s5_w_in, s5_a_re, s5_a_im, s5_log_dt, s5_b_re, s5_b_im, s5_c_re, s5_c_im, s5_d, s5_w_glu)
    hg_p = (hg_w_in, hg_lb, hg_norm_w, hg_w_out)
    ln_p = (ln1_g, ln1_b, ln2_g, ln2_b)
    moe_p = (moe_w_rg, moe_b_rg, moe_w_re, moe_b_re, moe_w_gate, moe_w_up, moe_w_down)
    return _forward(x_prompt, x_sample, states, ssd_p, swa_p, s5_p, hg_p, ln_p, moe_p)
```

```python
import functools

import jax
import jax.numpy as jnp
from jax import lax
from jax.experimental import pallas as pl
from jax.experimental.pallas import tpu as pltpu

F32 = jnp.float32
BF16 = jnp.bfloat16
HIGHEST = lax.Precision.HIGHEST

D = 2048
DEPTH = 4
DN_ALPHA = (2 * DEPTH) ** 0.25
LN_EPS = 1e-5
RMS_EPS = 1e-6
NEG = -1e30

VMEM_LIMIT = 56 * 1024 * 1024
MM_TN = 1024
MM_TM = 768

SSD_INNER = 4096
SSD_HEADS = 64
SSD_G = 8
SSD_HPG = 8
SSD_N = 128
SSD_GW = SSD_INNER // SSD_G
SSD_CONV = 4
SSD_XBC = SSD_INNER + 2 * SSD_G * SSD_N

SWA_DH = 64
SWA_QH = 32
SWA_KVH = 4
SWA_GRP = SWA_QH // SWA_KVH
WINDOW = 128
CHUNK = 64

S5_G = 128
S5_CH = 16
S5_N = 64
S5_SUB = 16
S5_GB = 8
S5_CW = S5_GB * S5_CH
S5_KR = 256

HG_H = 16
HG_K = 128
HG_LEAF = 16

MOE_G = 4
MOE_PG = 8
MOE_E = 32
D_EXPERT = 256
MOE_TM = 512


def _cparams(sem):
    return pltpu.CompilerParams(dimension_semantics=sem, vmem_limit_bytes=VMEM_LIMIT)


def _sigmoid(x):
    return 1.0 / (1.0 + jnp.exp(-x))


def _silu(x):
    return x * _sigmoid(x)


def _softplus(x):
    return jnp.maximum(x, 0.0) + jnp.log(1.0 + jnp.exp(-jnp.abs(x)))


def _dot(a, b, precision=None):
    return jnp.dot(a, b, preferred_element_type=F32, precision=precision)


def _dot_nt(a, b):
    return lax.dot_general(a, b, (((1,), (1,)), ((), ())), preferred_element_type=F32)


def _tri(n, upper=False):
    r = lax.broadcasted_iota(jnp.int32, (n, n), 0)
    c = lax.broadcasted_iota(jnp.int32, (n, n), 1)
    return (r <= c) if upper else (c <= r)


def _transpose_rows(x):
    c = x.shape[0]
    if c == 128:
        return x.T
    pad = jnp.zeros((128 - c, 128), x.dtype)
    return jnp.concatenate([x, pad], axis=0).T[:, :c]


def _pack_halves(x):
    n = x.shape[1] // 2
    lo = lax.bitcast_convert_type(x[:, :n].astype(BF16).astype(F32), jnp.uint32)
    hi = lax.bitcast_convert_type(x[:, n:].astype(BF16).astype(F32), jnp.uint32)
    return (lo >> 16) | hi


def _unpack_halves(w):
    lo = lax.bitcast_convert_type(w << 16, F32)
    hi = lax.bitcast_convert_type(w & jnp.uint32(0xFFFF0000), F32)
    return lo, hi


def _mm_body(x_ref, w_ref, o_ref, wb_ref, *, tiled_out):
    @pl.when(pl.program_id(1) == 0)
    def _():
        wb_ref[...] = w_ref[...].astype(BF16)

    r = _dot(x_ref[...], wb_ref[...]).astype(o_ref.dtype)
    if tiled_out:
        sub = o_ref.shape[2]
        for q in range(o_ref.shape[0]):
            o_ref[q] = r[:, sub * q:sub * (q + 1)]
    else:
        o_ref[...] = r


def _row_tile(m):
    return next(t for t in (MM_TM, 512, 256, 128, 64, 8) if m % t == 0)


def matmul(x, w, *, col0=0, ncols=None, tn=None, out_dtype=F32, tiled_out=False, sub=None, name="matmul"):
    m, k = x.shape
    ncols = w.shape[1] - col0 if ncols is None else ncols
    tm = _row_tile(m)
    if tn is None:
        tn = MM_TN if k * MM_TN * 4 <= 8 * 1024 * 1024 else MM_TN // 2
        tn = min(tn, ncols)
    assert col0 % tn == 0 and ncols % tn == 0 and m % tm == 0
    nj = ncols // tn
    j0 = col0 // tn
    if tiled_out:
        sub = tn if sub is None else sub
        out_shape = jax.ShapeDtypeStruct((ncols // sub, m, sub), out_dtype)
        out_spec = pl.BlockSpec((tn // sub, tm, sub), lambda j, i: (j, i, 0))
    else:
        out_shape = jax.ShapeDtypeStruct((m, ncols), out_dtype)
        out_spec = pl.BlockSpec((tm, tn), lambda j, i: (i, j))
    return pl.pallas_call(
        functools.partial(_mm_body, tiled_out=tiled_out),
        grid=(nj, m // tm),
        in_specs=[pl.BlockSpec((tm, k), lambda j, i: (i, 0)),
                  pl.BlockSpec((k, tn), lambda j, i: (0, j + j0))],
        out_specs=out_spec,
        out_shape=out_shape,
        scratch_shapes=[pltpu.VMEM((k, tn), BF16)],
        compiler_params=_cparams(("arbitrary", "arbitrary")),
        name=name,
    )(x, w)


def _glu_body(x_ref, wv_ref, wg_ref, o_ref, wvb_ref, wgb_ref):
    @pl.when(pl.program_id(1) == 0)
    def _():
        wvb_ref[...] = wv_ref[...].astype(BF16)
        wgb_ref[...] = wg_ref[...].astype(BF16)

    x = x_ref[...].astype(BF16)
    o_ref[...] = _dot(x, wvb_ref[...]) * _sigmoid(_dot(x, wgb_ref[...]))


def glu_matmul(x, w, *, tn=512):
    m, k = x.shape
    n = w.shape[1] // 2
    tm = _row_tile(m)
    nj = n // tn
    return pl.pallas_call(
        _glu_body,
        grid=(nj, m // tm),
        in_specs=[pl.BlockSpec((tm, k), lambda j, i: (i, 0)),
                  pl.BlockSpec((k, tn), lambda j, i: (0, j)),
                  pl.BlockSpec((k, tn), lambda j, i: (0, j + nj))],
        out_specs=pl.BlockSpec((tm, tn), lambda j, i: (i, j)),
        out_shape=jax.ShapeDtypeStruct((m, n), F32),
        scratch_shapes=[pltpu.VMEM((k, tn), BF16), pltpu.VMEM((k, tn), BF16)],
        compiler_params=_cparams(("arbitrary", "arbitrary")),
        name="glu_matmul",
    )(x, w, w)


def _layer_norm(v, g, b):
    mu = jnp.mean(v, axis=-1, keepdims=True)
    vc = v - mu
    var = jnp.mean(vc * vc, axis=-1, keepdims=True)
    return vc * lax.rsqrt(var + LN_EPS) * g + b


def _route(x, wh, wl, br, carry):
    xh = x.astype(BF16)
    xl = (x - xh.astype(F32)).astype(BF16)
    lg = _dot(xh, wh) + (_dot(xh, wl) + _dot(xl, wh)) + br
    tm = lg.shape[0]
    lane = lax.broadcasted_iota(jnp.int32, lg.shape, 1)
    lanef = lane.astype(F32)
    big = jnp.float32(1e9)
    is_g = lane < MOE_G
    gl = jnp.where(is_g, lg, NEG)
    gmax = jnp.max(gl, axis=-1, keepdims=True)
    gsel = jnp.min(jnp.where(is_g & (gl == gmax), lanef, big), axis=-1, keepdims=True)
    gprob = 1.0 / jnp.sum(jnp.where(is_g, jnp.exp(gl - gmax), 0.0), axis=-1, keepdims=True)
    lo = MOE_G + MOE_PG * gsel
    is_e = (lanef >= lo) & (lanef < lo + MOE_PG)
    el = jnp.where(is_e, lg, NEG)
    m1 = jnp.max(el, axis=-1, keepdims=True)
    l1 = jnp.min(jnp.where(is_e & (el == m1), lanef, big), axis=-1, keepdims=True)
    is_e2 = is_e & (lanef != l1)
    el2 = jnp.where(is_e2, lg, NEG)
    m2 = jnp.max(el2, axis=-1, keepdims=True)
    l2 = jnp.min(jnp.where(is_e2 & (el2 == m2), lanef, big), axis=-1, keepdims=True)
    r = jnp.exp(m2 - m1)
    w1 = gprob / (1.0 + r)
    w2 = gprob * r / (1.0 + r)
    hit1 = lanef == l1
    hit2 = lanef == l2
    oh = jnp.where(hit1 | hit2, 1.0, 0.0)
    rr = lax.broadcasted_iota(jnp.int32, (tm, tm), 0)
    cc = lax.broadcasted_iota(jnp.int32, (tm, tm), 1)
    before = _dot(jnp.where(cc < rr, 1.0, 0.0).astype(BF16), oh.astype(BF16)) + carry
    k1 = jnp.sum(jnp.where(hit1, before, 0.0), axis=-1, keepdims=True)
    k2 = jnp.sum(jnp.where(hit2, before, 0.0), axis=-1, keepdims=True)
    table = jnp.where(lane == 0, l1 - MOE_G,
                      jnp.where(lane == 1, l2 - MOE_G,
                                jnp.where(lane == 2, w1,
                                          jnp.where(lane == 3, w2,
                                                    jnp.where(lane == 4, k1, jnp.where(lane == 5, k2, 0.0))))))
    return table, carry + jnp.sum(oh, axis=0, keepdims=True)


def _ln_route_body(x_ref, h_ref, g_ref, b_ref, wh_ref, wl_ref, br_ref, o_ref, op_ref, r_ref, rt_ref, cnt_ref):
    @pl.when(pl.program_id(0) == 0)
    def _():
        cnt_ref[...] = jnp.zeros_like(cnt_ref)

    y = _layer_norm(DN_ALPHA * x_ref[...] + h_ref[...], g_ref[...], b_ref[...])
    o_ref[...] = y
    op_ref[...] = _pack_halves(y)
    table, cnt_ref[...] = _route(y, wh_ref[...], wl_ref[...], br_ref[...], cnt_ref[...])
    r_ref[...] = table
    rt_ref[...] = jnp.concatenate([table[128 * q:128 * (q + 1)].T[:8] for q in range(table.shape[0] // 128)],
                                  axis=1)


def ln_route(x, h, g, b, wr, br, tm=256):
    m = x.shape[0]
    row = pl.BlockSpec((tm, D), lambda i: (i, 0))
    vec = pl.BlockSpec((1, D), lambda i: (0, 0))
    one = pl.BlockSpec((1, 128), lambda i: (0, 0))
    wmat = pl.BlockSpec((D, 128), lambda i: (0, 0))
    wh = wr.astype(BF16)
    wl = (wr - wh.astype(F32)).astype(BF16)
    return pl.pallas_call(
        _ln_route_body,
        grid=(m // tm,),
        in_specs=[row, row, vec, vec, wmat, wmat, one],
        out_specs=[row, pl.BlockSpec((tm, D // 2), lambda i: (i, 0)), pl.BlockSpec((tm, 128), lambda i: (i, 0)),
                   pl.BlockSpec((8, tm), lambda i: (0, i)), one],
        out_shape=[jax.ShapeDtypeStruct((m, D), F32), jax.ShapeDtypeStruct((m, D // 2), jnp.uint32),
                   jax.ShapeDtypeStruct((m, 128), F32), jax.ShapeDtypeStruct((8, m), F32),
                   jax.ShapeDtypeStruct((1, 128), F32)],
        compiler_params=_cparams(("arbitrary",)),
        name="ln_route",
    )(x, h, g.reshape(1, D), b.reshape(1, D), wh, wl, br)


def _ln_combine_body(x_ref, y0_ref, y1_ref, r_ref, g_ref, b_ref, o_ref, ob_ref, *, split_tiles):
    r = r_ref[...]
    a_lo, a_hi = _unpack_halves(y0_ref[...])
    b_lo, b_hi = _unpack_halves(y1_ref[...])
    w0, w1 = r[:, 2:3], r[:, 3:4]
    f = jnp.concatenate([w0 * a_lo + w1 * b_lo, w0 * a_hi + w1 * b_hi], axis=1)
    y = _layer_norm(DN_ALPHA * x_ref[...] + f, g_ref[...], b_ref[...])
    if split_tiles is None:
        o_ref[...] = y
        ob_ref[...] = y.astype(BF16)
    else:
        @pl.when(pl.program_id(0) < split_tiles)
        def _():
            o_ref[...] = y

        @pl.when(pl.program_id(0) >= split_tiles)
        def _():
            ob_ref[...] = y


def ln_combine(x, y0, y1, route, g, b, tm=256, split=None):
    m = x.shape[0]
    row = pl.BlockSpec((tm, D), lambda i: (i, 0))
    half = pl.BlockSpec((tm, D // 2), lambda i: (i, 0))
    vec = pl.BlockSpec((1, D), lambda i: (0, 0))
    if split is None:
        st = None
        out_specs = [row, row]
        out_shape = [jax.ShapeDtypeStruct((m, D), F32), jax.ShapeDtypeStruct((m, D), BF16)]
    else:
        st = split // tm
        out_specs = [pl.BlockSpec((tm, D), lambda i: (jnp.minimum(i, st - 1), 0)),
                     pl.BlockSpec((tm, D), lambda i: (jnp.maximum(i - st, 0), 0))]
        out_shape = [jax.ShapeDtypeStruct((split, D), F32), jax.ShapeDtypeStruct((m - split, D), F32)]
    return pl.pallas_call(
        functools.partial(_ln_combine_body, split_tiles=st),
        grid=(m // tm,),
        in_specs=[row, half, half, pl.BlockSpec((tm, 128), lambda i: (i, 0)), vec, vec],
        out_specs=out_specs,
        out_shape=out_shape,
        compiler_params=_cparams(("arbitrary",)),
        name="ln_combine",
    )(x, y0, y1, route, g.reshape(1, D), b.reshape(1, D))


def _ssd_body(z_ref, x_ref, b_ref, c_ref, dtg_ref, dtt_ref, csx_ref, csb_ref, csc_ref, h0_ref,
              cwx_ref, cwb_ref, cwc_ref, cbx_ref, cbb_ref, cbc_ref, dtbr_ref, dtbc_ref,
              alr_ref, alc_ref, dsk_ref, nw_ref, *rest, C, aliased):
    if aliased:
        rest = rest[1:]
    (y_ref, cox_ref, cob_ref, coc_ref, ho_ref,
     tx_ref, tb_ref, tc_ref, ex_ref, eb_ref, ec_ref, ht_ref, yb_ref) = rest
    c = pl.program_id(1)
    last_chunk = c == pl.num_programs(1) - 1

    @pl.when(c == 0)
    def _init():
        tx_ref[...] = jnp.zeros_like(tx_ref)
        tb_ref[...] = jnp.zeros_like(tb_ref)
        tc_ref[...] = jnp.zeros_like(tc_ref)
        for g in range(SSD_G):
            tx_ref[g, 5:8, :] = csx_ref[0, g]
            tb_ref[g, 5:8, :] = csb_ref[0, g]
            tc_ref[g, 5:8, :] = csc_ref[0, g]
        for p in range(SSD_HEADS // 2):
            ht_ref[p] = h0_ref[0, p].T

    tril = _tri(C)
    tril_f = tril.astype(F32)
    triu_f = _tri(C, upper=True).astype(F32)
    lane = lax.broadcasted_iota(jnp.int32, (C, 128), 1)
    left = lane < 64

    def conv(e_ref, t_ref, raw, w_ref, bias_ref, g):
        e_ref[0:8, :] = t_ref[g]
        e_ref[8:8 + C, :] = raw
        w = w_ref[g]
        acc = bias_ref[g] + w[3:4, :] * raw
        for k in range(SSD_CONV - 1):
            acc = acc + w[k:k + 1, :] * e_ref[5 + k:5 + k + C, :]
        t_ref[g] = e_ref[C:C + 8, :]
        return _silu(acc)

    def group(g, carry):
        xs = conv(ex_ref, tx_ref, x_ref[g], cwx_ref, cbx_ref, g)
        bs = conv(eb_ref, tb_ref, b_ref[g], cwb_ref, cbb_ref, g)
        cs = conv(ec_ref, tc_ref, c_ref[g], cwc_ref, cbc_ref, g)

        @pl.when(last_chunk)
        def _():
            cox_ref[0, g] = ex_ref[C + 5:C + 8, :]
            cob_ref[0, g] = eb_ref[C + 5:C + 8, :]
            coc_ref[0, g] = ec_ref[C + 5:C + 8, :]

        dtv = _softplus(dtg_ref[0, g] + dtbr_ref[g])
        dtvt = _softplus(dtt_ref[0, g] + dtbc_ref[g])
        cum = _dot(tril_f, dtv * (-jnp.exp(alr_ref[g])), precision=HIGHEST)
        cumt = _dot(dtvt * (-jnp.exp(alc_ref[g])), triu_f, precision=HIGHEST)
        bsb = bs.astype(BF16)
        csb = cs.astype(BF16)
        cb = _dot_nt(csb, bsb)
        bst = _transpose_rows(bs).astype(BF16)
        dsk = dsk_ref[g]
        ys = []
        for j in range(SSD_HPG // 2):
            h0, h1 = 2 * j, 2 * j + 1
            c0, c1 = cum[:, h0:h0 + 1], cum[:, h1:h1 + 1]
            l0 = jnp.where(tril, jnp.exp(c0 - cumt[h0:h0 + 1, :]), 0.0) * cb
            l1 = jnp.where(tril, jnp.exp(c1 - cumt[h1:h1 + 1, :]), 0.0) * cb
            lhs = jnp.concatenate([l0, l1], axis=1).astype(BF16)
            xp = xs[:, 128 * j:128 * (j + 1)]
            xdt = xp * jnp.where(left, dtv[:, h0:h0 + 1], dtv[:, h1:h1 + 1])
            rhs = jnp.concatenate([jnp.where(left, xdt, 0.0), jnp.where(left, 0.0, xdt)],
                                  axis=0).astype(BF16)
            htp = ht_ref[g * 4 + j]
            y = _dot(lhs, rhs)
            y = y + _dot(csb, htp.astype(BF16)) * jnp.where(left, jnp.exp(c0), jnp.exp(c1))
            e0, e1 = cum[C - 1:C, h0:h0 + 1], cum[C - 1:C, h1:h1 + 1]
            wgt = (xdt * jnp.where(left, jnp.exp(e0 - c0), jnp.exp(e1 - c1))).astype(BF16)
            ht_ref[g * 4 + j] = jnp.where(left[0:1, :], jnp.exp(e0), jnp.exp(e1)) * htp + _dot(bst, wgt)
            ys.append(y + dsk[:, 128 * j:128 * (j + 1)] * xp)
        y = jnp.concatenate(ys, axis=1) * _silu(z_ref[g])
        ms = jnp.mean(y * y, axis=-1, keepdims=True)
        yb_ref[g] = (y * lax.rsqrt(ms + RMS_EPS) * nw_ref[g]).astype(BF16)
        return carry

    lax.fori_loop(0, SSD_G, group, 0)
    for g in range(SSD_G):
        y_ref[:, SSD_GW * g:SSD_GW * (g + 1)] = yb_ref[g]

    @pl.when(last_chunk)
    def _fin():
        for p in range(SSD_HEADS // 2):
            ho_ref[0, p] = ht_ref[p].T


def ssd_core(zx, bc, dt, conv_state, h0, params, *, row0, streams, length, chunk, y_prev=None):
    conv_w, conv_b, dt_bias, a_log, d_skip, norm_w = params
    S, L, C = streams, length, chunk
    nch = L // C
    rb0 = row0 // C
    t_all = zx.shape[1]
    dseg = dt[row0:row0 + S * L].reshape(S, L, SSD_G, SSD_HPG)
    dtg = dseg.transpose(0, 2, 1, 3)
    dtt = dseg.transpose(0, 2, 3, 1)

    def split(a, lead):
        ax = a[..., :SSD_INNER].reshape(lead + (SSD_G, SSD_GW))
        ab = a[..., SSD_INNER:SSD_INNER + SSD_G * SSD_N].reshape(lead + (SSD_G, SSD_N))
        ac = a[..., SSD_INNER + SSD_G * SSD_N:].reshape(lead + (SSD_G, SSD_N))
        return ax, ab, ac

    csx, csb, csc = (jnp.moveaxis(a, 2, 1) for a in split(conv_state, (S, SSD_CONV - 1)))
    cwx, cwb, cwc = (jnp.moveaxis(a, 1, 0) for a in split(conv_w, (SSD_CONV,)))
    cbx, cbb, cbc = (jnp.moveaxis(a, 1, 0) for a in split(conv_b.reshape(1, -1), (1,)))
    dtbr = dt_bias.reshape(SSD_G, 1, SSD_HPG)
    dtbc = dt_bias.reshape(SSD_G, SSD_HPG, 1)
    alr = a_log.reshape(SSD_G, 1, SSD_HPG)
    alc = a_log.reshape(SSD_G, SSD_HPG, 1)
    dsk = jnp.repeat(d_skip, SSD_INNER // SSD_HEADS).reshape(SSD_G, 1, SSD_GW)
    nw = norm_w.reshape(SSD_G, 1, SSD_GW)
    h0p = h0.reshape(S, SSD_HEADS // 2, 128, SSD_N)

    def rb(s, c):
        return rb0 + s * nch + c

    def full(a):
        nd = a.ndim
        return pl.BlockSpec(a.shape, lambda s, c: (0,) * nd)

    def per_stream(a):
        nd = a.ndim
        return pl.BlockSpec((1,) + a.shape[1:], lambda s, c: (s,) + (0,) * (nd - 1))

    in_specs = [
        pl.BlockSpec((SSD_G, C, SSD_GW), lambda s, c: (0, rb(s, c), 0)),
        pl.BlockSpec((SSD_G, C, SSD_GW), lambda s, c: (1, rb(s, c), 0)),
        pl.BlockSpec((SSD_G, C, SSD_N), lambda s, c: (0, rb(s, c), 0)),
        pl.BlockSpec((SSD_G, C, SSD_N), lambda s, c: (1, rb(s, c), 0)),
        pl.BlockSpec((1, SSD_G, C, SSD_HPG), lambda s, c: (s, 0, c, 0)),
        pl.BlockSpec((1, SSD_G, SSD_HPG, C), lambda s, c: (s, 0, 0, c)),
        per_stream(csx), per_stream(csb), per_stream(csc), per_stream(h0p),
        full(cwx), full(cwb), full(cwc), full(cbx), full(cbb), full(cbc),
        full(dtbr), full(dtbc), full(alr), full(alc), full(dsk), full(nw),
    ]
    args = [zx, zx, bc, bc, dtg, dtt, csx, csb, csc, h0p, cwx, cwb, cwc, cbx, cbb, cbc,
            dtbr, dtbc, alr, alc, dsk, nw]
    aliases = {}
    if y_prev is not None:
        in_specs.append(pl.BlockSpec(memory_space=pl.ANY))
        args.append(y_prev)
        aliases = {len(args) - 1: 0}
    out_shape = [
        jax.ShapeDtypeStruct((t_all, SSD_INNER), BF16),
        jax.ShapeDtypeStruct(csx.shape, F32), jax.ShapeDtypeStruct(csb.shape, F32),
        jax.ShapeDtypeStruct(csc.shape, F32), jax.ShapeDtypeStruct(h0p.shape, F32),
    ]
    out_specs = [
        pl.BlockSpec((C, SSD_INNER), lambda s, c: (rb(s, c), 0)),
        per_stream(csx), per_stream(csb), per_stream(csc), per_stream(h0p),
    ]
    scratch = [
        pltpu.VMEM((SSD_G, 8, SSD_GW), F32), pltpu.VMEM((SSD_G, 8, SSD_N), F32),
        pltpu.VMEM((SSD_G, 8, SSD_N), F32),
        pltpu.VMEM((C + 8, SSD_GW), F32), pltpu.VMEM((C + 8, SSD_N), F32), pltpu.VMEM((C + 8, SSD_N), F32),
        pltpu.VMEM((SSD_HEADS // 2, SSD_N, 128), F32),
        pltpu.VMEM((SSD_G, C, SSD_GW), BF16),
    ]
    y, cox, cob, coc, ho = pl.pallas_call(
        functools.partial(_ssd_body, C=C, aliased=y_prev is not None),
        grid=(S, nch), in_specs=in_specs, out_specs=out_specs, out_shape=out_shape,
        scratch_shapes=scratch, input_output_aliases=aliases,
        compiler_params=_cparams(("arbitrary", "arbitrary")),
        name="ssd_core",
    )(*args)
    conv_out = jnp.concatenate([jnp.moveaxis(a, 1, 2).reshape(S, SSD_CONV - 1, -1) for a in (cox, cob, coc)],
                               axis=-1)
    return y, conv_out, ho.reshape(S, SSD_HEADS, SSD_INNER // SSD_HEADS, SSD_N)


def _swa_body(sink_ref, q_ref, pk_ref, pv_ref, kv_ref, *rest, QT, prev_valid, aliased):
    o_ref = rest[-1]
    i = pl.program_id(0)
    kvw = SWA_KVH * SWA_DH
    nback = WINDOW // CHUNK
    rb = min(QT, 2 * CHUNK)
    span = rb + WINDOW
    kf = jnp.concatenate([pk_ref[...], kv_ref[:, :kvw]], axis=0).astype(BF16)
    vf = jnp.concatenate([pv_ref[...], kv_ref[:, kvw:]], axis=0).astype(BF16)
    r2 = lax.broadcasted_iota(jnp.int32, (128, 128), 0)
    c2 = lax.broadcasted_iota(jnp.int32, (128, 128), 1)
    swap = jnp.where((r2 + SWA_DH) % 128 == c2, 1.0, 0.0).astype(BF16)
    left = lax.broadcasted_iota(jnp.int32, (1, 128), 1) < SWA_DH
    k_side, v_side = [], []
    for kh in range(SWA_KVH):
        blk = slice(128 * (kh // 2), 128 * (kh // 2 + 1))
        mine = left if kh % 2 == 0 else ~left
        kb, vb = kf[:, blk], jnp.where(mine, vf[:, blk], 0.0).astype(BF16)
        ko, vo = _dot(kb, swap).astype(BF16), _dot(vb, swap).astype(BF16)
        k_side.append((kb, ko) if kh % 2 == 0 else (ko, kb))
        v_side.append((vb, vo) if kh % 2 == 0 else (vo, vb))
    scale = SWA_DH ** -0.5
    for b in range(QT // rb):
        rows = slice(rb * b, rb * (b + 1))
        keys = slice(rb * b, rb * b + span)
        qc = lax.broadcasted_iota(jnp.int32, (rb, span), 0) // CHUNK
        kc = lax.broadcasted_iota(jnp.int32, (rb, span), 1) // CHUNK
        ok = (kc >= qc) & (kc <= qc + nback)
        if not prev_valid and rb * b < WINDOW:
            ok = ok & ((kc + (rb // CHUNK) * b >= nback) | (i > 0))
        for pr in range(SWA_QH // 2):
            kh = 2 * pr // SWA_GRP
            q2 = q_ref[rows, 128 * pr:128 * (pr + 1)] * scale
            acc = None
            for side in range(2):
                qm = jnp.where(left if side == 0 else ~left, q2, 0.0).astype(BF16)
                s = jnp.where(ok, _dot_nt(qm, k_side[kh][side][keys]), NEG)
                sink = sink_ref[2 * pr + side]
                m = jnp.maximum(jnp.max(s, axis=-1, keepdims=True), sink)
                p = jnp.exp(s - m)
                den = jnp.sum(p, axis=-1, keepdims=True) + jnp.exp(sink - m)
                o = _dot(p.astype(BF16), v_side[kh][side][keys]) / den
                acc = o if acc is None else acc + o
            o_ref[rows, 128 * pr:128 * (pr + 1)] = acc.astype(BF16)


def swa_core(q, kv, prev_k, prev_v, sinks, *, row0, tiles, qt, prompt, o_prev=None):
    t_all = q.shape[0]
    rb0 = row0 // qt
    kvw = SWA_KVH * SWA_DH
    if prompt:
        wpt = qt // WINDOW
        prev_map_k = lambda i, s: (jnp.maximum(wpt * (rb0 + i) - 1, 0), 0)
        prev_map_v = lambda i, s: (jnp.maximum(wpt * (rb0 + i) - 1, 0), 1)
        pk_spec = pl.BlockSpec((WINDOW, kvw), prev_map_k)
        pv_spec = pl.BlockSpec((WINDOW, kvw), prev_map_v)
        prev_k = prev_v = kv
    else:
        pk_spec = pl.BlockSpec((None, WINDOW, kvw), lambda i, s: (i, 0, 0))
        pv_spec = pl.BlockSpec((None, WINDOW, kvw), lambda i, s: (i, 0, 0))
    in_specs = [pl.BlockSpec((qt, D), lambda i, s: (rb0 + i, 0)), pk_spec, pv_spec,
                pl.BlockSpec((qt, 2 * kvw), lambda i, s: (rb0 + i, 0))]
    args = [sinks, q, prev_k, prev_v, kv]
    aliases = {}
    if o_prev is not None:
        in_specs.append(pl.BlockSpec(memory_space=pl.ANY))
        args.append(o_prev)
        aliases = {len(args) - 1: 0}
    return pl.pallas_call(
        functools.partial(_swa_body, QT=qt, prev_valid=not prompt, aliased=o_prev is not None),
        grid_spec=pltpu.PrefetchScalarGridSpec(
            num_scalar_prefetch=1, grid=(tiles,), in_specs=in_specs,
            out_specs=pl.BlockSpec((qt, D), lambda i, s: (rb0 + i, 0))),
        out_shape=jax.ShapeDtypeStruct((t_all, D), BF16),
        input_output_aliases=aliases,
        compiler_params=_cparams(("arbitrary",)),
        name="swa_core",
    )(*args)


def _gelu_tanh(y):
    return 0.5 * y * (1.0 + jnp.tanh(0.7978845608028654 * (y + 0.044715 * y * y * y)))


def _s5_body(u_ref, wxr_ref, wxi_ref, wyr_ref, wyi_ref, kt_ref, a_ref, dsk_ref, hr0_ref, hi0_ref, *rest,
             S, R, aliased):
    if aliased:
        rest = rest[1:]
    (y_ref, hro_ref, hio_ref, wx_s, wy_s, ktm_s, xr_s, xi_s, pr_s, pi_s, hr_s, hi_s) = rest
    kb = pl.program_id(1)
    half = S5_CW * S5_N // S5_CH

    @pl.when((pl.program_id(0) == 0) & (kb == 0))
    def _zero():
        ktm_s[...] = jnp.zeros_like(ktm_s)

    @pl.when(kb == 0)
    def _build():
        own = (lax.broadcasted_iota(jnp.int32, (S5_CW, half), 0) // S5_CH
               == lax.broadcasted_iota(jnp.int32, (S5_CW, half), 1) // S5_N)
        for s in range(S5_SUB):
            rows = slice(S5_CW * s, S5_CW * (s + 1))
            wx_s[rows, 0:half] = jnp.where(own, jnp.concatenate([wxr_ref[s]] * 4, axis=1), 0.0).astype(BF16)
            wx_s[rows, half:2 * half] = jnp.where(own, jnp.concatenate([wxi_ref[s]] * 4, axis=1), 0.0).astype(BF16)
        own_t = (lax.broadcasted_iota(jnp.int32, (half, S5_CW), 0) // S5_N
                 == lax.broadcasted_iota(jnp.int32, (half, S5_CW), 1) // S5_CH)
        for t in range(S5_SUB):
            cols = slice(S5_CW * t, S5_CW * (t + 1))
            wy_s[0:half, cols] = jnp.where(own_t, jnp.concatenate([wyr_ref[t]] * 8, axis=0), 0.0).astype(BF16)
            wy_s[half:2 * half, cols] = jnp.where(own_t, jnp.concatenate([wyi_ref[t]] * 8, axis=0), 0.0).astype(BF16)
        same = (lax.broadcasted_iota(jnp.int32, (S5_CW, S5_CW), 0) // S5_CH
                == lax.broadcasted_iota(jnp.int32, (S5_CW, S5_CW), 1) // S5_CH)
        taps = [jnp.where(same, kt_ref[tau], 0.0).astype(BF16) for tau in range(S5_SUB)]
        for s in range(S5_SUB):
            for t in range(s, S5_SUB):
                ktm_s[S5_CW * s:S5_CW * (s + 1), S5_CW * t:S5_CW * (t + 1)] = taps[t - s]
        for st in range(S):
            hr_s[st] = hr0_ref[st]
            hi_s[st] = hi0_ref[st]

    kr = S * R
    ucat = jnp.concatenate([u_ref[pl.ds(s, kr, stride=S5_SUB), :] for s in range(S5_SUB)],
                           axis=1)
    ub = ucat.astype(BF16)
    x = _dot(ub, wx_s[...])
    xr_s[...] = x[:, :half]
    xi_s[...] = x[:, half:]
    ar = a_ref[0:1, :]
    ai = a_ref[1:2, :]
    for st in range(S):
        def step(k, carry):
            hr, hi = carry
            row = st * R + k
            pr_s[pl.ds(row, 1), :] = hr
            pi_s[pl.ds(row, 1), :] = hi
            nr = ar * hr - ai * hi + xr_s[pl.ds(row, 1), :]
            ni = ar * hi + ai * hr + xi_s[pl.ds(row, 1), :]
            return nr, ni

        hr, hi = lax.fori_loop(0, R, step, (hr_s[st], hi_s[st]))
        hr_s[st] = hr
        hi_s[st] = hi
    hprev = jnp.concatenate([pr_s[...], pi_s[...]], axis=1).astype(BF16)
    dsk = jnp.concatenate([dsk_ref[...]] * S5_SUB, axis=1)
    cb = 2 * S5_CW
    intra = jnp.concatenate([_dot(ub[:, :cb * (t + 1)], ktm_s[0:cb * (t + 1), cb * t:cb * (t + 1)])
                             for t in range(S5_SUB // 2)], axis=1)
    y = _gelu_tanh(intra + _dot(hprev, wy_s[...]) + dsk * ucat)
    for t in range(S5_SUB):
        y_ref[pl.ds(t, kr, stride=S5_SUB), :] = y[:, S5_CW * t:S5_CW * (t + 1)]

    @pl.when(kb == pl.num_programs(1) - 1)
    def _fin():
        for st in range(S):
            hro_ref[st] = hr_s[st]
            hio_ref[st] = hi_s[st]


def s5_tables(p):
    a_re, a_im, log_dt, b_re, b_im, c_re, c_im, d_skip = p
    lr, li = a_re.astype(F32), a_im.astype(F32)
    dt = jnp.exp(log_dt.astype(F32))[:, None]
    mag = jnp.exp(lr * dt)
    ab_r, ab_i = mag * jnp.cos(li * dt), mag * jnp.sin(li * dt)
    den = lr * lr + li * li
    co_r = ((ab_r - 1.0) * lr + ab_i * li) / den
    co_i = (ab_i * lr - (ab_r - 1.0) * li) / den
    bb_r = co_r[..., None] * b_re - co_i[..., None] * b_im
    bb_i = co_r[..., None] * b_im + co_i[..., None] * b_re
    pw_r, pw_i = [jnp.ones_like(ab_r)], [jnp.zeros_like(ab_i)]
    for _ in range(S5_SUB):
        r, i = pw_r[-1], pw_i[-1]
        pw_r.append(ab_r * r - ab_i * i)
        pw_i.append(ab_r * i + ab_i * r)
    pr, pi = jnp.stack(pw_r, 0), jnp.stack(pw_i, 0)
    er, ei = pr[S5_SUB - 1::-1][:S5_SUB], pi[S5_SUB - 1::-1][:S5_SUB]
    wx_r = er[..., None] * bb_r[None] - ei[..., None] * bb_i[None]
    wx_i = er[..., None] * bb_i[None] + ei[..., None] * bb_r[None]
    gb, gw = S5_G // S5_GB, S5_GB
    wx_r = wx_r.reshape(S5_SUB, gb, gw, S5_N, S5_CH).transpose(1, 0, 2, 4, 3).reshape(gb, S5_SUB, S5_CW, S5_N)
    wx_i = wx_i.reshape(S5_SUB, gb, gw, S5_N, S5_CH).transpose(1, 0, 2, 4, 3).reshape(gb, S5_SUB, S5_CW, S5_N)
    wxr = jnp.concatenate([wx_r, wx_r], axis=-1)
    wxi = jnp.concatenate([wx_i, wx_i], axis=-1)
    qr, qi = pr[1:], pi[1:]
    cr, ci = c_re.astype(F32), c_im.astype(F32)
    wy_r = cr[None] * qr[:, :, None, :] - ci[None] * qi[:, :, None, :]
    wy_i = -(cr[None] * qi[:, :, None, :] + ci[None] * qr[:, :, None, :])
    wyr = wy_r.reshape(S5_SUB, gb, gw, S5_CH, S5_N).transpose(1, 0, 4, 2, 3).reshape(gb, S5_SUB, S5_N, S5_CW)
    wyi = wy_i.reshape(S5_SUB, gb, gw, S5_CH, S5_N).transpose(1, 0, 4, 2, 3).reshape(gb, S5_SUB, S5_N, S5_CW)
    tr = pr[:S5_SUB, :, None, :] * cr[None] - pi[:S5_SUB, :, None, :] * ci[None]
    ti = pr[:S5_SUB, :, None, :] * ci[None] + pi[:S5_SUB, :, None, :] * cr[None]
    taps = jnp.einsum('agjn,gnk->agjk', tr, bb_r) - jnp.einsum('agjn,gnk->agjk', ti, bb_i)
    taps = taps.reshape(S5_SUB, gb, gw, S5_CH, S5_CH).transpose(1, 0, 2, 4, 3)
    kt = jnp.tile(taps.reshape(gb, S5_SUB, S5_CW, S5_CH), (1, 1, 1, gw))
    a16 = jnp.stack([pr[S5_SUB].reshape(gb, gw * S5_N), pi[S5_SUB].reshape(gb, gw * S5_N)], axis=1)
    dsk = d_skip.astype(F32).reshape(gb, 1, S5_CW)
    return wxr, wxi, wyr, wyi, kt, a16, dsk


def s5_core(u, s_re, s_im, tables, *, row0, streams, length, y_prev=None):
    wxr, wxi, wyr, wyi, kt, a16, dsk = tables
    S, L = streams, length
    gb = S5_G // S5_GB
    rows = S * L // S5_SUB
    if S == 1:
        kr = min(S5_KR, rows)
        spb, rps = 1, kr
    else:
        kr = rows
        spb, rps = S, L // S5_SUB
    nkb = rows // kr
    kb0 = row0 // S5_SUB // kr
    half = S5_GB * S5_N
    hr0 = s_re.reshape(S, gb, 1, half)
    hi0 = s_im.reshape(S, gb, 1, half)
    tab = lambda a: pl.BlockSpec((None,) + a.shape[1:], lambda p, k: (p,) + (0,) * (a.ndim - 1))
    st = pl.BlockSpec((S, None, 1, half), lambda p, k: (0, p, 0, 0))
    uspec = pl.BlockSpec((kr * S5_SUB, S5_CW), lambda p, k: (kb0 + k, p))
    in_specs = [uspec, tab(wxr), tab(wxi), tab(wyr), tab(wyi), tab(kt), tab(a16), tab(dsk), st, st]
    args = [u, wxr, wxi, wyr, wyi, kt, a16, dsk, hr0, hi0]
    aliases = {}
    if y_prev is not None:
        in_specs.append(pl.BlockSpec(memory_space=pl.ANY))
        args.append(y_prev)
        aliases = {len(args) - 1: 0}
    wide = S5_SUB * S5_CW
    y, hro, hio = pl.pallas_call(
        functools.partial(_s5_body, S=spb, R=rps, aliased=y_prev is not None),
        grid=(gb, nkb),
        in_specs=in_specs,
        out_specs=[uspec, st, st],
        out_shape=[jax.ShapeDtypeStruct(u.shape, F32), jax.ShapeDtypeStruct(hr0.shape, F32),
                   jax.ShapeDtypeStruct(hi0.shape, F32)],
        scratch_shapes=[pltpu.VMEM((wide, 2 * half), BF16), pltpu.VMEM((2 * half, wide), BF16),
                        pltpu.VMEM((wide, wide), BF16)]
        + [pltpu.VMEM((kr, half), F32) for _ in range(4)]
        + [pltpu.VMEM((spb, 1, half), F32) for _ in range(2)],
        input_output_aliases=aliases,
        compiler_params=_cparams(("arbitrary", "arbitrary")),
        name="s5_core",
    )(*args)
    return y, hro.reshape(S, S5_G, S5_N), hio.reshape(S, S5_G, S5_N)


def _hgrn_body(q_ref, f_ref, i_ref, g_ref, lbp_ref, nw_ref, s0_ref, *rest, C, layer, aliased):
    if aliased:
        rest = rest[1:]
    o_ref, so_ref, st_ref = rest
    c = pl.program_id(1)

    @pl.when(c == 0)
    def _init():
        for h in range(HG_H):
            st_ref[h] = s0_ref[0, h].T

    lbp = lbp_ref[...]
    e = jnp.exp(lbp - jnp.max(lbp, axis=0, keepdims=True))
    lbs = e / jnp.sum(e, axis=0, keepdims=True)
    lb = jnp.zeros((1, D), F32)
    for r in range(1, layer + 1):
        lb = lb + lbs[r:r + 1, :]
    fz = f_ref[...]
    log_sig = jnp.minimum(fz, 0.0) - jnp.log(1.0 + jnp.exp(-jnp.abs(fz)))
    la = jnp.log(lb)
    lbb = jnp.log(1.0 - lb) + log_sig
    mx = jnp.maximum(la, lbb)
    logf = mx + jnp.log(1.0 + jnp.exp(-jnp.abs(la - lbb)))
    kk = 1.0 - jnp.exp(logf)
    qs = _silu(q_ref[...])
    tril = _tri(C)
    cum = _dot(tril.astype(F32), logf, precision=HIGHEST)
    row = lax.broadcasted_iota(jnp.int32, (C, C), 0)
    col = lax.broadcasted_iota(jnp.int32, (C, C), 1)

    levels = []
    b = C
    while b >= HG_LEAF:
        nb = C // b
        ref = jnp.broadcast_to(cum.reshape(nb, b, D)[:, b // 2 - 1:b // 2, :], (nb, b, D)).reshape(C, D)
        ex = cum - ref
        same = (row // b) == (col // b)
        if b == HG_LEAF:
            mask = same & (col <= row)
            qe, ke = jnp.exp(jnp.minimum(ex, 80.0)), jnp.exp(jnp.minimum(-ex, 80.0))
        else:
            mask = same & ((row % b) >= b // 2) & ((col % b) < b // 2)
            e = jnp.exp(-jnp.abs(ex))
            qe, ke = jnp.where(ex <= 0.0, e, 1.0), jnp.where(ex <= 0.0, 1.0, e)
        levels.append((mask, (qs * qe).astype(BF16), (kk * ke).astype(BF16)))
        b //= 2

    last = cum[C - 1:C, :]
    qin = (qs * jnp.exp(cum)).astype(BF16)
    kin = (kk * jnp.exp(last - cum)).astype(BF16)
    dec = jnp.exp(last)
    vv = i_ref[...]
    vb = vv.astype(BF16)
    gate = _silu(g_ref[...])
    nw = nw_ref[...]
    outs = []
    for h in range(HG_H):
        sl = slice(HG_K * h, HG_K * (h + 1))
        att = jnp.zeros((C, C), F32)
        for mask, ql, kl in levels:
            att = att + jnp.where(mask, _dot_nt(ql[:, sl], kl[:, sl]), 0.0)
        st = st_ref[h]
        o = _dot(att.astype(BF16), vb[:, sl]) + _dot_nt(qin[:, sl], st.astype(BF16))
        st_ref[h] = st * dec[:, sl] + _dot(_transpose_rows(vv[:, sl]).astype(BF16), kin[:, sl])
        ms = jnp.mean(o * o, axis=-1, keepdims=True)
        outs.append(o * lax.rsqrt(ms + RMS_EPS) * nw)
    o_ref[...] = (jnp.concatenate(outs, axis=1) * gate).astype(BF16)

    @pl.when(c == pl.num_programs(1) - 1)
    def _fin():
        for h in range(HG_H):
            so_ref[0, h] = st_ref[h].T


def hgrn_core(qfig, lb_param, norm_w, s0, *, layer, row0, streams, length, chunk, o_prev=None):
    S, L, C = streams, length, chunk
    nch = L // C
    rb0 = row0 // C
    t_all = qfig.shape[0]
    rowspec = lambda j: pl.BlockSpec((C, D), lambda s, c: (rb0 + s * nch + c, j))
    in_specs = [rowspec(0), rowspec(1), rowspec(2), rowspec(3),
                pl.BlockSpec((DEPTH, D), lambda s, c: (0, 0)),
                pl.BlockSpec((1, HG_K), lambda s, c: (0, 0)),
                pl.BlockSpec((1, HG_H, HG_K, HG_K), lambda s, c: (s, 0, 0, 0))]
    args = [qfig, qfig, qfig, qfig, lb_param, norm_w.reshape(1, HG_K), s0]
    aliases = {}
    if o_prev is not None:
        in_specs.append(pl.BlockSpec(memory_space=pl.ANY))
        args.append(o_prev)
        aliases = {len(args) - 1: 0}
    return pl.pallas_call(
        functools.partial(_hgrn_body, C=C, layer=layer, aliased=o_prev is not None),
        grid=(S, nch), in_specs=in_specs,
        out_specs=[rowspec(0), pl.BlockSpec((1, HG_H, HG_K, HG_K), lambda s, c: (s, 0, 0, 0))],
        out_shape=[jax.ShapeDtypeStruct((t_all, D), BF16), jax.ShapeDtypeStruct(s0.shape, F32)],
        scratch_shapes=[pltpu.VMEM((HG_H, HG_K, HG_K), F32)],
        input_output_aliases=aliases,
        compiler_params=_cparams(("arbitrary", "arbitrary")),
        name="hgrn_core",
    )(*args)


def _expert_body(te_ref, nt_ref, x_ref, wg_ref, wu_ref, wd_ref, o_ref, wgb_ref, wub_ref, wdb_ref):
    i = pl.program_id(0)
    prev = te_ref[jnp.maximum(i - 1, 0)]

    @pl.when((i == 0) | (te_ref[i] != prev))
    def _():
        wgb_ref[...] = wg_ref[0, 0].astype(BF16)
        wub_ref[...] = wu_ref[0, 0].astype(BF16)
        wdb_ref[...] = wd_ref[0, 0].astype(BF16)

    @pl.when(i < nt_ref[0])
    def _():
        lo, hi = _unpack_halves(x_ref[...])
        lo, hi = lo.astype(BF16), hi.astype(BF16)
        k = D // 2
        gate = _dot(lo, wgb_ref[0:k, :]) + _dot(hi, wgb_ref[k:D, :])
        up = _dot(lo, wub_ref[0:k, :]) + _dot(hi, wub_ref[k:D, :])
        o_ref[...] = _pack_halves(_dot((_silu(gate) * up).astype(BF16), wdb_ref[...]))

    @pl.when(i >= nt_ref[0])
    def _():
        o_ref[...] = jnp.zeros_like(o_ref)


def expert_mlp(xs, tile_expert, n_tiles, w_gate, w_up, w_down, layer):
    p = xs.shape[0]
    nt = p // MOE_TM
    wspec = lambda shp: pl.BlockSpec((1, 1) + shp, lambda i, te, n: (layer, te[i], 0, 0))
    return pl.pallas_call(
        _expert_body,
        grid_spec=pltpu.PrefetchScalarGridSpec(
            num_scalar_prefetch=2, grid=(nt,),
            in_specs=[pl.BlockSpec((MOE_TM, D // 2), lambda i, te, n: (i, 0)),
                      wspec((D, D_EXPERT)), wspec((D, D_EXPERT)), wspec((D_EXPERT, D))],
            out_specs=pl.BlockSpec((MOE_TM, D // 2), lambda i, te, n: (i, 0)),
            scratch_shapes=[pltpu.VMEM((D, D_EXPERT), BF16), pltpu.VMEM((D, D_EXPERT), BF16),
                            pltpu.VMEM((D_EXPERT, D), BF16)]),
        out_shape=jax.ShapeDtypeStruct((p, D // 2), jnp.uint32),
        compiler_params=_cparams(("arbitrary",)),
        name="expert_mlp",
    )(tile_expert, n_tiles, xs, w_gate, w_up, w_down)


def moe_layer(x1, route_t, counts, w_gate, w_up, w_down, layer):
    t = x1.shape[0]
    e0 = route_t[0].astype(jnp.int32)
    e1 = route_t[1].astype(jnp.int32)
    cnt = counts[0, MOE_G:MOE_G + MOE_E].astype(jnp.int32)
    padded = (cnt + MOE_TM - 1) // MOE_TM * MOE_TM
    ends = jnp.cumsum(padded)
    starts = ends - padded
    d0 = starts[e0] + route_t[4].astype(jnp.int32)
    d1 = starts[e1] + route_t[5].astype(jnp.int32)
    p_rows = (2 * t + MOE_E * (MOE_TM - 1)) // 512 * 512 + 512
    nt = p_rows // MOE_TM
    tok = jnp.arange(t, dtype=jnp.int32)
    tile_start = jnp.arange(nt, dtype=jnp.int32) * MOE_TM
    tile_expert = jnp.minimum(jnp.sum((tile_start[:, None] >= ends[None, :]).astype(jnp.int32), axis=1),
                              MOE_E - 1).astype(jnp.int32)
    n_tiles = (ends[-1] // MOE_TM).astype(jnp.int32).reshape(1)
    _, by_pos = lax.sort_key_val(jnp.concatenate([d0, d1]), jnp.concatenate([tok, tok]))
    per_pos = lambda a: jnp.repeat(a[tile_expert], MOE_TM)
    pos = jnp.arange(p_rows, dtype=jnp.int32)
    off = pos - per_pos(starts)
    dense = jnp.minimum(per_pos(jnp.cumsum(cnt) - cnt) + off, 2 * t - 1)
    src = jnp.where(off < per_pos(cnt), by_pos.at[dense].get(mode="promise_in_bounds"), pos % t)
    xs = x1.at[src].get(mode="promise_in_bounds")
    ys = expert_mlp(xs, tile_expert, n_tiles, w_gate, w_up, w_down, layer)
    return ys.at[d0].get(mode="promise_in_bounds"), ys.at[d1].get(mode="promise_in_bounds")


def _forward(x_prompt, x_sample, states, ssd_p, swa_p, s5_p, hg_p, ln_p, moe_p):
    (state_ssd_conv, state_ssd, cache_k, cache_v, s5_re, s5_im, state_hgrn) = states
    lp = x_prompt.shape[1]
    sb, ls = x_sample.shape[0], x_sample.shape[1]
    t_all = lp + sb * ls
    ln1_g, ln1_b, ln2_g, ln2_b = ln_p
    w_rg, b_rg, w_re, b_re, w_gate, w_up, w_down = moe_p

    x = jnp.concatenate([x_prompt.reshape(lp, D), x_sample.reshape(sb * ls, D)], axis=0)
    xb = x.astype(BF16)
    outs = {}
    for layer in range(DEPTH):
        kind = layer % 4
        if kind == 0:
            w_in, conv_w, conv_b, dt_bias, a_log, d_skip, norm_w, w_out = ssd_p
            zx = matmul(xb, w_in, col0=0, ncols=2 * SSD_INNER, tn=MM_TN, tiled_out=True, sub=SSD_GW,
                        name="mm_ssd_zx")
            bc = matmul(xb, w_in, col0=2 * SSD_INNER, ncols=2 * SSD_G * SSD_N, tn=MM_TN, tiled_out=True,
                        sub=SSD_N, name="mm_ssd_bc")
            dt = matmul(xb, w_in[:, SSD_INNER + SSD_XBC:], tn=SSD_HEADS, name="mm_ssd_dt")
            prm = (conv_w, conv_b, dt_bias, a_log, d_skip, norm_w)
            zc = jnp.zeros((1, SSD_CONV - 1, SSD_XBC), F32)
            zh = jnp.zeros((1, SSD_HEADS, SSD_INNER // SSD_HEADS, SSD_N), F32)
            y, pc, ph = ssd_core(zx, bc, dt, zc, zh, prm, row0=0, streams=1, length=lp, chunk=128)
            y, sc, sh = ssd_core(zx, bc, dt, state_ssd_conv, state_ssd, prm, row0=lp, streams=sb,
                                 length=ls, chunk=ls, y_prev=y)
            outs['conv'], outs['ssd'] = (pc, sc), (ph, sh)
            h = matmul(y, w_out, name="mm_ssd_out")
        elif kind == 1:
            w_qkv, sinks, w_out = swa_p
            kvw = SWA_KVH * SWA_DH
            q = matmul(xb, w_qkv, col0=0, ncols=D, out_dtype=BF16, name="mm_swa_q")
            kv = matmul(xb, w_qkv, col0=D, ncols=2 * kvw, name="mm_swa_kv")
            o = swa_core(q, kv, None, None, sinks, row0=0, tiles=lp // 256, qt=256, prompt=True)
            ck = cache_k.reshape(sb, WINDOW, kvw)
            cv = cache_v.reshape(sb, WINDOW, kvw)
            o = swa_core(q, kv, ck, cv, sinks, row0=lp, tiles=sb, qt=ls, prompt=False, o_prev=o)
            kshape = (SWA_KVH, SWA_DH)
            pk = kv[lp - WINDOW:lp, :kvw].reshape((1, WINDOW) + kshape)
            pv = kv[lp - WINDOW:lp, kvw:].reshape((1, WINDOW) + kshape)
            kvs = kv[lp:].reshape(sb, ls, 2 * kvw)
            sk = jnp.concatenate([ck, kvs[:, :, :kvw]], axis=1)[:, -WINDOW:].reshape((sb, WINDOW) + kshape)
            sv = jnp.concatenate([cv, kvs[:, :, kvw:]], axis=1)[:, -WINDOW:].reshape((sb, WINDOW) + kshape)
            outs['k'], outs['v'] = (pk, sk), (pv, sv)
            h = matmul(o, w_out, name="mm_swa_out")
        elif kind == 2:
            w_in, w_glu = s5_p[0], s5_p[-1]
            tables = s5_tables(s5_p[1:-1])
            u = matmul(xb, w_in, name="mm_s5_in")
            zs = jnp.zeros((1, S5_G, S5_N), F32)
            y, pr, pi = s5_core(u, zs, zs, tables, row0=0, streams=1, length=lp)
            y, sr, si = s5_core(u, s5_re, s5_im, tables, row0=lp, streams=sb, length=ls, y_prev=y)
            outs['s5r'], outs['s5i'] = (pr, sr), (pi, si)
            h = glu_matmul(y, w_glu)
        else:
            w_in, lb_param, norm_w, w_out = hg_p
            qfig = matmul(xb, w_in, name="mm_hg_in")
            zs = jnp.zeros((1, HG_H, HG_K, HG_K), F32)
            o, ps = hgrn_core(qfig, lb_param, norm_w, zs, layer=layer, row0=0, streams=1, length=lp, chunk=128)
            o, ss = hgrn_core(qfig, lb_param, norm_w, state_hgrn, layer=layer, row0=lp, streams=sb,
                              length=ls, chunk=ls, o_prev=o)
            outs['hg'] = (ps, ss)
            h = matmul(o, w_out, name="mm_hg_out")
        wr = jnp.concatenate([w_rg[layer], w_re[layer], jnp.zeros((D, 128 - MOE_G - MOE_E), F32)], axis=1)
        br = jnp.concatenate([b_rg[layer], b_re[layer], jnp.zeros((128 - MOE_G - MOE_E,), F32)]).reshape(1, 128)
        x1, x1p, route, route_t, counts = ln_route(x, h, ln1_g[layer], ln1_b[layer], wr, br)
        y0, y1 = moe_layer(x1p, route_t, counts, w_gate, w_up, w_down, layer)
        x, xb = ln_combine(x1, y0, y1, route, ln2_g[layer], ln2_b[layer],
                           split=lp if layer == DEPTH - 1 else None)
    y_prompt = x.reshape(1, lp, D)
    y_sample = xb.reshape(sb, ls, D)
    order = ('conv', 'ssd', 'k', 'v', 's5r', 's5i', 'hg')
    return (y_prompt, y_sample) + tuple(outs[k][0] for k in order) + tuple(outs[k][1] for k in order)


def kernel(x_prompt, x_sample, state_ssd_conv, state_ssd, cache_swa_k, cache_swa_v, state_s5_re, state_s5_im, state_hgrn, ssd_w_in, ssd_conv_w, ssd_conv_b, ssd_dt_bias, ssd_a_log, ssd_d, ssd_norm_w, ssd_w_out, swa_w_qkv, swa_sinks, swa_w_out, s5_w_in, s5_a_re, s5_a_im, s5_log_dt, s5_b_re, s5_b_im, s5_c_re, s5_c_im, s5_d, s5_w_glu, hg_w_in, hg_lb, hg_norm_w, hg_w_out, ln1_g, ln1_b, ln2_g, ln2_b, moe_w_rg, moe_b_rg, moe_w_re, moe_b_re, moe_w_gate, moe_w_up, moe_w_down):
    states = (state_ssd_conv, state_ssd, cache_swa_k, cache_swa_v, state_s5_re, state_s5_im, state_hgrn)
    ssd_p = (ssd_w_in, ssd_conv_w, ssd_conv_b, ssd_dt_bias, ssd_a_log, ssd_d, ssd_norm_w, ssd_w_out)
    swa_p = (swa_w_qkv, swa_sinks, swa_w_out)
    s5_p = (s5_w_in, s5_a_re, s5_a_im, s5_log_dt, s5_b_re, s5_b_im, s5_c_re, s5_c_im, s5_d, s5_w_glu)
    hg_p = (hg_w_in, hg_lb, hg_norm_w, hg_w_out)
    ln_p = (ln1_g, ln1_b, ln2_g, ln2_b)
    moe_p = (moe_w_rg, moe_b_rg, moe_w_re, moe_b_re, moe_w_gate, moe_w_up, moe_w_down)
    return _forward(x_prompt, x_sample, states, ssd_p, swa_p, s5_p, hg_p, ln_p, moe_p)
```

```python
import functools

import jax
import jax.numpy as jnp
from jax import lax
from jax.experimental import pallas as pl
from jax.experimental.pallas import tpu as pltpu

F32 = jnp.float32
BF16 = jnp.bfloat16
HIGHEST = lax.Precision.HIGHEST

D = 2048
DEPTH = 4
DN_ALPHA = (2 * DEPTH) ** 0.25
LN_EPS = 1e-5
RMS_EPS = 1e-6
NEG = -1e30

VMEM_LIMIT = 56 * 1024 * 1024
MM_TN = 1024
MM_TM = 768

SSD_INNER = 4096
SSD_HEADS = 64
SSD_G = 8
SSD_HPG = 8
SSD_N = 128
SSD_GW = SSD_INNER // SSD_G
SSD_CONV = 4
SSD_XBC = SSD_INNER + 2 * SSD_G * SSD_N

SWA_DH = 64
SWA_QH = 32
SWA_KVH = 4
SWA_GRP = SWA_QH // SWA_KVH
WINDOW = 128
CHUNK = 64

S5_G = 128
S5_CH = 16
S5_N = 64
S5_SUB = 16
S5_GB = 8
S5_CW = S5_GB * S5_CH
S5_KR = 256

HG_H = 16
HG_K = 128
HG_LEAF = 16

MOE_G = 4
MOE_PG = 8
MOE_E = 32
D_EXPERT = 256
MOE_TM = 512


def _cparams(sem):
    return pltpu.CompilerParams(dimension_semantics=sem, vmem_limit_bytes=VMEM_LIMIT)


def _sigmoid(x):
    return 1.0 / (1.0 + jnp.exp(-x))


def _silu(x):
    return x * _sigmoid(x)


def _softplus(x):
    return jnp.maximum(x, 0.0) + jnp.log(1.0 + jnp.exp(-jnp.abs(x)))


def _dot(a, b, precision=None):
    return jnp.dot(a, b, preferred_element_type=F32, precision=precision)


def _dot_nt(a, b):
    return lax.dot_general(a, b, (((1,), (1,)), ((), ())), preferred_element_type=F32)


def _tri(n, upper=False):
    r = lax.broadcasted_iota(jnp.int32, (n, n), 0)
    c = lax.broadcasted_iota(jnp.int32, (n, n), 1)
    return (r <= c) if upper else (c <= r)


def _transpose_rows(x):
    c = x.shape[0]
    if c == 128:
        return x.T
    pad = jnp.zeros((128 - c, 128), x.dtype)
    return jnp.concatenate([x, pad], axis=0).T[:, :c]


def _pack_halves(x):
    n = x.shape[1] // 2
    lo = lax.bitcast_convert_type(x[:, :n].astype(BF16).astype(F32), jnp.uint32)
    hi = lax.bitcast_convert_type(x[:, n:].astype(BF16).astype(F32), jnp.uint32)
    return (lo >> 16) | hi


def _unpack_halves(w):
    lo = lax.bitcast_convert_type(w << 16, F32)
    hi = lax.bitcast_convert_type(w & jnp.uint32(0xFFFF0000), F32)
    return lo, hi


def _mm_body(x_ref, w_ref, o_ref, wb_ref, *, tiled_out):
    @pl.when(pl.program_id(1) == 0)
    def _():
        wb_ref[...] = w_ref[...].astype(BF16)

    r = _dot(x_ref[...], wb_ref[...]).astype(o_ref.dtype)
    if tiled_out:
        sub = o_ref.shape[2]
        for q in range(o_ref.shape[0]):
            o_ref[q] = r[:, sub * q:sub * (q + 1)]
    else:
        o_ref[...] = r


def _row_tile(m):
    return next(t for t in (MM_TM, 512, 256, 128, 64, 8) if m % t == 0)


def matmul(x, w, *, col0=0, ncols=None, tn=None, out_dtype=F32, tiled_out=False, sub=None, name="matmul"):
    m, k = x.shape
    ncols = w.shape[1] - col0 if ncols is None else ncols
    tm = _row_tile(m)
    if tn is None:
        tn = MM_TN if k * MM_TN * 4 <= 8 * 1024 * 1024 else MM_TN // 2
        tn = min(tn, ncols)
    assert col0 % tn == 0 and ncols % tn == 0 and m % tm == 0
    nj = ncols // tn
    j0 = col0 // tn
    if tiled_out:
        sub = tn if sub is None else sub
        out_shape = jax.ShapeDtypeStruct((ncols // sub, m, sub), out_dtype)
        out_spec = pl.BlockSpec((tn // sub, tm, sub), lambda j, i: (j, i, 0))
    else:
        out_shape = jax.ShapeDtypeStruct((m, ncols), out_dtype)
        out_spec = pl.BlockSpec((tm, tn), lambda j, i: (i, j))
    return pl.pallas_call(
        functools.partial(_mm_body, tiled_out=tiled_out),
        grid=(nj, m // tm),
        in_specs=[pl.BlockSpec((tm, k), lambda j, i: (i, 0)),
                  pl.BlockSpec((k, tn), lambda j, i: (0, j + j0))],
        out_specs=out_spec,
        out_shape=out_shape,
        scratch_shapes=[pltpu.VMEM((k, tn), BF16)],
        compiler_params=_cparams(("arbitrary", "arbitrary")),
        name=name,
    )(x, w)


def _glu_body(x_ref, wv_ref, wg_ref, o_ref, wvb_ref, wgb_ref):
    @pl.when(pl.program_id(1) == 0)
    def _():
        wvb_ref[...] = wv_ref[...].astype(BF16)
        wgb_ref[...] = wg_ref[...].astype(BF16)

    x = x_ref[...].astype(BF16)
    o_ref[...] = _dot(x, wvb_ref[...]) * _sigmoid(_dot(x, wgb_ref[...]))


def glu_matmul(x, w, *, tn=512):
    m, k = x.shape
    n = w.shape[1] // 2
    tm = _row_tile(m)
    nj = n // tn
    return pl.pallas_call(
        _glu_body,
        grid=(nj, m // tm),
        in_specs=[pl.BlockSpec((tm, k), lambda j, i: (i, 0)),
                  pl.BlockSpec((k, tn), lambda j, i: (0, j)),
                  pl.BlockSpec((k, tn), lambda j, i: (0, j + nj))],
        out_specs=pl.BlockSpec((tm, tn), lambda j, i: (i, j)),
        out_shape=jax.ShapeDtypeStruct((m, n), F32),
        scratch_shapes=[pltpu.VMEM((k, tn), BF16), pltpu.VMEM((k, tn), BF16)],
        compiler_params=_cparams(("arbitrary", "arbitrary")),
        name="glu_matmul",
    )(x, w, w)


def _layer_norm(v, g, b):
    mu = jnp.mean(v, axis=-1, keepdims=True)
    vc = v - mu
    var = jnp.mean(vc * vc, axis=-1, keepdims=True)
    return vc * lax.rsqrt(var + LN_EPS) * g + b


def _route(x, wh, wl, br, carry):
    xh = x.astype(BF16)
    xl = (x - xh.astype(F32)).astype(BF16)
    lg = _dot(xh, wh) + (_dot(xh, wl) + _dot(xl, wh)) + br
    tm = lg.shape[0]
    lane = lax.broadcasted_iota(jnp.int32, lg.shape, 1)
    lanef = lane.astype(F32)
    big = jnp.float32(1e9)
    is_g = lane < MOE_G
    gl = jnp.where(is_g, lg, NEG)
    gmax = jnp.max(gl, axis=-1, keepdims=True)
    gsel = jnp.min(jnp.where(is_g & (gl == gmax), lanef, big), axis=-1, keepdims=True)
    gprob = 1.0 / jnp.sum(jnp.where(is_g, jnp.exp(gl - gmax), 0.0), axis=-1, keepdims=True)
    lo = MOE_G + MOE_PG * gsel
    is_e = (lanef >= lo) & (lanef < lo + MOE_PG)
    el = jnp.where(is_e, lg, NEG)
    m1 = jnp.max(el, axis=-1, keepdims=True)
    l1 = jnp.min(jnp.where(is_e & (el == m1), lanef, big), axis=-1, keepdims=True)
    is_e2 = is_e & (lanef != l1)
    el2 = jnp.where(is_e2, lg, NEG)
    m2 = jnp.max(el2, axis=-1, keepdims=True)
    l2 = jnp.min(jnp.where(is_e2 & (el2 == m2), lanef, big), axis=-1, keepdims=True)
    r = jnp.exp(m2 - m1)
    w1 = gprob / (1.0 + r)
    w2 = gprob * r / (1.0 + r)
    hit1 = lanef == l1
    hit2 = lanef == l2
    oh = jnp.where(hit1 | hit2, 1.0, 0.0)
    rr = lax.broadcasted_iota(jnp.int32, (tm, tm), 0)
    cc = lax.broadcasted_iota(jnp.int32, (tm, tm), 1)
    before = _dot(jnp.where(cc < rr, 1.0, 0.0).astype(BF16), oh.astype(BF16)) + carry
    k1 = jnp.sum(jnp.where(hit1, before, 0.0), axis=-1, keepdims=True)
    k2 = jnp.sum(jnp.where(hit2, before, 0.0), axis=-1, keepdims=True)
    table = jnp.where(lane == 0, l1 - MOE_G,
                      jnp.where(lane == 1, l2 - MOE_G,
                                jnp.where(lane == 2, w1,
                                          jnp.where(lane == 3, w2,
                                                    jnp.where(lane == 4, k1, jnp.where(lane == 5, k2, 0.0))))))
    return table, carry + jnp.sum(oh, axis=0, keepdims=True)


def _ln_route_body(*refs, split_tiles):
    if split_tiles is None:
        x_ref, refs = refs[0], refs[1:]
        x = x_ref[...]
    else:
        (xa_ref, xb_ref), refs = refs[:2], refs[2:]
        x = jnp.where(pl.program_id(0) < split_tiles, xa_ref[...], xb_ref[...])
    h_ref, g_ref, b_ref, wh_ref, wl_ref, br_ref, o_ref, op_ref, r_ref, rt_ref, cnt_ref = refs

    @pl.when(pl.program_id(0) == 0)
    def _():
        cnt_ref[...] = jnp.zeros_like(cnt_ref)

    y = _layer_norm(DN_ALPHA * x + h_ref[...], g_ref[...], b_ref[...])
    o_ref[...] = y
    op_ref[...] = _pack_halves(y)
    table, cnt_ref[...] = _route(y, wh_ref[...], wl_ref[...], br_ref[...], cnt_ref[...])
    r_ref[...] = table
    rt_ref[...] = jnp.concatenate([table[128 * q:128 * (q + 1)].T[:8] for q in range(table.shape[0] // 128)],
                                  axis=1)


def ln_route(x, h, g, b, wr, br, tm=256):
    m = h.shape[0]
    row = pl.BlockSpec((tm, D), lambda i: (i, 0))
    vec = pl.BlockSpec((1, D), lambda i: (0, 0))
    one = pl.BlockSpec((1, 128), lambda i: (0, 0))
    wmat = pl.BlockSpec((D, 128), lambda i: (0, 0))
    wh = wr.astype(BF16)
    wl = (wr - wh.astype(F32)).astype(BF16)
    if isinstance(x, tuple):
        st = x[0].shape[0] // tm
        xs = list(x)
        x_specs = [pl.BlockSpec((tm, D), lambda i: (jnp.minimum(i, st - 1), 0)),
                   pl.BlockSpec((tm, D), lambda i: (jnp.maximum(i - st, 0), 0))]
    else:
        st, xs, x_specs = None, [x], [row]
    return pl.pallas_call(
        functools.partial(_ln_route_body, split_tiles=st),
        grid=(m // tm,),
        in_specs=x_specs + [row, vec, vec, wmat, wmat, one],
        out_specs=[row, pl.BlockSpec((tm, D // 2), lambda i: (i, 0)), pl.BlockSpec((tm, 128), lambda i: (i, 0)),
                   pl.BlockSpec((8, tm), lambda i: (0, i)), one],
        out_shape=[jax.ShapeDtypeStruct((m, D), F32), jax.ShapeDtypeStruct((m, D // 2), jnp.uint32),
                   jax.ShapeDtypeStruct((m, 128), F32), jax.ShapeDtypeStruct((8, m), F32),
                   jax.ShapeDtypeStruct((1, 128), F32)],
        compiler_params=_cparams(("arbitrary",)),
        name="ln_route",
    )(*xs, h, g.reshape(1, D), b.reshape(1, D), wh, wl, br)


def _ln_combine_body(x_ref, y0_ref, y1_ref, r_ref, g_ref, b_ref, o_ref, ob_ref, *, split_tiles):
    r = r_ref[...]
    a_lo, a_hi = _unpack_halves(y0_ref[...])
    b_lo, b_hi = _unpack_halves(y1_ref[...])
    w0, w1 = r[:, 2:3], r[:, 3:4]
    f = jnp.concatenate([w0 * a_lo + w1 * b_lo, w0 * a_hi + w1 * b_hi], axis=1)
    y = _layer_norm(DN_ALPHA * x_ref[...] + f, g_ref[...], b_ref[...])
    if split_tiles is None:
        o_ref[...] = y
        ob_ref[...] = y.astype(BF16)
    else:
        @pl.when(pl.program_id(0) < split_tiles)
        def _():
            o_ref[...] = y

        @pl.when(pl.program_id(0) >= split_tiles)
        def _():
            ob_ref[...] = y


def ln_combine(x, y0, y1, route, g, b, tm=256, split=None):
    m = x.shape[0]
    row = pl.BlockSpec((tm, D), lambda i: (i, 0))
    half = pl.BlockSpec((tm, D // 2), lambda i: (i, 0))
    vec = pl.BlockSpec((1, D), lambda i: (0, 0))
    if split is None:
        st = None
        out_specs = [row, row]
        out_shape = [jax.ShapeDtypeStruct((m, D), F32), jax.ShapeDtypeStruct((m, D), BF16)]
    else:
        st = split // tm
        out_specs = [pl.BlockSpec((tm, D), lambda i: (jnp.minimum(i, st - 1), 0)),
                     pl.BlockSpec((tm, D), lambda i: (jnp.maximum(i - st, 0), 0))]
        out_shape = [jax.ShapeDtypeStruct((split, D), F32), jax.ShapeDtypeStruct((m - split, D), F32)]
    return pl.pallas_call(
        functools.partial(_ln_combine_body, split_tiles=st),
        grid=(m // tm,),
        in_specs=[row, half, half, pl.BlockSpec((tm, 128), lambda i: (i, 0)), vec, vec],
        out_specs=out_specs,
        out_shape=out_shape,
        compiler_params=_cparams(("arbitrary",)),
        name="ln_combine",
    )(x, y0, y1, route, g.reshape(1, D), b.reshape(1, D))


def _ssd_body(z_ref, x_ref, b_ref, c_ref, dtg_ref, dtt_ref, csx_ref, csb_ref, csc_ref, h0_ref,
              cwx_ref, cwb_ref, cwc_ref, cbx_ref, cbb_ref, cbc_ref, dtbr_ref, dtbc_ref,
              alr_ref, alc_ref, dsk_ref, nw_ref, *rest, C, aliased):
    if aliased:
        rest = rest[1:]
    (y_ref, cox_ref, cob_ref, coc_ref, ho_ref,
     tx_ref, tb_ref, tc_ref, ex_ref, eb_ref, ec_ref, ht_ref, yb_ref) = rest
    c = pl.program_id(1)
    last_chunk = c == pl.num_programs(1) - 1

    @pl.when(c == 0)
    def _init():
        tx_ref[...] = jnp.zeros_like(tx_ref)
        tb_ref[...] = jnp.zeros_like(tb_ref)
        tc_ref[...] = jnp.zeros_like(tc_ref)
        for g in range(SSD_G):
            tx_ref[g, 5:8, :] = csx_ref[0, g]
            tb_ref[g, 5:8, :] = csb_ref[0, g]
            tc_ref[g, 5:8, :] = csc_ref[0, g]
        for p in range(SSD_HEADS // 2):
            ht_ref[p] = h0_ref[0, p].T

    tril = _tri(C)
    tril_f = tril.astype(F32)
    triu_f = _tri(C, upper=True).astype(F32)
    lane = lax.broadcasted_iota(jnp.int32, (C, 128), 1)
    left = lane < 64

    def conv(e_ref, t_ref, raw, w_ref, bias_ref, g):
        e_ref[0:8, :] = t_ref[g]
        e_ref[8:8 + C, :] = raw
        w = w_ref[g]
        acc = bias_ref[g] + w[3:4, :] * raw
        for k in range(SSD_CONV - 1):
            acc = acc + w[k:k + 1, :] * e_ref[5 + k:5 + k + C, :]
        t_ref[g] = e_ref[C:C + 8, :]
        return _silu(acc)

    def group(g, carry):
        xs = conv(ex_ref, tx_ref, x_ref[g], cwx_ref, cbx_ref, g)
        bs = conv(eb_ref, tb_ref, b_ref[g], cwb_ref, cbb_ref, g)
        cs = conv(ec_ref, tc_ref, c_ref[g], cwc_ref, cbc_ref, g)

        @pl.when(last_chunk)
        def _():
            cox_ref[0, g] = ex_ref[C + 5:C + 8, :]
            cob_ref[0, g] = eb_ref[C + 5:C + 8, :]
            coc_ref[0, g] = ec_ref[C + 5:C + 8, :]

        dtv = _softplus(dtg_ref[0, g] + dtbr_ref[g])
        dtvt = _softplus(dtt_ref[0, g] + dtbc_ref[g])
        cum = _dot(tril_f, dtv * (-jnp.exp(alr_ref[g])), precision=HIGHEST)
        cumt = _dot(dtvt * (-jnp.exp(alc_ref[g])), triu_f, precision=HIGHEST)
        bsb = bs.astype(BF16)
        csb = cs.astype(BF16)
        cb = _dot_nt(csb, bsb)
        bst = _transpose_rows(bs).astype(BF16)
        dsk = dsk_ref[g]
        ys = []
        for j in range(SSD_HPG // 2):
            h0, h1 = 2 * j, 2 * j + 1
            c0, c1 = cum[:, h0:h0 + 1], cum[:, h1:h1 + 1]
            l0 = jnp.where(tril, jnp.exp(c0 - cumt[h0:h0 + 1, :]), 0.0) * cb
            l1 = jnp.where(tril, jnp.exp(c1 - cumt[h1:h1 + 1, :]), 0.0) * cb
            lhs = jnp.concatenate([l0, l1], axis=1).astype(BF16)
            xp = xs[:, 128 * j:128 * (j + 1)]
            xdt = xp * jnp.where(left, dtv[:, h0:h0 + 1], dtv[:, h1:h1 + 1])
            rhs = jnp.concatenate([jnp.where(left, xdt, 0.0), jnp.where(left, 0.0, xdt)],
                                  axis=0).astype(BF16)
            htp = ht_ref[g * 4 + j]
            y = _dot(lhs, rhs)
            y = y + _dot(csb, htp.astype(BF16)) * jnp.where(left, jnp.exp(c0), jnp.exp(c1))
            e0, e1 = cum[C - 1:C, h0:h0 + 1], cum[C - 1:C, h1:h1 + 1]
            wgt = (xdt * jnp.where(left, jnp.exp(e0 - c0), jnp.exp(e1 - c1))).astype(BF16)
            ht_ref[g * 4 + j] = jnp.where(left[0:1, :], jnp.exp(e0), jnp.exp(e1)) * htp + _dot(bst, wgt)
            ys.append(y + dsk[:, 128 * j:128 * (j + 1)] * xp)
        y = jnp.concatenate(ys, axis=1) * _silu(z_ref[g])
        ms = jnp.mean(y * y, axis=-1, keepdims=True)
        yb_ref[g] = (y * lax.rsqrt(ms + RMS_EPS) * nw_ref[g]).astype(BF16)
        return carry

    lax.fori_loop(0, SSD_G, group, 0)
    for g in range(SSD_G):
        y_ref[:, SSD_GW * g:SSD_GW * (g + 1)] = yb_ref[g]

    @pl.when(last_chunk)
    def _fin():
        for p in range(SSD_HEADS // 2):
            ho_ref[0, p] = ht_ref[p].T


def ssd_core(zx, bc, dt, conv_state, h0, params, *, row0, streams, length, chunk, y_prev=None):
    conv_w, conv_b, dt_bias, a_log, d_skip, norm_w = params
    S, L, C = streams, length, chunk
    nch = L // C
    rb0 = row0 // C
    t_all = zx.shape[1]
    dseg = dt[row0:row0 + S * L].reshape(S, L, SSD_G, SSD_HPG)
    dtg = dseg.transpose(0, 2, 1, 3)
    dtt = dseg.transpose(0, 2, 3, 1)

    def split(a, lead):
        ax = a[..., :SSD_INNER].reshape(lead + (SSD_G, SSD_GW))
        ab = a[..., SSD_INNER:SSD_INNER + SSD_G * SSD_N].reshape(lead + (SSD_G, SSD_N))
        ac = a[..., SSD_INNER + SSD_G * SSD_N:].reshape(lead + (SSD_G, SSD_N))
        return ax, ab, ac

    csx, csb, csc = (jnp.moveaxis(a, 2, 1) for a in split(conv_state, (S, SSD_CONV - 1)))
    cwx, cwb, cwc = (jnp.moveaxis(a, 1, 0) for a in split(conv_w, (SSD_CONV,)))
    cbx, cbb, cbc = (jnp.moveaxis(a, 1, 0) for a in split(conv_b.reshape(1, -1), (1,)))
    dtbr = dt_bias.reshape(SSD_G, 1, SSD_HPG)
    dtbc = dt_bias.reshape(SSD_G, SSD_HPG, 1)
    alr = a_log.reshape(SSD_G, 1, SSD_HPG)
    alc = a_log.reshape(SSD_G, SSD_HPG, 1)
    dsk = jnp.repeat(d_skip, SSD_INNER // SSD_HEADS).reshape(SSD_G, 1, SSD_GW)
    nw = norm_w.reshape(SSD_G, 1, SSD_GW)
    h0p = h0.reshape(S, SSD_HEADS // 2, 128, SSD_N)

    def rb(s, c):
        return rb0 + s * nch + c

    def full(a):
        nd = a.ndim
        return pl.BlockSpec(a.shape, lambda s, c: (0,) * nd)

    def per_stream(a):
        nd = a.ndim
        return pl.BlockSpec((1,) + a.shape[1:], lambda s, c: (s,) + (0,) * (nd - 1))

    in_specs = [
        pl.BlockSpec((SSD_G, C, SSD_GW), lambda s, c: (0, rb(s, c), 0)),
        pl.BlockSpec((SSD_G, C, SSD_GW), lambda s, c: (1, rb(s, c), 0)),
        pl.BlockSpec((SSD_G, C, SSD_N), lambda s, c: (0, rb(s, c), 0)),
        pl.BlockSpec((SSD_G, C, SSD_N), lambda s, c: (1, rb(s, c), 0)),
        pl.BlockSpec((1, SSD_G, C, SSD_HPG), lambda s, c: (s, 0, c, 0)),
        pl.BlockSpec((1, SSD_G, SSD_HPG, C), lambda s, c: (s, 0, 0, c)),
        per_stream(csx), per_stream(csb), per_stream(csc), per_stream(h0p),
        full(cwx), full(cwb), full(cwc), full(cbx), full(cbb), full(cbc),
        full(dtbr), full(dtbc), full(alr), full(alc), full(dsk), full(nw),
    ]
    args = [zx, zx, bc, bc, dtg, dtt, csx, csb, csc, h0p, cwx, cwb, cwc, cbx, cbb, cbc,
            dtbr, dtbc, alr, alc, dsk, nw]
    aliases = {}
    if y_prev is not None:
        in_specs.append(pl.BlockSpec(memory_space=pl.ANY))
        args.append(y_prev)
        aliases = {len(args) - 1: 0}
    out_shape = [
        jax.ShapeDtypeStruct((t_all, SSD_INNER), BF16),
        jax.ShapeDtypeStruct(csx.shape, F32), jax.ShapeDtypeStruct(csb.shape, F32),
        jax.ShapeDtypeStruct(csc.shape, F32), jax.ShapeDtypeStruct(h0p.shape, F32),
    ]
    out_specs = [
        pl.BlockSpec((C, SSD_INNER), lambda s, c: (rb(s, c), 0)),
        per_stream(csx), per_stream(csb), per_stream(csc), per_stream(h0p),
    ]
    scratch = [
        pltpu.VMEM((SSD_G, 8, SSD_GW), F32), pltpu.VMEM((SSD_G, 8, SSD_N), F32),
        pltpu.VMEM((SSD_G, 8, SSD_N), F32),
        pltpu.VMEM((C + 8, SSD_GW), F32), pltpu.VMEM((C + 8, SSD_N), F32), pltpu.VMEM((C + 8, SSD_N), F32),
        pltpu.VMEM((SSD_HEADS // 2, SSD_N, 128), F32),
        pltpu.VMEM((SSD_G, C, SSD_GW), BF16),
    ]
    y, cox, cob, coc, ho = pl.pallas_call(
        functools.partial(_ssd_body, C=C, aliased=y_prev is not None),
        grid=(S, nch), in_specs=in_specs, out_specs=out_specs, out_shape=out_shape,
        scratch_shapes=scratch, input_output_aliases=aliases,
        compiler_params=_cparams(("arbitrary", "arbitrary")),
        name="ssd_core",
    )(*args)
    conv_out = jnp.concatenate([jnp.moveaxis(a, 1, 2).reshape(S, SSD_CONV - 1, -1) for a in (cox, cob, coc)],
                               axis=-1)
    return y, conv_out, ho.reshape(S, SSD_HEADS, SSD_INNER // SSD_HEADS, SSD_N)


def _swa_body(sink_ref, q_ref, pk_ref, pv_ref, kv_ref, *rest, QT, prev_valid, aliased):
    o_ref = rest[-1]
    i = pl.program_id(0)
    kvw = SWA_KVH * SWA_DH
    nback = WINDOW // CHUNK
    rb = min(QT, 2 * CHUNK)
    span = rb + WINDOW
    kf = jnp.concatenate([pk_ref[...], kv_ref[:, :kvw]], axis=0).astype(BF16)
    vf = jnp.concatenate([pv_ref[...], kv_ref[:, kvw:]], axis=0).astype(BF16)
    r2 = lax.broadcasted_iota(jnp.int32, (128, 128), 0)
    c2 = lax.broadcasted_iota(jnp.int32, (128, 128), 1)
    swap = jnp.where((r2 + SWA_DH) % 128 == c2, 1.0, 0.0).astype(BF16)
    left = lax.broadcasted_iota(jnp.int32, (1, 128), 1) < SWA_DH
    k_side, v_side = [], []
    for kh in range(SWA_KVH):
        blk = slice(128 * (kh // 2), 128 * (kh // 2 + 1))
        mine = left if kh % 2 == 0 else ~left
        kb, vb = kf[:, blk], jnp.where(mine, vf[:, blk], 0.0).astype(BF16)
        ko, vo = _dot(kb, swap).astype(BF16), _dot(vb, swap).astype(BF16)
        k_side.append((kb, ko) if kh % 2 == 0 else (ko, kb))
        v_side.append((vb, vo) if kh % 2 == 0 else (vo, vb))
    scale = SWA_DH ** -0.5
    for b in range(QT // rb):
        rows = slice(rb * b, rb * (b + 1))
        keys = slice(rb * b, rb * b + span)
        qc = lax.broadcasted_iota(jnp.int32, (rb, span), 0) // CHUNK
        kc = lax.broadcasted_iota(jnp.int32, (rb, span), 1) // CHUNK
        ok = (kc >= qc) & (kc <= qc + nback)
        if not prev_valid and rb * b < WINDOW:
            ok = ok & ((kc + (rb // CHUNK) * b >= nback) | (i > 0))
        for pr in range(SWA_QH // 2):
            kh = 2 * pr // SWA_GRP
            q2 = q_ref[rows, 128 * pr:128 * (pr + 1)] * scale
            acc = None
            for side in range(2):
                qm = jnp.where(left if side == 0 else ~left, q2, 0.0).astype(BF16)
                s = jnp.where(ok, _dot_nt(qm, k_side[kh][side][keys]), NEG)
                sink = sink_ref[2 * pr + side]
                m = jnp.maximum(jnp.max(s, axis=-1, keepdims=True), sink)
                p = jnp.exp(s - m)
                den = jnp.sum(p, axis=-1, keepdims=True) + jnp.exp(sink - m)
                o = _dot(p.astype(BF16), v_side[kh][side][keys]) / den
                acc = o if acc is None else acc + o
            o_ref[rows, 128 * pr:128 * (pr + 1)] = acc.astype(BF16)


def swa_core(q, kv, prev_k, prev_v, sinks, *, row0, tiles, qt, prompt, o_prev=None):
    t_all = q.shape[0]
    rb0 = row0 // qt
    kvw = SWA_KVH * SWA_DH
    if prompt:
        wpt = qt // WINDOW
        prev_map_k = lambda i, s: (jnp.maximum(wpt * (rb0 + i) - 1, 0), 0)
        prev_map_v = lambda i, s: (jnp.maximum(wpt * (rb0 + i) - 1, 0), 1)
        pk_spec = pl.BlockSpec((WINDOW, kvw), prev_map_k)
        pv_spec = pl.BlockSpec((WINDOW, kvw), prev_map_v)
        prev_k = prev_v = kv
    else:
        pk_spec = pl.BlockSpec((None, WINDOW, kvw), lambda i, s: (i, 0, 0))
        pv_spec = pl.BlockSpec((None, WINDOW, kvw), lambda i, s: (i, 0, 0))
    in_specs = [pl.BlockSpec((qt, D), lambda i, s: (rb0 + i, 0)), pk_spec, pv_spec,
                pl.BlockSpec((qt, 2 * kvw), lambda i, s: (rb0 + i, 0))]
    args = [sinks, q, prev_k, prev_v, kv]
    aliases = {}
    if o_prev is not None:
        in_specs.append(pl.BlockSpec(memory_space=pl.ANY))
        args.append(o_prev)
        aliases = {len(args) - 1: 0}
    return pl.pallas_call(
        functools.partial(_swa_body, QT=qt, prev_valid=not prompt, aliased=o_prev is not None),
        grid_spec=pltpu.PrefetchScalarGridSpec(
            num_scalar_prefetch=1, grid=(tiles,), in_specs=in_specs,
            out_specs=pl.BlockSpec((qt, D), lambda i, s: (rb0 + i, 0))),
        out_shape=jax.ShapeDtypeStruct((t_all, D), BF16),
        input_output_aliases=aliases,
        compiler_params=_cparams(("arbitrary",)),
        name="swa_core",
    )(*args)


def _gelu_tanh(y):
    return 0.5 * y * (1.0 + jnp.tanh(0.7978845608028654 * (y + 0.044715 * y * y * y)))


def _s5_body(u_ref, wxr_ref, wxi_ref, wyr_ref, wyi_ref, kt_ref, a_ref, dsk_ref, hr0_ref, hi0_ref, *rest,
             S, R, aliased):
    if aliased:
        rest = rest[1:]
    (y_ref, hro_ref, hio_ref, wx_s, wy_s, ktm_s, xr_s, xi_s, pr_s, pi_s, hr_s, hi_s) = rest
    kb = pl.program_id(1)
    half = S5_CW * S5_N // S5_CH

    @pl.when((pl.program_id(0) == 0) & (kb == 0))
    def _zero():
        ktm_s[...] = jnp.zeros_like(ktm_s)

    @pl.when(kb == 0)
    def _build():
        own = (lax.broadcasted_iota(jnp.int32, (S5_CW, half), 0) // S5_CH
               == lax.broadcasted_iota(jnp.int32, (S5_CW, half), 1) // S5_N)
        for s in range(S5_SUB):
            rows = slice(S5_CW * s, S5_CW * (s + 1))
            wx_s[rows, 0:half] = jnp.where(own, jnp.concatenate([wxr_ref[s]] * 4, axis=1), 0.0).astype(BF16)
            wx_s[rows, half:2 * half] = jnp.where(own, jnp.concatenate([wxi_ref[s]] * 4, axis=1), 0.0).astype(BF16)
        own_t = (lax.broadcasted_iota(jnp.int32, (half, S5_CW), 0) // S5_N
                 == lax.broadcasted_iota(jnp.int32, (half, S5_CW), 1) // S5_CH)
        for t in range(S5_SUB):
            cols = slice(S5_CW * t, S5_CW * (t + 1))
            wy_s[0:half, cols] = jnp.where(own_t, jnp.concatenate([wyr_ref[t]] * 8, axis=0), 0.0).astype(BF16)
            wy_s[half:2 * half, cols] = jnp.where(own_t, jnp.concatenate([wyi_ref[t]] * 8, axis=0), 0.0).astype(BF16)
        same = (lax.broadcasted_iota(jnp.int32, (S5_CW, S5_CW), 0) // S5_CH
                == lax.broadcasted_iota(jnp.int32, (S5_CW, S5_CW), 1) // S5_CH)
        taps = [jnp.where(same, kt_ref[tau], 0.0).astype(BF16) for tau in range(S5_SUB)]
        for s in range(S5_SUB):
            for t in range(s, S5_SUB):
                ktm_s[S5_CW * s:S5_CW * (s + 1), S5_CW * t:S5_CW * (t + 1)] = taps[t - s]
        for st in range(S):
            hr_s[st] = hr0_ref[st]
            hi_s[st] = hi0_ref[st]

    kr = S * R
    ucat = jnp.concatenate([u_ref[pl.ds(s, kr, stride=S5_SUB), :] for s in range(S5_SUB)],
                           axis=1)
    ub = ucat.astype(BF16)
    x = _dot(ub, wx_s[...])
    xr_s[...] = x[:, :half]
    xi_s[...] = x[:, half:]
    ar = a_ref[0:1, :]
    ai = a_ref[1:2, :]
    for st in range(S):
        def step(k, carry):
            hr, hi = carry
            row = st * R + k
            pr_s[pl.ds(row, 1), :] = hr
            pi_s[pl.ds(row, 1), :] = hi
            nr = ar * hr - ai * hi + xr_s[pl.ds(row, 1), :]
            ni = ar * hi + ai * hr + xi_s[pl.ds(row, 1), :]
            return nr, ni

        hr, hi = lax.fori_loop(0, R, step, (hr_s[st], hi_s[st]))
        hr_s[st] = hr
        hi_s[st] = hi
    hprev = jnp.concatenate([pr_s[...], pi_s[...]], axis=1).astype(BF16)
    dsk = jnp.concatenate([dsk_ref[...]] * S5_SUB, axis=1)
    cb = 2 * S5_CW
    intra = jnp.concatenate([_dot(ub[:, :cb * (t + 1)], ktm_s[0:cb * (t + 1), cb * t:cb * (t + 1)])
                             for t in range(S5_SUB // 2)], axis=1)
    y = _gelu_tanh(intra + _dot(hprev, wy_s[...]) + dsk * ucat)
    for t in range(S5_SUB):
        y_ref[pl.ds(t, kr, stride=S5_SUB), :] = y[:, S5_CW * t:S5_CW * (t + 1)]

    @pl.when(kb == pl.num_programs(1) - 1)
    def _fin():
        for st in range(S):
            hro_ref[st] = hr_s[st]
            hio_ref[st] = hi_s[st]


def s5_tables(p):
    a_re, a_im, log_dt, b_re, b_im, c_re, c_im, d_skip = p
    lr, li = a_re.astype(F32), a_im.astype(F32)
    dt = jnp.exp(log_dt.astype(F32))[:, None]
    mag = jnp.exp(lr * dt)
    ab_r, ab_i = mag * jnp.cos(li * dt), mag * jnp.sin(li * dt)
    den = lr * lr + li * li
    co_r = ((ab_r - 1.0) * lr + ab_i * li) / den
    co_i = (ab_i * lr - (ab_r - 1.0) * li) / den
    bb_r = co_r[..., None] * b_re - co_i[..., None] * b_im
    bb_i = co_r[..., None] * b_im + co_i[..., None] * b_re
    pw_r, pw_i = [jnp.ones_like(ab_r)], [jnp.zeros_like(ab_i)]
    for _ in range(S5_SUB):
        r, i = pw_r[-1], pw_i[-1]
        pw_r.append(ab_r * r - ab_i * i)
        pw_i.append(ab_r * i + ab_i * r)
    pr, pi = jnp.stack(pw_r, 0), jnp.stack(pw_i, 0)
    er, ei = pr[S5_SUB - 1::-1][:S5_SUB], pi[S5_SUB - 1::-1][:S5_SUB]
    wx_r = er[..., None] * bb_r[None] - ei[..., None] * bb_i[None]
    wx_i = er[..., None] * bb_i[None] + ei[..., None] * bb_r[None]
    gb, gw = S5_G // S5_GB, S5_GB
    wx_r = wx_r.reshape(S5_SUB, gb, gw, S5_N, S5_CH).transpose(1, 0, 2, 4, 3).reshape(gb, S5_SUB, S5_CW, S5_N)
    wx_i = wx_i.reshape(S5_SUB, gb, gw, S5_N, S5_CH).transpose(1, 0, 2, 4, 3).reshape(gb, S5_SUB, S5_CW, S5_N)
    wxr = jnp.concatenate([wx_r, wx_r], axis=-1)
    wxi = jnp.concatenate([wx_i, wx_i], axis=-1)
    qr, qi = pr[1:], pi[1:]
    cr, ci = c_re.astype(F32), c_im.astype(F32)
    wy_r = cr[None] * qr[:, :, None, :] - ci[None] * qi[:, :, None, :]
    wy_i = -(cr[None] * qi[:, :, None, :] + ci[None] * qr[:, :, None, :])
    wyr = wy_r.reshape(S5_SUB, gb, gw, S5_CH, S5_N).transpose(1, 0, 4, 2, 3).reshape(gb, S5_SUB, S5_N, S5_CW)
    wyi = wy_i.reshape(S5_SUB, gb, gw, S5_CH, S5_N).transpose(1, 0, 4, 2, 3).reshape(gb, S5_SUB, S5_N, S5_CW)
    tr = pr[:S5_SUB, :, None, :] * cr[None] - pi[:S5_SUB, :, None, :] * ci[None]
    ti = pr[:S5_SUB, :, None, :] * ci[None] + pi[:S5_SUB, :, None, :] * cr[None]
    taps = jnp.einsum('agjn,gnk->agjk', tr, bb_r) - jnp.einsum('agjn,gnk->agjk', ti, bb_i)
    taps = taps.reshape(S5_SUB, gb, gw, S5_CH, S5_CH).transpose(1, 0, 2, 4, 3)
    kt = jnp.tile(taps.reshape(gb, S5_SUB, S5_CW, S5_CH), (1, 1, 1, gw))
    a16 = jnp.stack([pr[S5_SUB].reshape(gb, gw * S5_N), pi[S5_SUB].reshape(gb, gw * S5_N)], axis=1)
    dsk = d_skip.astype(F32).reshape(gb, 1, S5_CW)
    return wxr, wxi, wyr, wyi, kt, a16, dsk


def s5_core(u, s_re, s_im, tables, *, row0, streams, length, y_prev=None):
    wxr, wxi, wyr, wyi, kt, a16, dsk = tables
    S, L = streams, length
    gb = S5_G // S5_GB
    rows = S * L // S5_SUB
    if S == 1:
        kr = min(S5_KR, rows)
        spb, rps = 1, kr
    else:
        kr = rows
        spb, rps = S, L // S5_SUB
    nkb = rows // kr
    kb0 = row0 // S5_SUB // kr
    half = S5_GB * S5_N
    hr0 = s_re.reshape(S, gb, 1, half)
    hi0 = s_im.reshape(S, gb, 1, half)
    tab = lambda a: pl.BlockSpec((None,) + a.shape[1:], lambda p, k: (p,) + (0,) * (a.ndim - 1))
    st = pl.BlockSpec((S, None, 1, half), lambda p, k: (0, p, 0, 0))
    uspec = pl.BlockSpec((kr * S5_SUB, S5_CW), lambda p, k: (kb0 + k, p))
    in_specs = [uspec, tab(wxr), tab(wxi), tab(wyr), tab(wyi), tab(kt), tab(a16), tab(dsk), st, st]
    args = [u, wxr, wxi, wyr, wyi, kt, a16, dsk, hr0, hi0]
    aliases = {}
    if y_prev is not None:
        in_specs.append(pl.BlockSpec(memory_space=pl.ANY))
        args.append(y_prev)
        aliases = {len(args) - 1: 0}
    wide = S5_SUB * S5_CW
    y, hro, hio = pl.pallas_call(
        functools.partial(_s5_body, S=spb, R=rps, aliased=y_prev is not None),
        grid=(gb, nkb),
        in_specs=in_specs,
        out_specs=[uspec, st, st],
        out_shape=[jax.ShapeDtypeStruct(u.shape, F32), jax.ShapeDtypeStruct(hr0.shape, F32),
                   jax.ShapeDtypeStruct(hi0.shape, F32)],
        scratch_shapes=[pltpu.VMEM((wide, 2 * half), BF16), pltpu.VMEM((2 * half, wide), BF16),
                        pltpu.VMEM((wide, wide), BF16)]
        + [pltpu.VMEM((kr, half), F32) for _ in range(4)]
        + [pltpu.VMEM((spb, 1, half), F32) for _ in range(2)],
        input_output_aliases=aliases,
        compiler_params=_cparams(("arbitrary", "arbitrary")),
        name="s5_core",
    )(*args)
    return y, hro.reshape(S, S5_G, S5_N), hio.reshape(S, S5_G, S5_N)


def _hgrn_body(q_ref, f_ref, i_ref, g_ref, lbp_ref, nw_ref, s0_ref, *rest, C, layer, aliased):
    if aliased:
        rest = rest[1:]
    o_ref, so_ref, st_ref = rest
    c = pl.program_id(1)

    @pl.when(c == 0)
    def _init():
        for h in range(HG_H):
            st_ref[h] = s0_ref[0, h].T

    lbp = lbp_ref[...]
    e = jnp.exp(lbp - jnp.max(lbp, axis=0, keepdims=True))
    lbs = e / jnp.sum(e, axis=0, keepdims=True)
    lb = jnp.zeros((1, D), F32)
    for r in range(1, layer + 1):
        lb = lb + lbs[r:r + 1, :]
    fz = f_ref[...]
    log_sig = jnp.minimum(fz, 0.0) - jnp.log(1.0 + jnp.exp(-jnp.abs(fz)))
    la = jnp.log(lb)
    lbb = jnp.log(1.0 - lb) + log_sig
    mx = jnp.maximum(la, lbb)
    logf = mx + jnp.log(1.0 + jnp.exp(-jnp.abs(la - lbb)))
    kk = 1.0 - jnp.exp(logf)
    qs = _silu(q_ref[...])
    tril = _tri(C)
    cum = _dot(tril.astype(F32), logf, precision=HIGHEST)
    row = lax.broadcasted_iota(jnp.int32, (C, C), 0)
    col = lax.broadcasted_iota(jnp.int32, (C, C), 1)

    levels = []
    b = C
    while b >= HG_LEAF:
        nb = C // b
        ref = jnp.broadcast_to(cum.reshape(nb, b, D)[:, b // 2 - 1:b // 2, :], (nb, b, D)).reshape(C, D)
        ex = cum - ref
        same = (row // b) == (col // b)
        if b == HG_LEAF:
            mask = same & (col <= row)
            qe, ke = jnp.exp(jnp.minimum(ex, 80.0)), jnp.exp(jnp.minimum(-ex, 80.0))
        else:
            mask = same & ((row % b) >= b // 2) & ((col % b) < b // 2)
            e = jnp.exp(-jnp.abs(ex))
            qe, ke = jnp.where(ex <= 0.0, e, 1.0), jnp.where(ex <= 0.0, 1.0, e)
        levels.append((mask, (qs * qe).astype(BF16), (kk * ke).astype(BF16)))
        b //= 2

    last = cum[C - 1:C, :]
    qin = (qs * jnp.exp(cum)).astype(BF16)
    kin = (kk * jnp.exp(last - cum)).astype(BF16)
    dec = jnp.exp(last)
    vv = i_ref[...]
    vb = vv.astype(BF16)
    gate = _silu(g_ref[...])
    nw = nw_ref[...]
    outs = []
    for h in range(HG_H):
        sl = slice(HG_K * h, HG_K * (h + 1))
        att = jnp.zeros((C, C), F32)
        for mask, ql, kl in levels:
            att = att + jnp.where(mask, _dot_nt(ql[:, sl], kl[:, sl]), 0.0)
        st = st_ref[h]
        o = _dot(att.astype(BF16), vb[:, sl]) + _dot_nt(qin[:, sl], st.astype(BF16))
        st_ref[h] = st * dec[:, sl] + _dot(_transpose_rows(vv[:, sl]).astype(BF16), kin[:, sl])
        ms = jnp.mean(o * o, axis=-1, keepdims=True)
        outs.append(o * lax.rsqrt(ms + RMS_EPS) * nw)
    o_ref[...] = (jnp.concatenate(outs, axis=1) * gate).astype(BF16)

    @pl.when(c == pl.num_programs(1) - 1)
    def _fin():
        for h in range(HG_H):
            so_ref[0, h] = st_ref[h].T


def hgrn_core(qfig, lb_param, norm_w, s0, *, layer, row0, streams, length, chunk, o_prev=None):
    S, L, C = streams, length, chunk
    nch = L // C
    rb0 = row0 // C
    t_all = qfig.shape[0]
    rowspec = lambda j: pl.BlockSpec((C, D), lambda s, c: (rb0 + s * nch + c, j))
    in_specs = [rowspec(0), rowspec(1), rowspec(2), rowspec(3),
                pl.BlockSpec((DEPTH, D), lambda s, c: (0, 0)),
                pl.BlockSpec((1, HG_K), lambda s, c: (0, 0)),
                pl.BlockSpec((1, HG_H, HG_K, HG_K), lambda s, c: (s, 0, 0, 0))]
    args = [qfig, qfig, qfig, qfig, lb_param, norm_w.reshape(1, HG_K), s0]
    aliases = {}
    if o_prev is not None:
        in_specs.append(pl.BlockSpec(memory_space=pl.ANY))
        args.append(o_prev)
        aliases = {len(args) - 1: 0}
    return pl.pallas_call(
        functools.partial(_hgrn_body, C=C, layer=layer, aliased=o_prev is not None),
        grid=(S, nch), in_specs=in_specs,
        out_specs=[rowspec(0), pl.BlockSpec((1, HG_H, HG_K, HG_K), lambda s, c: (s, 0, 0, 0))],
        out_shape=[jax.ShapeDtypeStruct((t_all, D), BF16), jax.ShapeDtypeStruct(s0.shape, F32)],
        scratch_shapes=[pltpu.VMEM((HG_H, HG_K, HG_K), F32)],
        input_output_aliases=aliases,
        compiler_params=_cparams(("arbitrary", "arbitrary")),
        name="hgrn_core",
    )(*args)


def _expert_body(te_ref, nt_ref, x_ref, wg_ref, wu_ref, wd_ref, o_ref, wgb_ref, wub_ref, wdb_ref):
    i = pl.program_id(0)
    prev = te_ref[jnp.maximum(i - 1, 0)]

    @pl.when((i == 0) | (te_ref[i] != prev))
    def _():
        wgb_ref[...] = wg_ref[0, 0].astype(BF16)
        wub_ref[...] = wu_ref[0, 0].astype(BF16)
        wdb_ref[...] = wd_ref[0, 0].astype(BF16)

    @pl.when(i < nt_ref[0])
    def _():
        lo, hi = _unpack_halves(x_ref[...])
        lo, hi = lo.astype(BF16), hi.astype(BF16)
        k = D // 2
        gate = _dot(lo, wgb_ref[0:k, :]) + _dot(hi, wgb_ref[k:D, :])
        up = _dot(lo, wub_ref[0:k, :]) + _dot(hi, wub_ref[k:D, :])
        o_ref[...] = _pack_halves(_dot((_silu(gate) * up).astype(BF16), wdb_ref[...]))

    @pl.when(i >= nt_ref[0])
    def _():
        o_ref[...] = jnp.zeros_like(o_ref)


def expert_mlp(xs, tile_expert, n_tiles, w_gate, w_up, w_down, layer):
    p = xs.shape[0]
    nt = p // MOE_TM
    wspec = lambda shp: pl.BlockSpec((1, 1) + shp, lambda i, te, n: (layer, te[i], 0, 0))
    return pl.pallas_call(
        _expert_body,
        grid_spec=pltpu.PrefetchScalarGridSpec(
            num_scalar_prefetch=2, grid=(nt,),
            in_specs=[pl.BlockSpec((MOE_TM, D // 2), lambda i, te, n: (i, 0)),
                      wspec((D, D_EXPERT)), wspec((D, D_EXPERT)), wspec((D_EXPERT, D))],
            out_specs=pl.BlockSpec((MOE_TM, D // 2), lambda i, te, n: (i, 0)),
            scratch_shapes=[pltpu.VMEM((D, D_EXPERT), BF16), pltpu.VMEM((D, D_EXPERT), BF16),
                            pltpu.VMEM((D_EXPERT, D), BF16)]),
        out_shape=jax.ShapeDtypeStruct((p, D // 2), jnp.uint32),
        compiler_params=_cparams(("arbitrary",)),
        name="expert_mlp",
    )(tile_expert, n_tiles, xs, w_gate, w_up, w_down)


def moe_layer(x1, route_t, counts, w_gate, w_up, w_down, layer):
    t = x1.shape[0]
    e0 = route_t[0].astype(jnp.int32)
    e1 = route_t[1].astype(jnp.int32)
    cnt = counts[0, MOE_G:MOE_G + MOE_E].astype(jnp.int32)
    padded = (cnt + MOE_TM - 1) // MOE_TM * MOE_TM
    ends = jnp.cumsum(padded)
    starts = ends - padded
    d0 = starts[e0] + route_t[4].astype(jnp.int32)
    d1 = starts[e1] + route_t[5].astype(jnp.int32)
    p_rows = (2 * t + MOE_E * (MOE_TM - 1)) // 512 * 512 + 512
    nt = p_rows // MOE_TM
    tok = jnp.arange(t, dtype=jnp.int32)
    tile_start = jnp.arange(nt, dtype=jnp.int32) * MOE_TM
    tile_expert = jnp.minimum(jnp.sum((tile_start[:, None] >= ends[None, :]).astype(jnp.int32), axis=1),
                              MOE_E - 1).astype(jnp.int32)
    n_tiles = (ends[-1] // MOE_TM).astype(jnp.int32).reshape(1)
    _, by_pos = lax.sort_key_val(jnp.concatenate([d0, d1]), jnp.concatenate([tok, tok]))
    per_pos = lambda a: jnp.repeat(a[tile_expert], MOE_TM)
    pos = jnp.arange(p_rows, dtype=jnp.int32)
    off = pos - per_pos(starts)
    dense = jnp.minimum(per_pos(jnp.cumsum(cnt) - cnt) + off, 2 * t - 1)
    src = jnp.where(off < per_pos(cnt), by_pos.at[dense].get(mode="promise_in_bounds"), pos % t)
    xs = x1.at[src].get(mode="promise_in_bounds")
    ys = expert_mlp(xs, tile_expert, n_tiles, w_gate, w_up, w_down, layer)
    return ys.at[d0].get(mode="promise_in_bounds"), ys.at[d1].get(mode="promise_in_bounds")


def _forward(x_prompt, x_sample, states, ssd_p, swa_p, s5_p, hg_p, ln_p, moe_p):
    (state_ssd_conv, state_ssd, cache_k, cache_v, s5_re, s5_im, state_hgrn) = states
    lp = x_prompt.shape[1]
    sb, ls = x_sample.shape[0], x_sample.shape[1]
    t_all = lp + sb * ls
    ln1_g, ln1_b, ln2_g, ln2_b = ln_p
    w_rg, b_rg, w_re, b_re, w_gate, w_up, w_down = moe_p

    x = (x_prompt.reshape(lp, D), x_sample.reshape(sb * ls, D))
    xb = jnp.concatenate([x[0].astype(BF16), x[1].astype(BF16)], axis=0)
    outs = {}
    for layer in range(DEPTH):
        kind = layer % 4
        if kind == 0:
            w_in, conv_w, conv_b, dt_bias, a_log, d_skip, norm_w, w_out = ssd_p
            zx = matmul(xb, w_in, col0=0, ncols=2 * SSD_INNER, tn=MM_TN, tiled_out=True, sub=SSD_GW,
                        name="mm_ssd_zx")
            bc = matmul(xb, w_in, col0=2 * SSD_INNER, ncols=2 * SSD_G * SSD_N, tn=MM_TN, tiled_out=True,
                        sub=SSD_N, name="mm_ssd_bc")
            dt = matmul(xb, w_in[:, SSD_INNER + SSD_XBC:], tn=SSD_HEADS, name="mm_ssd_dt")
            prm = (conv_w, conv_b, dt_bias, a_log, d_skip, norm_w)
            zc = jnp.zeros((1, SSD_CONV - 1, SSD_XBC), F32)
            zh = jnp.zeros((1, SSD_HEADS, SSD_INNER // SSD_HEADS, SSD_N), F32)
            y, pc, ph = ssd_core(zx, bc, dt, zc, zh, prm, row0=0, streams=1, length=lp, chunk=128)
            y, sc, sh = ssd_core(zx, bc, dt, state_ssd_conv, state_ssd, prm, row0=lp, streams=sb,
                                 length=ls, chunk=ls, y_prev=y)
            outs['conv'], outs['ssd'] = (pc, sc), (ph, sh)
            h = matmul(y, w_out, name="mm_ssd_out")
        elif kind == 1:
            w_qkv, sinks, w_out = swa_p
            kvw = SWA_KVH * SWA_DH
            q = matmul(xb, w_qkv, col0=0, ncols=D, out_dtype=BF16, name="mm_swa_q")
            kv = matmul(xb, w_qkv, col0=D, ncols=2 * kvw, name="mm_swa_kv")
            o = swa_core(q, kv, None, None, sinks, row0=0, tiles=lp // 256, qt=256, prompt=True)
            ck = cache_k.reshape(sb, WINDOW, kvw)
            cv = cache_v.reshape(sb, WINDOW, kvw)
            o = swa_core(q, kv, ck, cv, sinks, row0=lp, tiles=sb, qt=ls, prompt=False, o_prev=o)
            kshape = (SWA_KVH, SWA_DH)
            pk = kv[lp - WINDOW:lp, :kvw].reshape((1, WINDOW) + kshape)
            pv = kv[lp - WINDOW:lp, kvw:].reshape((1, WINDOW) + kshape)
            kvs = kv[lp:].reshape(sb, ls, 2 * kvw)
            sk = jnp.concatenate([ck, kvs[:, :, :kvw]], axis=1)[:, -WINDOW:].reshape((sb, WINDOW) + kshape)
            sv = jnp.concatenate([cv, kvs[:, :, kvw:]], axis=1)[:, -WINDOW:].reshape((sb, WINDOW) + kshape)
            outs['k'], outs['v'] = (pk, sk), (pv, sv)
            h = matmul(o, w_out, name="mm_swa_out")
        elif kind == 2:
            w_in, w_glu = s5_p[0], s5_p[-1]
            tables = s5_tables(s5_p[1:-1])
            u = matmul(xb, w_in, name="mm_s5_in")
            zs = jnp.zeros((1, S5_G, S5_N), F32)
            y, pr, pi = s5_core(u, zs, zs, tables, row0=0, streams=1, length=lp)
            y, sr, si = s5_core(u, s5_re, s5_im, tables, row0=lp, streams=sb, length=ls, y_prev=y)
            outs['s5r'], outs['s5i'] = (pr, sr), (pi, si)
            h = glu_matmul(y, w_glu)
        else:
            w_in, lb_param, norm_w, w_out = hg_p
            qfig = matmul(xb, w_in, name="mm_hg_in")
            zs = jnp.zeros((1, HG_H, HG_K, HG_K), F32)
            o, ps = hgrn_core(qfig, lb_param, norm_w, zs, layer=layer, row0=0, streams=1, length=lp, chunk=128)
            o, ss = hgrn_core(qfig, lb_param, norm_w, state_hgrn, layer=layer, row0=lp, streams=sb,
                              length=ls, chunk=ls, o_prev=o)
            outs['hg'] = (ps, ss)
            h = matmul(o, w_out, name="mm_hg_out")
        wr = jnp.concatenate([w_rg[layer], w_re[layer], jnp.zeros((D, 128 - MOE_G - MOE_E), F32)], axis=1)
        br = jnp.concatenate([b_rg[layer], b_re[layer], jnp.zeros((128 - MOE_G - MOE_E,), F32)]).reshape(1, 128)
        x1, x1p, route, route_t, counts = ln_route(x, h, ln1_g[layer], ln1_b[layer], wr, br)
        y0, y1 = moe_layer(x1p, route_t, counts, w_gate, w_up, w_down, layer)
        x, xb = ln_combine(x1, y0, y1, route, ln2_g[layer], ln2_b[layer],
                           split=lp if layer == DEPTH - 1 else None)
    y_prompt = x.reshape(1, lp, D)
    y_sample = xb.reshape(sb, ls, D)
    order = ('conv', 'ssd', 'k', 'v', 's5r', 's5i', 'hg')
    return (y_prompt, y_sample) + tuple(outs[k][0] for k in order) + tuple(outs[k][1] for k in order)


def kernel(x_prompt, x_sample, state_ssd_conv, state_ssd, cache_swa_k, cache_swa_v, state_s5_re, state_s5_im, state_hgrn, ssd_w_in, ssd_conv_w, ssd_conv_b, ssd_dt_bias, ssd_a_log, ssd_d, ssd_norm_w, ssd_w_out, swa_w_qkv, swa_sinks, swa_w_out, s5_w_in, s5_a_re, s5_a_im, s5_log_dt, s5_b_re, s5_b_im, s5_c_re, s5_c_im, s5_d, s5_w_glu, hg_w_in, hg_lb, hg_norm_w, hg_w_out, ln1_g, ln1_b, ln2_g, ln2_b, moe_w_rg, moe_b_rg, moe_w_re, moe_b_re, moe_w_gate, moe_w_up, moe_w_down):
    states = (state_ssd_conv, state_ssd, cache_swa_k, cache_swa_v, state_s5_re, state_s5_im, state_hgrn)
    ssd_p = (ssd_w_in, ssd_conv_w, ssd_conv_b, ssd_dt_bias, ssd_a_log, ssd_d, ssd_norm_w, ssd_w_out)
    swa_p = (swa_w_qkv, swa_sinks, swa_w_out)
    s5_p = (s5_w_in, s5_a_re, s5_a_im, s5_log_dt, s5_b_re, s5_b_im, s5_c_re, s5_c_im, s5_d, s5_w_glu)
    hg_p = (hg_w_in, hg_lb, hg_norm_w, hg_w_out)
    ln_p = (ln1_g, ln1_b, ln2_g, ln2_b)
    moe_p = (moe_w_rg, moe_b_rg, moe_w_re, moe_b_re, moe_w_gate, moe_w_up, moe_w_down)
    return _forward(x_prompt, x_sample, states, ssd_p, swa_p, s5_p, hg_p, ln_p, moe_p)
```

```python
import functools

import jax
import jax.numpy as jnp
from jax import lax
from jax.experimental import pallas as pl
from jax.experimental.pallas import tpu as pltpu

F32 = jnp.float32
BF16 = jnp.bfloat16
HIGHEST = lax.Precision.HIGHEST

D = 2048
DEPTH = 4
DN_ALPHA = (2 * DEPTH) ** 0.25
LN_EPS = 1e-5
RMS_EPS = 1e-6
NEG = -1e30

VMEM_LIMIT = 56 * 1024 * 1024
MM_TN = 1024
MM_TM = 768

SSD_INNER = 4096
SSD_HEADS = 64
SSD_G = 8
SSD_HPG = 8
SSD_N = 128
SSD_GW = SSD_INNER // SSD_G
SSD_CONV = 4
SSD_XBC = SSD_INNER + 2 * SSD_G * SSD_N

SWA_DH = 64
SWA_QH = 32
SWA_KVH = 4
SWA_GRP = SWA_QH // SWA_KVH
WINDOW = 128
CHUNK = 64

S5_G = 128
S5_CH = 16
S5_N = 64
S5_SUB = 16
S5_GB = 8
S5_CW = S5_GB * S5_CH
S5_KR = 256

HG_H = 16
HG_K = 128
HG_LEAF = 16

MOE_G = 4
MOE_PG = 8
MOE_E = 32
D_EXPERT = 256
MOE_TM = 512


def _cparams(sem):
    return pltpu.CompilerParams(dimension_semantics=sem, vmem_limit_bytes=VMEM_LIMIT)


def _sigmoid(x):
    return 1.0 / (1.0 + jnp.exp(-x))


def _silu(x):
    return x * _sigmoid(x)


def _softplus(x):
    return jnp.maximum(x, 0.0) + jnp.log(1.0 + jnp.exp(-jnp.abs(x)))


def _dot(a, b, precision=None):
    return jnp.dot(a, b, preferred_element_type=F32, precision=precision)


def _dot_nt(a, b):
    return lax.dot_general(a, b, (((1,), (1,)), ((), ())), preferred_element_type=F32)


def _tri(n, upper=False):
    r = lax.broadcasted_iota(jnp.int32, (n, n), 0)
    c = lax.broadcasted_iota(jnp.int32, (n, n), 1)
    return (r <= c) if upper else (c <= r)


def _transpose_rows(x):
    c = x.shape[0]
    if c == 128:
        return x.T
    pad = jnp.zeros((128 - c, 128), x.dtype)
    return jnp.concatenate([x, pad], axis=0).T[:, :c]


def _pack_halves(x):
    n = x.shape[1] // 2
    lo = lax.bitcast_convert_type(x[:, :n].astype(BF16).astype(F32), jnp.uint32)
    hi = lax.bitcast_convert_type(x[:, n:].astype(BF16).astype(F32), jnp.uint32)
    return (lo >> 16) | hi


def _unpack_halves(w):
    lo = lax.bitcast_convert_type(w << 16, F32)
    hi = lax.bitcast_convert_type(w & jnp.uint32(0xFFFF0000), F32)
    return lo, hi


def _mm_body(x_ref, w_ref, o_ref, wb_ref, *, tiled_out):
    @pl.when(pl.program_id(1) == 0)
    def _():
        wb_ref[...] = w_ref[...].astype(BF16)

    r = _dot(x_ref[...], wb_ref[...]).astype(o_ref.dtype)
    if tiled_out:
        sub = o_ref.shape[2]
        for q in range(o_ref.shape[0]):
            o_ref[q] = r[:, sub * q:sub * (q + 1)]
    else:
        o_ref[...] = r


def _row_tile(m):
    return next(t for t in (MM_TM, 512, 256, 128, 64, 8) if m % t == 0)


def matmul(x, w, *, col0=0, ncols=None, tn=None, out_dtype=F32, tiled_out=False, sub=None, name="matmul"):
    m, k = x.shape
    ncols = w.shape[1] - col0 if ncols is None else ncols
    tm = _row_tile(m)
    if tn is None:
        tn = MM_TN if k * MM_TN * 4 <= 8 * 1024 * 1024 else MM_TN // 2
        tn = min(tn, ncols)
    assert col0 % tn == 0 and ncols % tn == 0 and m % tm == 0
    nj = ncols // tn
    j0 = col0 // tn
    if tiled_out:
        sub = tn if sub is None else sub
        out_shape = jax.ShapeDtypeStruct((ncols // sub, m, sub), out_dtype)
        out_spec = pl.BlockSpec((tn // sub, tm, sub), lambda j, i: (j, i, 0))
    else:
        out_shape = jax.ShapeDtypeStruct((m, ncols), out_dtype)
        out_spec = pl.BlockSpec((tm, tn), lambda j, i: (i, j))
    return pl.pallas_call(
        functools.partial(_mm_body, tiled_out=tiled_out),
        grid=(nj, m // tm),
        in_specs=[pl.BlockSpec((tm, k), lambda j, i: (i, 0)),
                  pl.BlockSpec((k, tn), lambda j, i: (0, j + j0))],
        out_specs=out_spec,
        out_shape=out_shape,
        scratch_shapes=[pltpu.VMEM((k, tn), BF16)],
        compiler_params=_cparams(("arbitrary", "arbitrary")),
        name=name,
    )(x, w)


def _glu_body(x_ref, wv_ref, wg_ref, o_ref, wvb_ref, wgb_ref):
    @pl.when(pl.program_id(1) == 0)
    def _():
        wvb_ref[...] = wv_ref[...].astype(BF16)
        wgb_ref[...] = wg_ref[...].astype(BF16)

    x = x_ref[...].astype(BF16)
    o_ref[...] = _dot(x, wvb_ref[...]) * _sigmoid(_dot(x, wgb_ref[...]))


def glu_matmul(x, w, *, tn=512):
    m, k = x.shape
    n = w.shape[1] // 2
    tm = _row_tile(m)
    nj = n // tn
    return pl.pallas_call(
        _glu_body,
        grid=(nj, m // tm),
        in_specs=[pl.BlockSpec((tm, k), lambda j, i: (i, 0)),
                  pl.BlockSpec((k, tn), lambda j, i: (0, j)),
                  pl.BlockSpec((k, tn), lambda j, i: (0, j + nj))],
        out_specs=pl.BlockSpec((tm, tn), lambda j, i: (i, j)),
        out_shape=jax.ShapeDtypeStruct((m, n), F32),
        scratch_shapes=[pltpu.VMEM((k, tn), BF16), pltpu.VMEM((k, tn), BF16)],
        compiler_params=_cparams(("arbitrary", "arbitrary")),
        name="glu_matmul",
    )(x, w, w)


def _cast_rows_body(a_ref, b_ref, o_ref, *, split_tiles):
    o_ref[...] = jnp.where(pl.program_id(0) < split_tiles, a_ref[...], b_ref[...]).astype(BF16)


def rows_to_bf16(xa, xb, tm=256):
    st = xa.shape[0] // tm
    m = xa.shape[0] + xb.shape[0]
    return pl.pallas_call(
        functools.partial(_cast_rows_body, split_tiles=st),
        grid=(m // tm,),
        in_specs=[pl.BlockSpec((tm, D), lambda i: (jnp.minimum(i, st - 1), 0)),
                  pl.BlockSpec((tm, D), lambda i: (jnp.maximum(i - st, 0), 0))],
        out_specs=pl.BlockSpec((tm, D), lambda i: (i, 0)),
        out_shape=jax.ShapeDtypeStruct((m, D), BF16),
        compiler_params=_cparams(("arbitrary",)),
        name="rows_to_bf16",
    )(xa, xb)


def _layer_norm(v, g, b):
    mu = jnp.mean(v, axis=-1, keepdims=True)
    vc = v - mu
    var = jnp.mean(vc * vc, axis=-1, keepdims=True)
    return vc * lax.rsqrt(var + LN_EPS) * g + b


def _route(x, wh, wl, br, carry):
    xh = x.astype(BF16)
    xl = (x - xh.astype(F32)).astype(BF16)
    lg = _dot(xh, wh) + (_dot(xh, wl) + _dot(xl, wh)) + br
    tm = lg.shape[0]
    lane = lax.broadcasted_iota(jnp.int32, lg.shape, 1)
    lanef = lane.astype(F32)
    big = jnp.float32(1e9)
    is_g = lane < MOE_G
    gl = jnp.where(is_g, lg, NEG)
    gmax = jnp.max(gl, axis=-1, keepdims=True)
    gsel = jnp.min(jnp.where(is_g & (gl == gmax), lanef, big), axis=-1, keepdims=True)
    gprob = 1.0 / jnp.sum(jnp.where(is_g, jnp.exp(gl - gmax), 0.0), axis=-1, keepdims=True)
    lo = MOE_G + MOE_PG * gsel
    is_e = (lanef >= lo) & (lanef < lo + MOE_PG)
    el = jnp.where(is_e, lg, NEG)
    m1 = jnp.max(el, axis=-1, keepdims=True)
    l1 = jnp.min(jnp.where(is_e & (el == m1), lanef, big), axis=-1, keepdims=True)
    is_e2 = is_e & (lanef != l1)
    el2 = jnp.where(is_e2, lg, NEG)
    m2 = jnp.max(el2, axis=-1, keepdims=True)
    l2 = jnp.min(jnp.where(is_e2 & (el2 == m2), lanef, big), axis=-1, keepdims=True)
    r = jnp.exp(m2 - m1)
    w1 = gprob / (1.0 + r)
    w2 = gprob * r / (1.0 + r)
    hit1 = lanef == l1
    hit2 = lanef == l2
    oh = jnp.where(hit1 | hit2, 1.0, 0.0)
    rr = lax.broadcasted_iota(jnp.int32, (tm, tm), 0)
    cc = lax.broadcasted_iota(jnp.int32, (tm, tm), 1)
    before = _dot(jnp.where(cc < rr, 1.0, 0.0).astype(BF16), oh.astype(BF16)) + carry
    k1 = jnp.sum(jnp.where(hit1, before, 0.0), axis=-1, keepdims=True)
    k2 = jnp.sum(jnp.where(hit2, before, 0.0), axis=-1, keepdims=True)
    table = jnp.where(lane == 0, l1 - MOE_G,
                      jnp.where(lane == 1, l2 - MOE_G,
                                jnp.where(lane == 2, w1,
                                          jnp.where(lane == 3, w2,
                                                    jnp.where(lane == 4, k1, jnp.where(lane == 5, k2, 0.0))))))
    return table, carry + jnp.sum(oh, axis=0, keepdims=True)


def _ln_route_body(*refs, split_tiles):
    if split_tiles is None:
        x_ref, refs = refs[0], refs[1:]
        x = x_ref[...]
    else:
        (xa_ref, xb_ref), refs = refs[:2], refs[2:]
        x = jnp.where(pl.program_id(0) < split_tiles, xa_ref[...], xb_ref[...])
    h_ref, g_ref, b_ref, wh_ref, wl_ref, br_ref, o_ref, op_ref, r_ref, rt_ref, cnt_ref = refs

    @pl.when(pl.program_id(0) == 0)
    def _():
        cnt_ref[...] = jnp.zeros_like(cnt_ref)

    y = _layer_norm(DN_ALPHA * x + h_ref[...], g_ref[...], b_ref[...])
    o_ref[...] = y
    op_ref[...] = _pack_halves(y)
    table, cnt_ref[...] = _route(y, wh_ref[...], wl_ref[...], br_ref[...], cnt_ref[...])
    r_ref[...] = table
    rt_ref[...] = jnp.concatenate([table[128 * q:128 * (q + 1)].T[:8] for q in range(table.shape[0] // 128)],
                                  axis=1)


def ln_route(x, h, g, b, wr, br, tm=256):
    m = h.shape[0]
    row = pl.BlockSpec((tm, D), lambda i: (i, 0))
    vec = pl.BlockSpec((1, D), lambda i: (0, 0))
    one = pl.BlockSpec((1, 128), lambda i: (0, 0))
    wmat = pl.BlockSpec((D, 128), lambda i: (0, 0))
    wh = wr.astype(BF16)
    wl = (wr - wh.astype(F32)).astype(BF16)
    if isinstance(x, tuple):
        st = x[0].shape[0] // tm
        xs = list(x)
        x_specs = [pl.BlockSpec((tm, D), lambda i: (jnp.minimum(i, st - 1), 0)),
                   pl.BlockSpec((tm, D), lambda i: (jnp.maximum(i - st, 0), 0))]
    else:
        st, xs, x_specs = None, [x], [row]
    return pl.pallas_call(
        functools.partial(_ln_route_body, split_tiles=st),
        grid=(m // tm,),
        in_specs=x_specs + [row, vec, vec, wmat, wmat, one],
        out_specs=[row, pl.BlockSpec((tm, D // 2), lambda i: (i, 0)), pl.BlockSpec((tm, 128), lambda i: (i, 0)),
                   pl.BlockSpec((8, tm), lambda i: (0, i)), one],
        out_shape=[jax.ShapeDtypeStruct((m, D), F32), jax.ShapeDtypeStruct((m, D // 2), jnp.uint32),
                   jax.ShapeDtypeStruct((m, 128), F32), jax.ShapeDtypeStruct((8, m), F32),
                   jax.ShapeDtypeStruct((1, 128), F32)],
        compiler_params=_cparams(("arbitrary",)),
        name="ln_route",
    )(*xs, h, g.reshape(1, D), b.reshape(1, D), wh, wl, br)


def _ln_combine_body(x_ref, y0_ref, y1_ref, r_ref, g_ref, b_ref, o_ref, ob_ref, *, split_tiles):
    r = r_ref[...]
    a_lo, a_hi = _unpack_halves(y0_ref[...])
    b_lo, b_hi = _unpack_halves(y1_ref[...])
    w0, w1 = r[:, 2:3], r[:, 3:4]
    f = jnp.concatenate([w0 * a_lo + w1 * b_lo, w0 * a_hi + w1 * b_hi], axis=1)
    y = _layer_norm(DN_ALPHA * x_ref[...] + f, g_ref[...], b_ref[...])
    if split_tiles is None:
        o_ref[...] = y
        ob_ref[...] = y.astype(BF16)
    else:
        @pl.when(pl.program_id(0) < split_tiles)
        def _():
            o_ref[...] = y

        @pl.when(pl.program_id(0) >= split_tiles)
        def _():
            ob_ref[...] = y


def ln_combine(x, y0, y1, route, g, b, tm=256, split=None):
    m = x.shape[0]
    row = pl.BlockSpec((tm, D), lambda i: (i, 0))
    half = pl.BlockSpec((tm, D // 2), lambda i: (i, 0))
    vec = pl.BlockSpec((1, D), lambda i: (0, 0))
    if split is None:
        st = None
        out_specs = [row, row]
        out_shape = [jax.ShapeDtypeStruct((m, D), F32), jax.ShapeDtypeStruct((m, D), BF16)]
    else:
        st = split // tm
        out_specs = [pl.BlockSpec((tm, D), lambda i: (jnp.minimum(i, st - 1), 0)),
                     pl.BlockSpec((tm, D), lambda i: (jnp.maximum(i - st, 0), 0))]
        out_shape = [jax.ShapeDtypeStruct((split, D), F32), jax.ShapeDtypeStruct((m - split, D), F32)]
    return pl.pallas_call(
        functools.partial(_ln_combine_body, split_tiles=st),
        grid=(m // tm,),
        in_specs=[row, half, half, pl.BlockSpec((tm, 128), lambda i: (i, 0)), vec, vec],
        out_specs=out_specs,
        out_shape=out_shape,
        compiler_params=_cparams(("arbitrary",)),
        name="ln_combine",
    )(x, y0, y1, route, g.reshape(1, D), b.reshape(1, D))


def _ssd_body(z_ref, x_ref, b_ref, c_ref, dtg_ref, dtt_ref, csx_ref, csb_ref, csc_ref, h0_ref,
              cwx_ref, cwb_ref, cwc_ref, cbx_ref, cbb_ref, cbc_ref, dtbr_ref, dtbc_ref,
              alr_ref, alc_ref, dsk_ref, nw_ref, *rest, C, aliased):
    if aliased:
        rest = rest[1:]
    (y_ref, cox_ref, cob_ref, coc_ref, ho_ref,
     tx_ref, tb_ref, tc_ref, ex_ref, eb_ref, ec_ref, ht_ref, yb_ref) = rest
    c = pl.program_id(1)
    last_chunk = c == pl.num_programs(1) - 1

    @pl.when(c == 0)
    def _init():
        tx_ref[...] = jnp.zeros_like(tx_ref)
        tb_ref[...] = jnp.zeros_like(tb_ref)
        tc_ref[...] = jnp.zeros_like(tc_ref)
        for g in range(SSD_G):
            tx_ref[g, 5:8, :] = csx_ref[0, g]
            tb_ref[g, 5:8, :] = csb_ref[0, g]
            tc_ref[g, 5:8, :] = csc_ref[0, g]
        for p in range(SSD_HEADS // 2):
            ht_ref[p] = h0_ref[0, p].T

    tril = _tri(C)
    tril_f = tril.astype(F32)
    triu_f = _tri(C, upper=True).astype(F32)
    lane = lax.broadcasted_iota(jnp.int32, (C, 128), 1)
    left = lane < 64

    def conv(e_ref, t_ref, raw, w_ref, bias_ref, g):
        e_ref[0:8, :] = t_ref[g]
        e_ref[8:8 + C, :] = raw
        w = w_ref[g]
        acc = bias_ref[g] + w[3:4, :] * raw
        for k in range(SSD_CONV - 1):
            acc = acc + w[k:k + 1, :] * e_ref[5 + k:5 + k + C, :]
        t_ref[g] = e_ref[C:C + 8, :]
        return _silu(acc)

    def group(g, carry):
        xs = conv(ex_ref, tx_ref, x_ref[g], cwx_ref, cbx_ref, g)
        bs = conv(eb_ref, tb_ref, b_ref[g], cwb_ref, cbb_ref, g)
        cs = conv(ec_ref, tc_ref, c_ref[g], cwc_ref, cbc_ref, g)

        @pl.when(last_chunk)
        def _():
            cox_ref[0, g] = ex_ref[C + 5:C + 8, :]
            cob_ref[0, g] = eb_ref[C + 5:C + 8, :]
            coc_ref[0, g] = ec_ref[C + 5:C + 8, :]

        dtv = _softplus(dtg_ref[0, g] + dtbr_ref[g])
        dtvt = _softplus(dtt_ref[0, g] + dtbc_ref[g])
        cum = _dot(tril_f, dtv * (-jnp.exp(alr_ref[g])), precision=HIGHEST)
        cumt = _dot(dtvt * (-jnp.exp(alc_ref[g])), triu_f, precision=HIGHEST)
        bsb = bs.astype(BF16)
        csb = cs.astype(BF16)
        cb = _dot_nt(csb, bsb)
        bst = _transpose_rows(bs).astype(BF16)
        dsk = dsk_ref[g]
        ys = []
        for j in range(SSD_HPG // 2):
            h0, h1 = 2 * j, 2 * j + 1
            c0, c1 = cum[:, h0:h0 + 1], cum[:, h1:h1 + 1]
            l0 = jnp.where(tril, jnp.exp(c0 - cumt[h0:h0 + 1, :]), 0.0) * cb
            l1 = jnp.where(tril, jnp.exp(c1 - cumt[h1:h1 + 1, :]), 0.0) * cb
            lhs = jnp.concatenate([l0, l1], axis=1).astype(BF16)
            xp = xs[:, 128 * j:128 * (j + 1)]
            xdt = xp * jnp.where(left, dtv[:, h0:h0 + 1], dtv[:, h1:h1 + 1])
            rhs = jnp.concatenate([jnp.where(left, xdt, 0.0), jnp.where(left, 0.0, xdt)],
                                  axis=0).astype(BF16)
            htp = ht_ref[g * 4 + j]
            y = _dot(lhs, rhs)
            y = y + _dot(csb, htp.astype(BF16)) * jnp.where(left, jnp.exp(c0), jnp.exp(c1))
            e0, e1 = cum[C - 1:C, h0:h0 + 1], cum[C - 1:C, h1:h1 + 1]
            wgt = (xdt * jnp.where(left, jnp.exp(e0 - c0), jnp.exp(e1 - c1))).astype(BF16)
            ht_ref[g * 4 + j] = jnp.where(left[0:1, :], jnp.exp(e0), jnp.exp(e1)) * htp + _dot(bst, wgt)
            ys.append(y + dsk[:, 128 * j:128 * (j + 1)] * xp)
        y = jnp.concatenate(ys, axis=1) * _silu(z_ref[g])
        ms = jnp.mean(y * y, axis=-1, keepdims=True)
        yb_ref[g] = (y * lax.rsqrt(ms + RMS_EPS) * nw_ref[g]).astype(BF16)
        return carry

    lax.fori_loop(0, SSD_G, group, 0)
    for g in range(SSD_G):
        y_ref[:, SSD_GW * g:SSD_GW * (g + 1)] = yb_ref[g]

    @pl.when(last_chunk)
    def _fin():
        for p in range(SSD_HEADS // 2):
            ho_ref[0, p] = ht_ref[p].T


def ssd_core(zx, bc, dt, conv_state, h0, params, *, row0, streams, length, chunk, y_prev=None):
    conv_w, conv_b, dt_bias, a_log, d_skip, norm_w = params
    S, L, C = streams, length, chunk
    nch = L // C
    rb0 = row0 // C
    t_all = zx.shape[1]
    dseg = dt[row0:row0 + S * L].reshape(S, L, SSD_G, SSD_HPG)
    dtg = dseg.transpose(0, 2, 1, 3)
    dtt = dseg.transpose(0, 2, 3, 1)

    def split(a, lead):
        ax = a[..., :SSD_INNER].reshape(lead + (SSD_G, SSD_GW))
        ab = a[..., SSD_INNER:SSD_INNER + SSD_G * SSD_N].reshape(lead + (SSD_G, SSD_N))
        ac = a[..., SSD_INNER + SSD_G * SSD_N:].reshape(lead + (SSD_G, SSD_N))
        return ax, ab, ac

    csx, csb, csc = (jnp.moveaxis(a, 2, 1) for a in split(conv_state, (S, SSD_CONV - 1)))
    cwx, cwb, cwc = (jnp.moveaxis(a, 1, 0) for a in split(conv_w, (SSD_CONV,)))
    cbx, cbb, cbc = (jnp.moveaxis(a, 1, 0) for a in split(conv_b.reshape(1, -1), (1,)))
    dtbr = dt_bias.reshape(SSD_G, 1, SSD_HPG)
    dtbc = dt_bias.reshape(SSD_G, SSD_HPG, 1)
    alr = a_log.reshape(SSD_G, 1, SSD_HPG)
    alc = a_log.reshape(SSD_G, SSD_HPG, 1)
    dsk = jnp.repeat(d_skip, SSD_INNER // SSD_HEADS).reshape(SSD_G, 1, SSD_GW)
    nw = norm_w.reshape(SSD_G, 1, SSD_GW)
    h0p = h0.reshape(S, SSD_HEADS // 2, 128, SSD_N)

    def rb(s, c):
        return rb0 + s * nch + c

    def full(a):
        nd = a.ndim
        return pl.BlockSpec(a.shape, lambda s, c: (0,) * nd)

    def per_stream(a):
        nd = a.ndim
        return pl.BlockSpec((1,) + a.shape[1:], lambda s, c: (s,) + (0,) * (nd - 1))

    in_specs = [
        pl.BlockSpec((SSD_G, C, SSD_GW), lambda s, c: (0, rb(s, c), 0)),
        pl.BlockSpec((SSD_G, C, SSD_GW), lambda s, c: (1, rb(s, c), 0)),
        pl.BlockSpec((SSD_G, C, SSD_N), lambda s, c: (0, rb(s, c), 0)),
        pl.BlockSpec((SSD_G, C, SSD_N), lambda s, c: (1, rb(s, c), 0)),
        pl.BlockSpec((1, SSD_G, C, SSD_HPG), lambda s, c: (s, 0, c, 0)),
        pl.BlockSpec((1, SSD_G, SSD_HPG, C), lambda s, c: (s, 0, 0, c)),
        per_stream(csx), per_stream(csb), per_stream(csc), per_stream(h0p),
        full(cwx), full(cwb), full(cwc), full(cbx), full(cbb), full(cbc),
        full(dtbr), full(dtbc), full(alr), full(alc), full(dsk), full(nw),
    ]
    args = [zx, zx, bc, bc, dtg, dtt, csx, csb, csc, h0p, cwx, cwb, cwc, cbx, cbb, cbc,
            dtbr, dtbc, alr, alc, dsk, nw]
    aliases = {}
    if y_prev is not None:
        in_specs.append(pl.BlockSpec(memory_space=pl.ANY))
        args.append(y_prev)
        aliases = {len(args) - 1: 0}
    out_shape = [
        jax.ShapeDtypeStruct((t_all, SSD_INNER), BF16),
        jax.ShapeDtypeStruct(csx.shape, F32), jax.ShapeDtypeStruct(csb.shape, F32),
        jax.ShapeDtypeStruct(csc.shape, F32), jax.ShapeDtypeStruct(h0p.shape, F32),
    ]
    out_specs = [
        pl.BlockSpec((C, SSD_INNER), lambda s, c: (rb(s, c), 0)),
        per_stream(csx), per_stream(csb), per_stream(csc), per_stream(h0p),
    ]
    scratch = [
        pltpu.VMEM((SSD_G, 8, SSD_GW), F32), pltpu.VMEM((SSD_G, 8, SSD_N), F32),
        pltpu.VMEM((SSD_G, 8, SSD_N), F32),
        pltpu.VMEM((C + 8, SSD_GW), F32), pltpu.VMEM((C + 8, SSD_N), F32), pltpu.VMEM((C + 8, SSD_N), F32),
        pltpu.VMEM((SSD_HEADS // 2, SSD_N, 128), F32),
        pltpu.VMEM((SSD_G, C, SSD_GW), BF16),
    ]
    y, cox, cob, coc, ho = pl.pallas_call(
        functools.partial(_ssd_body, C=C, aliased=y_prev is not None),
        grid=(S, nch), in_specs=in_specs, out_specs=out_specs, out_shape=out_shape,
        scratch_shapes=scratch, input_output_aliases=aliases,
        compiler_params=_cparams(("arbitrary", "arbitrary")),
        name="ssd_core",
    )(*args)
    conv_out = jnp.concatenate([jnp.moveaxis(a, 1, 2).reshape(S, SSD_CONV - 1, -1) for a in (cox, cob, coc)],
                               axis=-1)
    return y, conv_out, ho.reshape(S, SSD_HEADS, SSD_INNER // SSD_HEADS, SSD_N)


def _swa_body(sink_ref, q_ref, pk_ref, pv_ref, kv_ref, *rest, QT, prev_valid, aliased):
    o_ref = rest[-1]
    i = pl.program_id(0)
    kvw = SWA_KVH * SWA_DH
    nback = WINDOW // CHUNK
    rb = min(QT, 2 * CHUNK)
    span = rb + WINDOW
    kf = jnp.concatenate([pk_ref[...], kv_ref[:, :kvw]], axis=0).astype(BF16)
    vf = jnp.concatenate([pv_ref[...], kv_ref[:, kvw:]], axis=0).astype(BF16)
    r2 = lax.broadcasted_iota(jnp.int32, (128, 128), 0)
    c2 = lax.broadcasted_iota(jnp.int32, (128, 128), 1)
    swap = jnp.where((r2 + SWA_DH) % 128 == c2, 1.0, 0.0).astype(BF16)
    left = lax.broadcasted_iota(jnp.int32, (1, 128), 1) < SWA_DH
    k_side, v_side = [], []
    for kh in range(SWA_KVH):
        blk = slice(128 * (kh // 2), 128 * (kh // 2 + 1))
        mine = left if kh % 2 == 0 else ~left
        kb, vb = kf[:, blk], jnp.where(mine, vf[:, blk], 0.0).astype(BF16)
        ko, vo = _dot(kb, swap).astype(BF16), _dot(vb, swap).astype(BF16)
        k_side.append((kb, ko) if kh % 2 == 0 else (ko, kb))
        v_side.append((vb, vo) if kh % 2 == 0 else (vo, vb))
    scale = SWA_DH ** -0.5
    for b in range(QT // rb):
        rows = slice(rb * b, rb * (b + 1))
        keys = slice(rb * b, rb * b + span)
        qc = lax.broadcasted_iota(jnp.int32, (rb, span), 0) // CHUNK
        kc = lax.broadcasted_iota(jnp.int32, (rb, span), 1) // CHUNK
        ok = (kc >= qc) & (kc <= qc + nback)
        if not prev_valid and rb * b < WINDOW:
            ok = ok & ((kc + (rb // CHUNK) * b >= nback) | (i > 0))
        for pr in range(SWA_QH // 2):
            kh = 2 * pr // SWA_GRP
            q2 = q_ref[rows, 128 * pr:128 * (pr + 1)] * scale
            acc = None
            for side in range(2):
                qm = jnp.where(left if side == 0 else ~left, q2, 0.0).astype(BF16)
                s = jnp.where(ok, _dot_nt(qm, k_side[kh][side][keys]), NEG)
                sink = sink_ref[2 * pr + side]
                m = jnp.maximum(jnp.max(s, axis=-1, keepdims=True), sink)
                p = jnp.exp(s - m)
                den = jnp.sum(p, axis=-1, keepdims=True) + jnp.exp(sink - m)
                o = _dot(p.astype(BF16), v_side[kh][side][keys]) / den
                acc = o if acc is None else acc + o
            o_ref[rows, 128 * pr:128 * (pr + 1)] = acc.astype(BF16)


def swa_core(q, kv, prev_k, prev_v, sinks, *, row0, tiles, qt, prompt, o_prev=None):
    t_all = q.shape[0]
    rb0 = row0 // qt
    kvw = SWA_KVH * SWA_DH
    if prompt:
        wpt = qt // WINDOW
        prev_map_k = lambda i, s: (jnp.maximum(wpt * (rb0 + i) - 1, 0), 0)
        prev_map_v = lambda i, s: (jnp.maximum(wpt * (rb0 + i) - 1, 0), 1)
        pk_spec = pl.BlockSpec((WINDOW, kvw), prev_map_k)
        pv_spec = pl.BlockSpec((WINDOW, kvw), prev_map_v)
        prev_k = prev_v = kv
    else:
        pk_spec = pl.BlockSpec((None, WINDOW, kvw), lambda i, s: (i, 0, 0))
        pv_spec = pl.BlockSpec((None, WINDOW, kvw), lambda i, s: (i, 0, 0))
    in_specs = [pl.BlockSpec((qt, D), lambda i, s: (rb0 + i, 0)), pk_spec, pv_spec,
                pl.BlockSpec((qt, 2 * kvw), lambda i, s: (rb0 + i, 0))]
    args = [sinks, q, prev_k, prev_v, kv]
    aliases = {}
    if o_prev is not None:
        in_specs.append(pl.BlockSpec(memory_space=pl.ANY))
        args.append(o_prev)
        aliases = {len(args) - 1: 0}
    return pl.pallas_call(
        functools.partial(_swa_body, QT=qt, prev_valid=not prompt, aliased=o_prev is not None),
        grid_spec=pltpu.PrefetchScalarGridSpec(
            num_scalar_prefetch=1, grid=(tiles,), in_specs=in_specs,
            out_specs=pl.BlockSpec((qt, D), lambda i, s: (rb0 + i, 0))),
        out_shape=jax.ShapeDtypeStruct((t_all, D), BF16),
        input_output_aliases=aliases,
        compiler_params=_cparams(("arbitrary",)),
        name="swa_core",
    )(*args)


def _gelu_tanh(y):
    return 0.5 * y * (1.0 + jnp.tanh(0.7978845608028654 * (y + 0.044715 * y * y * y)))


def _s5_body(u_ref, wxr_ref, wxi_ref, wyr_ref, wyi_ref, kt_ref, a_ref, dsk_ref, hr0_ref, hi0_ref, *rest,
             S, R, aliased):
    if aliased:
        rest = rest[1:]
    (y_ref, hro_ref, hio_ref, wx_s, wy_s, ktm_s, xr_s, xi_s, pr_s, pi_s, hr_s, hi_s) = rest
    kb = pl.program_id(1)
    half = S5_CW * S5_N // S5_CH

    @pl.when((pl.program_id(0) == 0) & (kb == 0))
    def _zero():
        ktm_s[...] = jnp.zeros_like(ktm_s)

    @pl.when(kb == 0)
    def _build():
        own = (lax.broadcasted_iota(jnp.int32, (S5_CW, half), 0) // S5_CH
               == lax.broadcasted_iota(jnp.int32, (S5_CW, half), 1) // S5_N)
        for s in range(S5_SUB):
            rows = slice(S5_CW * s, S5_CW * (s + 1))
            wx_s[rows, 0:half] = jnp.where(own, jnp.concatenate([wxr_ref[s]] * 4, axis=1), 0.0).astype(BF16)
            wx_s[rows, half:2 * half] = jnp.where(own, jnp.concatenate([wxi_ref[s]] * 4, axis=1), 0.0).astype(BF16)
        own_t = (lax.broadcasted_iota(jnp.int32, (half, S5_CW), 0) // S5_N
                 == lax.broadcasted_iota(jnp.int32, (half, S5_CW), 1) // S5_CH)
        for t in range(S5_SUB):
            cols = slice(S5_CW * t, S5_CW * (t + 1))
            wy_s[0:half, cols] = jnp.where(own_t, jnp.concatenate([wyr_ref[t]] * 8, axis=0), 0.0).astype(BF16)
            wy_s[half:2 * half, cols] = jnp.where(own_t, jnp.concatenate([wyi_ref[t]] * 8, axis=0), 0.0).astype(BF16)
        same = (lax.broadcasted_iota(jnp.int32, (S5_CW, S5_CW), 0) // S5_CH
                == lax.broadcasted_iota(jnp.int32, (S5_CW, S5_CW), 1) // S5_CH)
        taps = [jnp.where(same, kt_ref[tau], 0.0).astype(BF16) for tau in range(S5_SUB)]
        for s in range(S5_SUB):
            for t in range(s, S5_SUB):
                ktm_s[S5_CW * s:S5_CW * (s + 1), S5_CW * t:S5_CW * (t + 1)] = taps[t - s]
        for st in range(S):
            hr_s[st] = hr0_ref[st]
            hi_s[st] = hi0_ref[st]

    kr = S * R
    ucat = jnp.concatenate([u_ref[pl.ds(s, kr, stride=S5_SUB), :] for s in range(S5_SUB)],
                           axis=1)
    ub = ucat.astype(BF16)
    x = _dot(ub, wx_s[...])
    xr_s[...] = x[:, :half]
    xi_s[...] = x[:, half:]
    ar = a_ref[0:1, :]
    ai = a_ref[1:2, :]
    for st in range(S):
        def step(k, carry):
            hr, hi = carry
            row = st * R + k
            pr_s[pl.ds(row, 1), :] = hr
            pi_s[pl.ds(row, 1), :] = hi
            nr = ar * hr - ai * hi + xr_s[pl.ds(row, 1), :]
            ni = ar * hi + ai * hr + xi_s[pl.ds(row, 1), :]
            return nr, ni

        hr, hi = lax.fori_loop(0, R, step, (hr_s[st], hi_s[st]))
        hr_s[st] = hr
        hi_s[st] = hi
    hprev = jnp.concatenate([pr_s[...], pi_s[...]], axis=1).astype(BF16)
    dsk = jnp.concatenate([dsk_ref[...]] * S5_SUB, axis=1)
    cb = 2 * S5_CW
    intra = jnp.concatenate([_dot(ub[:, :cb * (t + 1)], ktm_s[0:cb * (t + 1), cb * t:cb * (t + 1)])
                             for t in range(S5_SUB // 2)], axis=1)
    y = _gelu_tanh(intra + _dot(hprev, wy_s[...]) + dsk * ucat)
    for t in range(S5_SUB):
        y_ref[pl.ds(t, kr, stride=S5_SUB), :] = y[:, S5_CW * t:S5_CW * (t + 1)]

    @pl.when(kb == pl.num_programs(1) - 1)
    def _fin():
        for st in range(S):
            hro_ref[st] = hr_s[st]
            hio_ref[st] = hi_s[st]


def s5_tables(p):
    a_re, a_im, log_dt, b_re, b_im, c_re, c_im, d_skip = p
    lr, li = a_re.astype(F32), a_im.astype(F32)
    dt = jnp.exp(log_dt.astype(F32))[:, None]
    mag = jnp.exp(lr * dt)
    ab_r, ab_i = mag * jnp.cos(li * dt), mag * jnp.sin(li * dt)
    den = lr * lr + li * li
    co_r = ((ab_r - 1.0) * lr + ab_i * li) / den
    co_i = (ab_i * lr - (ab_r - 1.0) * li) / den
    bb_r = co_r[..., None] * b_re - co_i[..., None] * b_im
    bb_i = co_r[..., None] * b_im + co_i[..., None] * b_re
    pw_r, pw_i = [jnp.ones_like(ab_r)], [jnp.zeros_like(ab_i)]
    for _ in range(S5_SUB):
        r, i = pw_r[-1], pw_i[-1]
        pw_r.append(ab_r * r - ab_i * i)
        pw_i.append(ab_r * i + ab_i * r)
    pr, pi = jnp.stack(pw_r, 0), jnp.stack(pw_i, 0)
    er, ei = pr[S5_SUB - 1::-1][:S5_SUB], pi[S5_SUB - 1::-1][:S5_SUB]
    wx_r = er[..., None] * bb_r[None] - ei[..., None] * bb_i[None]
    wx_i = er[..., None] * bb_i[None] + ei[..., None] * bb_r[None]
    gb, gw = S5_G // S5_GB, S5_GB
    wx_r = wx_r.reshape(S5_SUB, gb, gw, S5_N, S5_CH).transpose(1, 0, 2, 4, 3).reshape(gb, S5_SUB, S5_CW, S5_N)
    wx_i = wx_i.reshape(S5_SUB, gb, gw, S5_N, S5_CH).transpose(1, 0, 2, 4, 3).reshape(gb, S5_SUB, S5_CW, S5_N)
    wxr = jnp.concatenate([wx_r, wx_r], axis=-1)
    wxi = jnp.concatenate([wx_i, wx_i], axis=-1)
    qr, qi = pr[1:], pi[1:]
    cr, ci = c_re.astype(F32), c_im.astype(F32)
    wy_r = cr[None] * qr[:, :, None, :] - ci[None] * qi[:, :, None, :]
    wy_i = -(cr[None] * qi[:, :, None, :] + ci[None] * qr[:, :, None, :])
    wyr = wy_r.reshape(S5_SUB, gb, gw, S5_CH, S5_N).transpose(1, 0, 4, 2, 3).reshape(gb, S5_SUB, S5_N, S5_CW)
    wyi = wy_i.reshape(S5_SUB, gb, gw, S5_CH, S5_N).transpose(1, 0, 4, 2, 3).reshape(gb, S5_SUB, S5_N, S5_CW)
    tr = pr[:S5_SUB, :, None, :] * cr[None] - pi[:S5_SUB, :, None, :] * ci[None]
    ti = pr[:S5_SUB, :, None, :] * ci[None] + pi[:S5_SUB, :, None, :] * cr[None]
    taps = jnp.einsum('agjn,gnk->agjk', tr, bb_r) - jnp.einsum('agjn,gnk->agjk', ti, bb_i)
    taps = taps.reshape(S5_SUB, gb, gw, S5_CH, S5_CH).transpose(1, 0, 2, 4, 3)
    kt = jnp.tile(taps.reshape(gb, S5_SUB, S5_CW, S5_CH), (1, 1, 1, gw))
    a16 = jnp.stack([pr[S5_SUB].reshape(gb, gw * S5_N), pi[S5_SUB].reshape(gb, gw * S5_N)], axis=1)
    dsk = d_skip.astype(F32).reshape(gb, 1, S5_CW)
    return wxr, wxi, wyr, wyi, kt, a16, dsk


def s5_core(u, s_re, s_im, tables, *, row0, streams, length, y_prev=None):
    wxr, wxi, wyr, wyi, kt, a16, dsk = tables
    S, L = streams, length
    gb = S5_G // S5_GB
    rows = S * L // S5_SUB
    if S == 1:
        kr = min(S5_KR, rows)
        spb, rps = 1, kr
    else:
        kr = rows
        spb, rps = S, L // S5_SUB
    nkb = rows // kr
    kb0 = row0 // S5_SUB // kr
    half = S5_GB * S5_N
    hr0 = s_re.reshape(S, gb, 1, half)
    hi0 = s_im.reshape(S, gb, 1, half)
    tab = lambda a: pl.BlockSpec((None,) + a.shape[1:], lambda p, k: (p,) + (0,) * (a.ndim - 1))
    st = pl.BlockSpec((S, None, 1, half), lambda p, k: (0, p, 0, 0))
    uspec = pl.BlockSpec((kr * S5_SUB, S5_CW), lambda p, k: (kb0 + k, p))
    in_specs = [uspec, tab(wxr), tab(wxi), tab(wyr), tab(wyi), tab(kt), tab(a16), tab(dsk), st, st]
    args = [u, wxr, wxi, wyr, wyi, kt, a16, dsk, hr0, hi0]
    aliases = {}
    if y_prev is not None:
        in_specs.append(pl.BlockSpec(memory_space=pl.ANY))
        args.append(y_prev)
        aliases = {len(args) - 1: 0}
    wide = S5_SUB * S5_CW
    y, hro, hio = pl.pallas_call(
        functools.partial(_s5_body, S=spb, R=rps, aliased=y_prev is not None),
        grid=(gb, nkb),
        in_specs=in_specs,
        out_specs=[uspec, st, st],
        out_shape=[jax.ShapeDtypeStruct(u.shape, F32), jax.ShapeDtypeStruct(hr0.shape, F32),
                   jax.ShapeDtypeStruct(hi0.shape, F32)],
        scratch_shapes=[pltpu.VMEM((wide, 2 * half), BF16), pltpu.VMEM((2 * half, wide), BF16),
                        pltpu.VMEM((wide, wide), BF16)]
        + [pltpu.VMEM((kr, half), F32) for _ in range(4)]
        + [pltpu.VMEM((spb, 1, half), F32) for _ in range(2)],
        input_output_aliases=aliases,
        compiler_params=_cparams(("arbitrary", "arbitrary")),
        name="s5_core",
    )(*args)
    return y, hro.reshape(S, S5_G, S5_N), hio.reshape(S, S5_G, S5_N)


def _hgrn_body(q_ref, f_ref, i_ref, g_ref, lbp_ref, nw_ref, s0_ref, *rest, C, layer, aliased):
    if aliased:
        rest = rest[1:]
    o_ref, so_ref, st_ref = rest
    c = pl.program_id(1)

    @pl.when(c == 0)
    def _init():
        for h in range(HG_H):
            st_ref[h] = s0_ref[0, h].T

    lbp = lbp_ref[...]
    e = jnp.exp(lbp - jnp.max(lbp, axis=0, keepdims=True))
    lbs = e / jnp.sum(e, axis=0, keepdims=True)
    lb = jnp.zeros((1, D), F32)
    for r in range(1, layer + 1):
        lb = lb + lbs[r:r + 1, :]
    fz = f_ref[...]
    log_sig = jnp.minimum(fz, 0.0) - jnp.log(1.0 + jnp.exp(-jnp.abs(fz)))
    la = jnp.log(lb)
    lbb = jnp.log(1.0 - lb) + log_sig
    mx = jnp.maximum(la, lbb)
    logf = mx + jnp.log(1.0 + jnp.exp(-jnp.abs(la - lbb)))
    kk = 1.0 - jnp.exp(logf)
    qs = _silu(q_ref[...])
    tril = _tri(C)
    cum = _dot(tril.astype(F32), logf, precision=HIGHEST)
    row = lax.broadcasted_iota(jnp.int32, (C, C), 0)
    col = lax.broadcasted_iota(jnp.int32, (C, C), 1)

    levels = []
    b = C
    while b >= HG_LEAF:
        nb = C // b
        ref = jnp.broadcast_to(cum.reshape(nb, b, D)[:, b // 2 - 1:b // 2, :], (nb, b, D)).reshape(C, D)
        ex = cum - ref
        same = (row // b) == (col // b)
        if b == HG_LEAF:
            mask = same & (col <= row)
            qe, ke = jnp.exp(jnp.minimum(ex, 80.0)), jnp.exp(jnp.minimum(-ex, 80.0))
        else:
            mask = same & ((row % b) >= b // 2) & ((col % b) < b // 2)
            e = jnp.exp(-jnp.abs(ex))
            qe, ke = jnp.where(ex <= 0.0, e, 1.0), jnp.where(ex <= 0.0, 1.0, e)
        levels.append((mask, (qs * qe).astype(BF16), (kk * ke).astype(BF16)))
        b //= 2

    last = cum[C - 1:C, :]
    qin = (qs * jnp.exp(cum)).astype(BF16)
    kin = (kk * jnp.exp(last - cum)).astype(BF16)
    dec = jnp.exp(last)
    vv = i_ref[...]
    vb = vv.astype(BF16)
    gate = _silu(g_ref[...])
    nw = nw_ref[...]
    outs = []
    for h in range(HG_H):
        sl = slice(HG_K * h, HG_K * (h + 1))
        att = jnp.zeros((C, C), F32)
        for mask, ql, kl in levels:
            att = att + jnp.where(mask, _dot_nt(ql[:, sl], kl[:, sl]), 0.0)
        st = st_ref[h]
        o = _dot(att.astype(BF16), vb[:, sl]) + _dot_nt(qin[:, sl], st.astype(BF16))
        st_ref[h] = st * dec[:, sl] + _dot(_transpose_rows(vv[:, sl]).astype(BF16), kin[:, sl])
        ms = jnp.mean(o * o, axis=-1, keepdims=True)
        outs.append(o * lax.rsqrt(ms + RMS_EPS) * nw)
    o_ref[...] = (jnp.concatenate(outs, axis=1) * gate).astype(BF16)

    @pl.when(c == pl.num_programs(1) - 1)
    def _fin():
        for h in range(HG_H):
            so_ref[0, h] = st_ref[h].T


def hgrn_core(qfig, lb_param, norm_w, s0, *, layer, row0, streams, length, chunk, o_prev=None):
    S, L, C = streams, length, chunk
    nch = L // C
    rb0 = row0 // C
    t_all = qfig.shape[0]
    rowspec = lambda j: pl.BlockSpec((C, D), lambda s, c: (rb0 + s * nch + c, j))
    in_specs = [rowspec(0), rowspec(1), rowspec(2), rowspec(3),
                pl.BlockSpec((DEPTH, D), lambda s, c: (0, 0)),
                pl.BlockSpec((1, HG_K), lambda s, c: (0, 0)),
                pl.BlockSpec((1, HG_H, HG_K, HG_K), lambda s, c: (s, 0, 0, 0))]
    args = [qfig, qfig, qfig, qfig, lb_param, norm_w.reshape(1, HG_K), s0]
    aliases = {}
    if o_prev is not None:
        in_specs.append(pl.BlockSpec(memory_space=pl.ANY))
        args.append(o_prev)
        aliases = {len(args) - 1: 0}
    return pl.pallas_call(
        functools.partial(_hgrn_body, C=C, layer=layer, aliased=o_prev is not None),
        grid=(S, nch), in_specs=in_specs,
        out_specs=[rowspec(0), pl.BlockSpec((1, HG_H, HG_K, HG_K), lambda s, c: (s, 0, 0, 0))],
        out_shape=[jax.ShapeDtypeStruct((t_all, D), BF16), jax.ShapeDtypeStruct(s0.shape, F32)],
        scratch_shapes=[pltpu.VMEM((HG_H, HG_K, HG_K), F32)],
        input_output_aliases=aliases,
        compiler_params=_cparams(("arbitrary", "arbitrary")),
        name="hgrn_core",
    )(*args)


def _expert_body(te_ref, nt_ref, x_ref, wg_ref, wu_ref, wd_ref, o_ref, wgb_ref, wub_ref, wdb_ref):
    i = pl.program_id(0)
    prev = te_ref[jnp.maximum(i - 1, 0)]

    @pl.when((i == 0) | (te_ref[i] != prev))
    def _():
        wgb_ref[...] = wg_ref[0, 0].astype(BF16)
        wub_ref[...] = wu_ref[0, 0].astype(BF16)
        wdb_ref[...] = wd_ref[0, 0].astype(BF16)

    @pl.when(i < nt_ref[0])
    def _():
        lo, hi = _unpack_halves(x_ref[...])
        lo, hi = lo.astype(BF16), hi.astype(BF16)
        k = D // 2
        gate = _dot(lo, wgb_ref[0:k, :]) + _dot(hi, wgb_ref[k:D, :])
        up = _dot(lo, wub_ref[0:k, :]) + _dot(hi, wub_ref[k:D, :])
        o_ref[...] = _pack_halves(_dot((_silu(gate) * up).astype(BF16), wdb_ref[...]))

    @pl.when(i >= nt_ref[0])
    def _():
        o_ref[...] = jnp.zeros_like(o_ref)


def expert_mlp(xs, tile_expert, n_tiles, w_gate, w_up, w_down, layer):
    p = xs.shape[0]
    nt = p // MOE_TM
    wspec = lambda shp: pl.BlockSpec((1, 1) + shp, lambda i, te, n: (layer, te[i], 0, 0))
    return pl.pallas_call(
        _expert_body,
        grid_spec=pltpu.PrefetchScalarGridSpec(
            num_scalar_prefetch=2, grid=(nt,),
            in_specs=[pl.BlockSpec((MOE_TM, D // 2), lambda i, te, n: (i, 0)),
                      wspec((D, D_EXPERT)), wspec((D, D_EXPERT)), wspec((D_EXPERT, D))],
            out_specs=pl.BlockSpec((MOE_TM, D // 2), lambda i, te, n: (i, 0)),
            scratch_shapes=[pltpu.VMEM((D, D_EXPERT), BF16), pltpu.VMEM((D, D_EXPERT), BF16),
                            pltpu.VMEM((D_EXPERT, D), BF16)]),
        out_shape=jax.ShapeDtypeStruct((p, D // 2), jnp.uint32),
        compiler_params=_cparams(("arbitrary",)),
        name="expert_mlp",
    )(tile_expert, n_tiles, xs, w_gate, w_up, w_down)


def moe_layer(x1, route_t, counts, w_gate, w_up, w_down, layer):
    t = x1.shape[0]
    e0 = route_t[0].astype(jnp.int32)
    e1 = route_t[1].astype(jnp.int32)
    cnt = counts[0, MOE_G:MOE_G + MOE_E].astype(jnp.int32)
    padded = (cnt + MOE_TM - 1) // MOE_TM * MOE_TM
    ends = jnp.cumsum(padded)
    starts = ends - padded
    d0 = starts[e0] + route_t[4].astype(jnp.int32)
    d1 = starts[e1] + route_t[5].astype(jnp.int32)
    p_rows = (2 * t + MOE_E * (MOE_TM - 1)) // 512 * 512 + 512
    nt = p_rows // MOE_TM
    tok = jnp.arange(t, dtype=jnp.int32)
    tile_start = jnp.arange(nt, dtype=jnp.int32) * MOE_TM
    tile_expert = jnp.minimum(jnp.sum((tile_start[:, None] >= ends[None, :]).astype(jnp.int32), axis=1),
                              MOE_E - 1).astype(jnp.int32)
    n_tiles = (ends[-1] // MOE_TM).astype(jnp.int32).reshape(1)
    _, by_pos = lax.sort_key_val(jnp.concatenate([d0, d1]), jnp.concatenate([tok, tok]))
    per_pos = lambda a: jnp.repeat(a[tile_expert], MOE_TM)
    pos = jnp.arange(p_rows, dtype=jnp.int32)
    off = pos - per_pos(starts)
    dense = jnp.minimum(per_pos(jnp.cumsum(cnt) - cnt) + off, 2 * t - 1)
    src = jnp.where(off < per_pos(cnt), by_pos.at[dense].get(mode="promise_in_bounds"), pos % t)
    xs = x1.at[src].get(mode="promise_in_bounds")
    ys = expert_mlp(xs, tile_expert, n_tiles, w_gate, w_up, w_down, layer)
    return ys.at[d0].get(mode="promise_in_bounds"), ys.at[d1].get(mode="promise_in_bounds")


def _forward(x_prompt, x_sample, states, ssd_p, swa_p, s5_p, hg_p, ln_p, moe_p):
    (state_ssd_conv, state_ssd, cache_k, cache_v, s5_re, s5_im, state_hgrn) = states
    lp = x_prompt.shape[1]
    sb, ls = x_sample.shape[0], x_sample.shape[1]
    t_all = lp + sb * ls
    ln1_g, ln1_b, ln2_g, ln2_b = ln_p
    w_rg, b_rg, w_re, b_re, w_gate, w_up, w_down = moe_p

    x = (x_prompt.reshape(lp, D), x_sample.reshape(sb * ls, D))
    xb = rows_to_bf16(*x)
    outs = {}
    for layer in range(DEPTH):
        kind = layer % 4
        if kind == 0:
            w_in, conv_w, conv_b, dt_bias, a_log, d_skip, norm_w, w_out = ssd_p
            zx = matmul(xb, w_in, col0=0, ncols=2 * SSD_INNER, tn=MM_TN, tiled_out=True, sub=SSD_GW,
                        name="mm_ssd_zx")
            bc = matmul(xb, w_in, col0=2 * SSD_INNER, ncols=2 * SSD_G * SSD_N, tn=MM_TN, tiled_out=True,
                        sub=SSD_N, name="mm_ssd_bc")
            dt = matmul(xb, w_in[:, SSD_INNER + SSD_XBC:], tn=SSD_HEADS, name="mm_ssd_dt")
            prm = (conv_w, conv_b, dt_bias, a_log, d_skip, norm_w)
            zc = jnp.zeros((1, SSD_CONV - 1, SSD_XBC), F32)
            zh = jnp.zeros((1, SSD_HEADS, SSD_INNER // SSD_HEADS, SSD_N), F32)
            y, pc, ph = ssd_core(zx, bc, dt, zc, zh, prm, row0=0, streams=1, length=lp, chunk=128)
            y, sc, sh = ssd_core(zx, bc, dt, state_ssd_conv, state_ssd, prm, row0=lp, streams=sb,
                                 length=ls, chunk=ls, y_prev=y)
            outs['conv'], outs['ssd'] = (pc, sc), (ph, sh)
            h = matmul(y, w_out, name="mm_ssd_out")
        elif kind == 1:
            w_qkv, sinks, w_out = swa_p
            kvw = SWA_KVH * SWA_DH
            q = matmul(xb, w_qkv, col0=0, ncols=D, out_dtype=BF16, name="mm_swa_q")
            kv = matmul(xb, w_qkv, col0=D, ncols=2 * kvw, name="mm_swa_kv")
            o = swa_core(q, kv, None, None, sinks, row0=0, tiles=lp // 256, qt=256, prompt=True)
            ck = cache_k.reshape(sb, WINDOW, kvw)
            cv = cache_v.reshape(sb, WINDOW, kvw)
            o = swa_core(q, kv, ck, cv, sinks, row0=lp, tiles=sb, qt=ls, prompt=False, o_prev=o)
            kshape = (SWA_KVH, SWA_DH)
            pk = kv[lp - WINDOW:lp, :kvw].reshape((1, WINDOW) + kshape)
            pv = kv[lp - WINDOW:lp, kvw:].reshape((1, WINDOW) + kshape)
            kvs = kv[lp:].reshape(sb, ls, 2 * kvw)
            sk = jnp.concatenate([ck, kvs[:, :, :kvw]], axis=1)[:, -WINDOW:].reshape((sb, WINDOW) + kshape)
            sv = jnp.concatenate([cv, kvs[:, :, kvw:]], axis=1)[:, -WINDOW:].reshape((sb, WINDOW) + kshape)
            outs['k'], outs['v'] = (pk, sk), (pv, sv)
            h = matmul(o, w_out, name="mm_swa_out")
        elif kind == 2:
            w_in, w_glu = s5_p[0], s5_p[-1]
            tables = s5_tables(s5_p[1:-1])
            u = matmul(xb, w_in, name="mm_s5_in")
            zs = jnp.zeros((1, S5_G, S5_N), F32)
            y, pr, pi = s5_core(u, zs, zs, tables, row0=0, streams=1, length=lp)
            y, sr, si = s5_core(u, s5_re, s5_im, tables, row0=lp, streams=sb, length=ls, y_prev=y)
            outs['s5r'], outs['s5i'] = (pr, sr), (pi, si)
            h = glu_matmul(y, w_glu)
        else:
            w_in, lb_param, norm_w, w_out = hg_p
            qfig = matmul(xb, w_in, name="mm_hg_in")
            zs = jnp.zeros((1, HG_H, HG_K, HG_K), F32)
            o, ps = hgrn_core(qfig, lb_param, norm_w, zs, layer=layer, row0=0, streams=1, length=lp, chunk=128)
            o, ss = hgrn_core(qfig, lb_param, norm_w, state_hgrn, layer=layer, row0=lp, streams=sb,
                              length=ls, chunk=ls, o_prev=o)
            outs['hg'] = (ps, ss)
            h = matmul(o, w_out, name="mm_hg_out")
        wr = jnp.concatenate([w_rg[layer], w_re[layer], jnp.zeros((D, 128 - MOE_G - MOE_E), F32)], axis=1)
        br = jnp.concatenate([b_rg[layer], b_re[layer], jnp.zeros((128 - MOE_G - MOE_E,), F32)]).reshape(1, 128)
        x1, x1p, route, route_t, counts = ln_route(x, h, ln1_g[layer], ln1_b[layer], wr, br)
        y0, y1 = moe_layer(x1p, route_t, counts, w_gate, w_up, w_down, layer)
        x, xb = ln_combine(x1, y0, y1, route, ln2_g[layer], ln2_b[layer],
                           split=lp if layer == DEPTH - 1 else None)
    y_prompt = x.reshape(1, lp, D)
    y_sample = xb.reshape(sb, ls, D)
    order = ('conv', 'ssd', 'k', 'v', 's5r', 's5i', 'hg')
    return (y_prompt, y_sample) + tuple(outs[k][0] for k in order) + tuple(outs[k][1] for k in order)


def kernel(x_prompt, x_sample, state_ssd_conv, state_ssd, cache_swa_k, cache_swa_v, state_s5_re, state_s5_im, state_hgrn, ssd_w_in, ssd_conv_w, ssd_conv_b, ssd_dt_bias, ssd_a_log, ssd_d, ssd_norm_w, ssd_w_out, swa_w_qkv, swa_sinks, swa_w_out, s5_w_in, s5_a_re, s5_a_im, s5_log_dt, s5_b_re, s5_b_im, s5_c_re, s5_c_im, s5_d, s5_w_glu, hg_w_in, hg_lb, hg_norm_w, hg_w_out, ln1_g, ln1_b, ln2_g, ln2_b, moe_w_rg, moe_b_rg, moe_w_re, moe_b_re, moe_w_gate, moe_w_up, moe_w_down):
    states = (state_ssd_conv, state_ssd, cache_swa_k, cache_swa_v, state_s5_re, state_s5_im, state_hgrn)
    ssd_p = (ssd_w_in, ssd_conv_w, ssd_conv_b, ssd_dt_bias, ssd_a_log, ssd_d, ssd_norm_w, ssd_w_out)
    swa_p = (swa_w_qkv, swa_sinks, swa_w_out)
    s5_p = (s5_w_in, s5_a_re, s5_a_im, s5_log_dt, s5_b_re, s5_b_im, s5_c_re, s5_c_im, s5_d, s5_w_glu)
    hg_p = (hg_w_in, hg_lb, hg_norm_w, hg_w_out)
    ln_p = (ln1_g, ln1_b, ln2_g, ln2_b)
    moe_p = (moe_w_rg, moe_b_rg, moe_w_re, moe_b_re, moe_w_gate, moe_w_up, moe_w_down)
    return _forward(x_prompt, x_sample, states, ssd_p, swa_p, s5_p, hg_p, ln_p, moe_p)
```
